```python
import math
import jax, jax.numpy as jnp
from jax import lax
import numpy as np


D_MODEL = 1024
BATCH = 8
SEQ = 4096
DEPTH = 2

GRID_W = 64
CTX_LEN = 256
N_MIXERS = 2
N_MLSTM_LAYERS = (DEPTH + N_MIXERS - 1) // N_MIXERS
N_SSD_LAYERS = DEPTH // N_MIXERS
N_MOD = 9
D_FF = 2816
CONV_W = 5
CHUNK = 128
EPS = 1e-6
ML_INNER = 2 * D_MODEL
ML_HEADS = 4
ML_HEAD_DIM = ML_INNER // ML_HEADS
ML_BLOCK = 4
ML_NBLOCKS = ML_INNER // ML_BLOCK
SSD_INNER = 2 * D_MODEL
SSD_HEAD_DIM = 64
SSD_HEADS = SSD_INNER // SSD_HEAD_DIM
SSD_GROUPS = 8
SSD_HPG = SSD_HEADS // SSD_GROUPS
SSD_STATE = 128
SSD_CONV_DIM = SSD_INNER + 2 * SSD_GROUPS * SSD_STATE
SSD_IN_DIM = SSD_INNER + SSD_CONV_DIM + 2 * SSD_HEADS

kernel_name = 'hybrid_mlstm_ssd_prefix_dit'


def rmsnorm(x, g):
    xf = x.astype(jnp.float32)
    y = xf * lax.rsqrt(jnp.mean(xf * xf, axis=-1, keepdims=True) + EPS)
    return (y * g.astype(jnp.float32)).astype(x.dtype)


def head_layernorm(h):
    mu = jnp.mean(h, axis=-1, keepdims=True)
    var = jnp.mean(jnp.square(h - mu), axis=-1, keepdims=True)
    return (h - mu) * lax.rsqrt(var + EPS)


def modulate(h, shift, scale):
    return h * (1.0 + scale) + shift


def short_conv(u, w, b, rows):
    bsz, t, ch = u.shape
    row_len = t // rows
    k = w.shape[0]
    pad = k // 2
    up = jnp.pad(u.reshape(bsz, rows, row_len, ch), ((0, 0), (0, 0), (pad, pad), (0, 0)))
    out = b
    for j in range(k):
        out = out + up[:, :, j:j + row_len] * w[j]
    return out.reshape(bsz, t, ch)


def to_chunks(t):
    bsz, tl = t.shape[:2]
    return jnp.moveaxis(t.reshape(bsz, tl // CHUNK, CHUNK, *t.shape[2:]), 1, 0)


def from_chunks(t):
    t = jnp.moveaxis(t, 0, 1)
    return t.reshape(t.shape[0], t.shape[1] * t.shape[2], *t.shape[3:])


def swiglu(h, w_gate, w_up, w_down):
    return (jax.nn.silu(h @ w_gate) * (h @ w_up)) @ w_down


def ffn_sub(x, mod, j0, g_pre, g_post, w_gate, w_up, w_down):
    h = modulate(rmsnorm(x, g_pre), mod[:, :, j0], mod[:, :, j0 + 1])
    return x + 0.5 * mod[:, :, j0 + 2] * rmsnorm(swiglu(h, w_gate, w_up, w_down), g_post)


def mlstm_scan(state, q, k, v, ig, lf):
    tril = jnp.tril(jnp.ones((CHUNK, CHUNK), bool))[None, :, :, None]

    def step(carry, inp):
        cmat, nvec, m = carry
        qc, kc, vc, igc, lfc = inp
        b = jnp.cumsum(lfc, axis=1)
        dm = b[:, :, None, :] - b[:, None, :, :] + igc[:, None, :, :]
        dm = jnp.where(tril, dm, -jnp.inf)
        m_inter = b + m[:, None, :]
        m_t = jnp.maximum(m_inter, jnp.max(dm, axis=2))
        s = jnp.einsum('bthd,bshd->btsh', qc, kc) * jnp.exp(dm - m_t[:, :, None, :])
        sc = jnp.exp(m_inter - m_t)
        num = jnp.einsum('btsh,bshv->bthv', s, vc) + sc[..., None] * jnp.einsum('bthd,bhdv->bthv', qc, cmat)
        den = jnp.sum(s, axis=2) + sc * jnp.einsum('bthd,bhd->bth', qc, nvec)
        h = num / jnp.maximum(jnp.abs(den), jnp.exp(-m_t))[..., None]
        b_end = b[:, -1]
        gl = b_end[:, None] - b + igc
        m_new = jnp.maximum(b_end + m, jnp.max(gl, axis=1))
        wgt = jnp.exp(gl - m_new[:, None])
        dec = jnp.exp(b_end + m - m_new)
        c_new = dec[..., None, None] * cmat + jnp.einsum('bsh,bshd,bshv->bhdv', wgt, kc, vc)
        n_new = dec[..., None] * nvec + jnp.einsum('bsh,bshd->bhd', wgt, kc)
        return (c_new, n_new, m_new), h

    state, h = lax.scan(step, state, tuple(to_chunks(t) for t in (q, k, v, ig, lf)))
    return from_chunks(h), state


def ssd_scan(state, xs, dt, la, bm, cm):
    tril = jnp.tril(jnp.ones((CHUNK, CHUNK), bool))[None, :, :, None, None]

    def step(h, inp):
        xc, dtc, lac, bc, cc = inp
        cs = jnp.cumsum(lac, axis=1)
        seg = cs[:, :, None] - cs[:, None, :]
        decay = jnp.exp(jnp.where(tril, seg, -jnp.inf))
        cb = jnp.einsum('btgn,bsgn->btsg', cc, bc)
        mmat = cb[..., None] * decay * dtc[:, None]
        y = jnp.einsum('btsgr,bsgrp->btgrp', mmat, xc) \
            + jnp.exp(cs)[..., None] * jnp.einsum('btgn,bgrpn->btgrp', cc, h)
        wgt = jnp.exp(cs[:, -1:] - cs) * dtc
        h_new = jnp.exp(cs[:, -1])[..., None, None] * h + jnp.einsum('bsgr,bsgrp,bsgn->bgrpn', wgt, xc, bc)
        return h_new, y

    state, y = lax.scan(step, state, tuple(to_chunks(t) for t in (xs, dt, la, bm, cm)))
    return from_chunks(y), state


def scan_both_ways(scan_fn, init_state, ctx_in, lat_in, reverse):
    if reverse:
        ctx_in = tuple(jnp.flip(t, axis=1) for t in ctx_in)
        lat_in = tuple(jnp.flip(t, axis=1) for t in lat_in)
    h_c, st = scan_fn(init_state, *ctx_in)
    h_l, _ = scan_fn(st, *lat_in)
    if reverse:
        h_c, h_l = jnp.flip(h_c, axis=1), jnp.flip(h_l, axis=1)
    return h_c, h_l


def mlstm_mixer(hc, hl, rows, need_ctx, w_in, conv_w, conv_b, w_q, w_k, w_v,
                w_gates, b_gates, norm_g, skip, w_out):
    def prep(h, n_rows):
        bsz, t, _ = h.shape
        xm, z = jnp.split(h @ w_in, 2, axis=-1)
        xc = jax.nn.silu(short_conv(xm, conv_w, conv_b, n_rows))

        def blk(u, w):
            return jnp.einsum('btnc,ncd->btnd', u.reshape(bsz, t, ML_NBLOCKS, ML_BLOCK), w).reshape(bsz, t, ML_INNER)

        q, k, v = blk(xc, w_q), blk(xc, w_k), blk(xm, w_v)
        gates = (q @ w_gates[:ML_INNER] + k @ w_gates[ML_INNER:2 * ML_INNER]
                 + v @ w_gates[2 * ML_INNER:] + b_gates)
        gates = gates.astype(jnp.float32).reshape(bsz, t, 2, 2, ML_HEADS)

        def heads(u):
            return u.astype(jnp.float32).reshape(bsz, t, ML_HEADS, ML_HEAD_DIM)

        return xc, z, heads(q), heads(k) * (ML_HEAD_DIM ** -0.5), heads(v), gates

    xc_c, z_c, qc, kc, vc, gc = prep(hc, 1)
    xc_l, z_l, ql, kl, vl, gl = prep(hl, rows)
    bsz = hl.shape[0]
    init = (jnp.zeros((bsz, ML_HEADS, ML_HEAD_DIM, ML_HEAD_DIM), jnp.float32),
            jnp.zeros((bsz, ML_HEADS, ML_HEAD_DIM), jnp.float32),
            jnp.zeros((bsz, ML_HEADS), jnp.float32))
    h_c, h_l = None, None
    for d in range(2):
        ctx_in = (qc, kc, vc, gc[:, :, d, 0], jax.nn.log_sigmoid(gc[:, :, d, 1]))
        lat_in = (ql, kl, vl, gl[:, :, d, 0], jax.nn.log_sigmoid(gl[:, :, d, 1]))
        hcd, hld = scan_both_ways(mlstm_scan, init, ctx_in, lat_in, reverse=(d == 1))
        h_c = hcd if d == 0 else h_c + hcd
        h_l = hld if d == 0 else h_l + hld

    def readout(h, xc, z):
        b_, t_ = h.shape[:2]
        h = head_layernorm(h).reshape(b_, t_, ML_INNER).astype(xc.dtype) * norm_g + skip * xc
        return (h * jax.nn.silu(z)) @ w_out

    y_l = readout(h_l, xc_l, z_l)
    y_c = readout(h_c, xc_c, z_c) if need_ctx else None
    return y_c, y_l


def ssd_mixer(hc, hl, rows, need_ctx, w_in, conv_w, conv_b, dt_bias, a_log, d_skip, norm_g, w_out):
    gn = SSD_GROUPS * SSD_STATE

    def prep(h, n_rows):
        bsz, t, _ = h.shape
        proj = h @ w_in
        z = proj[..., :SSD_INNER]
        xbc = jax.nn.silu(short_conv(proj[..., SSD_INNER:SSD_INNER + SSD_CONV_DIM], conv_w, conv_b, n_rows))
        xbc = xbc.astype(jnp.float32)
        xs = xbc[..., :SSD_INNER].reshape(bsz, t, SSD_GROUPS, SSD_HPG, SSD_HEAD_DIM)
        bm = xbc[..., SSD_INNER:SSD_INNER + gn].reshape(bsz, t, SSD_GROUPS, SSD_STATE)
        cm = xbc[..., SSD_INNER + gn:].reshape(bsz, t, SSD_GROUPS, SSD_STATE)
        dt_raw = proj[..., SSD_INNER + SSD_CONV_DIM:].astype(jnp.float32).reshape(bsz, t, 2, SSD_GROUPS, SSD_HPG)
        return z, xs, bm, cm, dt_raw

    z_c, xs_c, bm_c, cm_c, dt_c = prep(hc, 1)
    z_l, xs_l, bm_l, cm_l, dt_l = prep(hl, rows)
    bsz = hl.shape[0]
    init = jnp.zeros((bsz, SSD_GROUPS, SSD_HPG, SSD_HEAD_DIM, SSD_STATE), jnp.float32)
    y_c, y_l = None, None
    for d in range(2):
        a = -jnp.exp(a_log[d].astype(jnp.float32)).reshape(SSD_GROUPS, SSD_HPG)
        bias = dt_bias[d].astype(jnp.float32).reshape(SSD_GROUPS, SSD_HPG)
        dtc = jax.nn.softplus(dt_c[:, :, d] + bias)
        dtl = jax.nn.softplus(dt_l[:, :, d] + bias)
        ctx_in = (xs_c, dtc, dtc * a, bm_c, cm_c)
        lat_in = (xs_l, dtl, dtl * a, bm_l, cm_l)
        ycd, yld = scan_both_ways(ssd_scan, init, ctx_in, lat_in, reverse=(d == 1))
        y_c = ycd if d == 0 else y_c + ycd
        y_l = yld if d == 0 else y_l + yld
    dsk = d_skip.astype(jnp.float32).reshape(SSD_GROUPS, SSD_HPG)[..., None]

    def readout(y, xs, z):
        b_, t_ = y.shape[:2]
        y = (y + dsk * xs).reshape(b_, t_, SSD_INNER).astype(z.dtype)
        return rmsnorm(y * jax.nn.silu(z), norm_g) @ w_out

    out_l = readout(y_l, xs_l, z_l)
    out_c = readout(y_c, xs_c, z_c) if need_ctx else None
    return out_c, out_l


def setup_inputs(seed: int = 0) -> dict:
    key = jax.random.key(seed)
    ks = iter(jax.random.split(key, 40))
    f32 = jnp.float32

    def nrm(shape, s):
        return s * jax.random.normal(next(ks), shape, f32)

    nm, ns = N_MLSTM_LAYERS, N_SSD_LAYERS
    x = nrm((BATCH, SEQ, D_MODEL), 1.0)
    c = nrm((BATCH, D_MODEL), 1.0)
    ctx = nrm((BATCH, CTX_LEN, D_MODEL), 1.0)
    c_ctx = nrm((D_MODEL,), 1.0)
    ada_w = nrm((DEPTH, D_MODEL, N_MOD * D_MODEL), 0.5 * D_MODEL ** -0.5)
    ada_b = nrm((DEPTH, N_MOD * D_MODEL), 0.02)
    norm_g = 1.0 + nrm((DEPTH, 6, D_MODEL), 0.02)
    ffn_w_gate = nrm((DEPTH, 2, D_MODEL, D_FF), D_MODEL ** -0.5)
    ffn_w_up = nrm((DEPTH, 2, D_MODEL, D_FF), D_MODEL ** -0.5)
    ffn_w_down = nrm((DEPTH, 2, D_FF, D_MODEL), D_FF ** -0.5)
    mlstm_w_in = nrm((nm, D_MODEL, 2 * ML_INNER), D_MODEL ** -0.5)
    mlstm_conv_w = nrm((nm, CONV_W, ML_INNER), CONV_W ** -0.5)
    mlstm_conv_b = nrm((nm, ML_INNER), 0.02)
    mlstm_w_q = nrm((nm, ML_NBLOCKS, ML_BLOCK, ML_BLOCK), ML_BLOCK ** -0.5)
    mlstm_w_k = nrm((nm, ML_NBLOCKS, ML_BLOCK, ML_BLOCK), ML_BLOCK ** -0.5)
    mlstm_w_v = nrm((nm, ML_NBLOCKS, ML_BLOCK, ML_BLOCK), ML_BLOCK ** -0.5)
    mlstm_w_gates = nrm((nm, 3 * ML_INNER, 4 * ML_HEADS), 0.01)
    i_bias = nrm((nm, 2, 1, ML_HEADS), 0.1)
    f_bias = jnp.linspace(3.0, 6.0, ML_HEADS, dtype=f32) + nrm((nm, 2, 1, ML_HEADS), 0.1)
    mlstm_b_gates = jnp.concatenate([i_bias, f_bias], axis=2).reshape(nm, 4 * ML_HEADS)
    mlstm_norm_g = 1.0 + nrm((nm, ML_INNER), 0.02)
    mlstm_skip = 1.0 + nrm((nm, ML_INNER), 0.02)
    mlstm_w_out = nrm((nm, ML_INNER, D_MODEL), ML_INNER ** -0.5)
    ssd_w_in = nrm((ns, D_MODEL, SSD_IN_DIM), D_MODEL ** -0.5)
    ssd_conv_w = nrm((ns, CONV_W, SSD_CONV_DIM), CONV_W ** -0.5)
    ssd_conv_b = nrm((ns, SSD_CONV_DIM), 0.02)
    dt0 = jnp.exp(jax.random.uniform(next(ks), (ns, 2, SSD_HEADS), f32, math.log(1e-3), math.log(1e-1)))
    ssd_dt_bias = dt0 + jnp.log(-jnp.expm1(-dt0))
    ssd_a_log = jnp.log(jax.random.uniform(next(ks), (ns, 2, SSD_HEADS), f32, 1.0, 16.0))
    ssd_d = 1.0 + nrm((ns, SSD_HEADS), 0.1)
    ssd_norm_g = 1.0 + nrm((ns, SSD_INNER), 0.02)
    ssd_w_out = nrm((ns, SSD_INNER, D_MODEL), SSD_INNER ** -0.5)
    return {'x': x, 'c': c, 'ctx': ctx, 'c_ctx': c_ctx,
            'ada_w': ada_w, 'ada_b': ada_b, 'norm_g': norm_g,
            'ffn_w_gate': ffn_w_gate, 'ffn_w_up': ffn_w_up, 'ffn_w_down': ffn_w_down,
            'mlstm_w_in': mlstm_w_in, 'mlstm_conv_w': mlstm_conv_w, 'mlstm_conv_b': mlstm_conv_b,
            'mlstm_w_q': mlstm_w_q, 'mlstm_w_k': mlstm_w_k, 'mlstm_w_v': mlstm_w_v,
            'mlstm_w_gates': mlstm_w_gates, 'mlstm_b_gates': mlstm_b_gates,
            'mlstm_norm_g': mlstm_norm_g, 'mlstm_skip': mlstm_skip, 'mlstm_w_out': mlstm_w_out,
            'ssd_w_in': ssd_w_in, 'ssd_conv_w': ssd_conv_w, 'ssd_conv_b': ssd_conv_b,
            'ssd_dt_bias': ssd_dt_bias, 'ssd_a_log': ssd_a_log, 'ssd_d': ssd_d,
            'ssd_norm_g': ssd_norm_g, 'ssd_w_out': ssd_w_out}


def reference(x, c, ctx, c_ctx, ada_w, ada_b, norm_g, ffn_w_gate, ffn_w_up, ffn_w_down,
              mlstm_w_in, mlstm_conv_w, mlstm_conv_b, mlstm_w_q, mlstm_w_k, mlstm_w_v,
              mlstm_w_gates, mlstm_b_gates, mlstm_norm_g, mlstm_skip, mlstm_w_out,
              ssd_w_in, ssd_conv_w, ssd_conv_b, ssd_dt_bias, ssd_a_log, ssd_d,
              ssd_norm_g, ssd_w_out):
    bsz = x.shape[0]
    rows = x.shape[1] // GRID_W
    sc = jax.nn.silu(c)
    scc = jax.nn.silu(c_ctx)
    for i in range(DEPTH):
        last = i == DEPTH - 1
        mod_l = (sc @ ada_w[i] + ada_b[i]).reshape(bsz, 1, N_MOD, D_MODEL)
        mod_c = (scc @ ada_w[i] + ada_b[i]).reshape(1, 1, N_MOD, D_MODEL)
        g = norm_g[i]
        x = ffn_sub(x, mod_l, 0, g[0], g[1], ffn_w_gate[i, 0], ffn_w_up[i, 0], ffn_w_down[i, 0])
        ctx = ffn_sub(ctx, mod_c, 0, g[0], g[1], ffn_w_gate[i, 0], ffn_w_up[i, 0], ffn_w_down[i, 0])
        hl = modulate(rmsnorm(x, g[2]), mod_l[:, :, 3], mod_l[:, :, 4])
        hc = modulate(rmsnorm(ctx, g[2]), mod_c[:, :, 3], mod_c[:, :, 4])
        j = i // N_MIXERS
        if i % N_MIXERS == 0:
            yc, yl = mlstm_mixer(hc, hl, rows, not last, mlstm_w_in[j], mlstm_conv_w[j], mlstm_conv_b[j],
                                 mlstm_w_q[j], mlstm_w_k[j], mlstm_w_v[j], mlstm_w_gates[j],
                                 mlstm_b_gates[j], mlstm_norm_g[j], mlstm_skip[j], mlstm_w_out[j])
        else:
            yc, yl = ssd_mixer(hc, hl, rows, not last, ssd_w_in[j], ssd_conv_w[j], ssd_conv_b[j],
                               ssd_dt_bias[j], ssd_a_log[j], ssd_d[j], ssd_norm_g[j], ssd_w_out[j])
        x = x + mod_l[:, :, 5] * rmsnorm(yl, g[3])
        x = ffn_sub(x, mod_l, 6, g[4], g[5], ffn_w_gate[i, 1], ffn_w_up[i, 1], ffn_w_down[i, 1])
        if not last:
            ctx = ctx + mod_c[:, :, 5] * rmsnorm(yc, g[3])
            ctx = ffn_sub(ctx, mod_c, 6, g[4], g[5], ffn_w_gate[i, 1], ffn_w_up[i, 1], ffn_w_down[i, 1])
    return x
```

```python
import functools

import jax
import jax.numpy as jnp
from jax import lax
from jax.experimental import pallas as pl
from jax.experimental.pallas import tpu as pltpu

F32 = jnp.float32
BF16 = jnp.bfloat16

EPS = 1e-6
GRID_W = 64
CONV_W = 5
N_MOD = 9
ML_HEADS = 4
ML_BLOCK = 4
SSD_GROUPS = 8
SSD_HPG = 4
SSD_HEAD_DIM = 64
SSD_STATE = 128

LANES = 128
MXU_DIM = 256
TOKEN_TILE = 256
SCAN_CHUNK = 128
FF_CHUNK = MXU_DIM
BD_CHUNK = MXU_DIM
MOD_ROWS = 16
VMEM_LIMIT_BYTES = 56 * 1024 * 1024


def _rms(x, g):
    return x * lax.rsqrt(jnp.mean(x * x, axis=-1, keepdims=True) + EPS) * g


def _silu(x):
    return x * jax.nn.sigmoid(x)


def _softplus(x):
    return jnp.maximum(x, 0.0) + jnp.log1p(jnp.exp(-jnp.abs(x)))


def _log_sigmoid(x):
    return -_softplus(-x)


def _dot(a, b):
    return jnp.dot(a, b, preferred_element_type=F32)


def _dot_nt(a, b):
    return lax.dot_general(a, b, (((1,), (1,)), ((), ())), preferred_element_type=F32)


def _dot_tn(a, b):
    return lax.dot_general(a, b, (((0,), (0,)), ((), ())), preferred_element_type=F32)


def _resident(arr):
    nd = arr.ndim
    return pl.BlockSpec(arr.shape, lambda *_: (0,) * nd, pipeline_mode=pl.Buffered(1))


def _params(sem):
    return pltpu.CompilerParams(dimension_semantics=sem, vmem_limit_bytes=VMEM_LIMIT_BYTES)


def _tile_specs(batch, tm, tile_off, ctx_tiles):
    def tok(width):
        return pl.BlockSpec((1, tm, width), lambda b, j: (b, j + tile_off, 0))

    def out(width):
        return pl.BlockSpec((1, tm, width), lambda b, j: (b, j, 0))

    def mod_spec(d_model):
        return pl.BlockSpec((1, N_MOD, d_model),
                            lambda b, j: (jnp.where(j + tile_off < ctx_tiles, batch, b), 0, 0))

    return tok, out, mod_spec


def _mod_kernel(c_ref, w_ref, b_ref, o_ref):
    sc = _silu(c_ref[...])
    o_ref[...] = jnp.dot(sc, w_ref[...], preferred_element_type=F32,
                         precision=lax.Precision.HIGHEST) + b_ref[...]


def _modulation(c_all, w, b):
    d_model, n_out = w.shape
    return pl.pallas_call(
        _mod_kernel,
        grid=(n_out // d_model,),
        in_specs=[pl.BlockSpec(c_all.shape, lambda n: (0, 0)),
                  pl.BlockSpec((d_model, d_model), lambda n: (0, n)),
                  pl.BlockSpec((1, d_model), lambda n: (0, n))],
        out_specs=pl.BlockSpec((c_all.shape[0], d_model), lambda n: (0, n)),
        out_shape=jax.ShapeDtypeStruct((c_all.shape[0], n_out), F32),
        compiler_params=_params(("arbitrary",)),
        name="modulation",
    )(c_all, w, b.reshape(1, n_out))


def _ffn_kernel(x_ref, mod_ref, gpre_ref, gpost_ref, wg_ref, wu_ref, wd_ref, o_ref, acc_ref, *, j0):
    x = x_ref[0]
    shift, scale, gate = mod_ref[0, j0:j0 + 1, :], mod_ref[0, j0 + 1:j0 + 2, :], mod_ref[0, j0 + 2:j0 + 3, :]
    h = (_rms(x, gpre_ref[...]) * (1.0 + scale) + shift).astype(BF16)
    for c in range(wg_ref.shape[0]):
        a = (_silu(_dot(h, wg_ref[c])) * _dot(h, wu_ref[c])).astype(BF16)
        contrib = _dot(a, wd_ref[c])
        if c == 0:
            acc_ref[...] = contrib
        else:
            acc_ref[...] += contrib
    o_ref[0] = x + 0.5 * gate * _rms(acc_ref[...], gpost_ref[...])


def _ffn(xs, mod, g_pre, g_post, w_gate, w_up, w_down, *, j0, tm, tile_off, n_tiles, ctx_tiles):
    batch, _, d_model = xs.shape
    d_ff = w_gate.shape[1]
    n_chunks = d_ff // FF_CHUNK
    wg = w_gate.reshape(d_model, n_chunks, FF_CHUNK).transpose(1, 0, 2).astype(BF16)
    wu = w_up.reshape(d_model, n_chunks, FF_CHUNK).transpose(1, 0, 2).astype(BF16)
    wd = w_down.reshape(n_chunks, FF_CHUNK, d_model).astype(BF16)
    gpre, gpost = g_pre.reshape(1, d_model), g_post.reshape(1, d_model)
    tok, out, mod_spec = _tile_specs(batch, tm, tile_off, ctx_tiles)
    return pl.pallas_call(
        functools.partial(_ffn_kernel, j0=j0),
        grid=(batch, n_tiles),
        in_specs=[tok(d_model), mod_spec(d_model), _resident(gpre), _resident(gpost),
                  _resident(wg), _resident(wu), _resident(wd)],
        out_specs=out(d_model),
        out_shape=jax.ShapeDtypeStruct((batch, n_tiles * tm, d_model), F32),
        scratch_shapes=[pltpu.VMEM((tm, d_model), F32)],
        compiler_params=_params(("parallel", "parallel")),
        name="ffn",
    )(xs, mod, gpre, gpost, wg, wu, wd)


def _short_conv(u, cw_ref, cb_ref, sl, pos, row_len):
    tm = u.shape[0]
    pad = CONV_W // 2
    out = cb_ref[:, sl] + u * cw_ref[pad:pad + 1, sl]
    for j in range(CONV_W):
        d = j - pad
        if d == 0:
            continue
        shifted = pltpu.roll(u, shift=(-d) % tm, axis=0)
        ok = jnp.logical_and(pos + d >= 0, pos + d < row_len)
        out = out + jnp.where(ok, shifted, 0.0) * cw_ref[j:j + 1, sl]
    return out


def _row_position(tm, is_ctx):
    row_len = jnp.where(is_ctx, tm, GRID_W)
    pos = jnp.bitwise_and(lax.broadcasted_iota(jnp.int32, (tm, 1), 0), row_len - 1)
    return pos, row_len


def _block_diag(w):
    per = BD_CHUNK // ML_BLOCK
    n_tiles = w.shape[0] // per
    w = w.reshape(n_tiles, per, ML_BLOCK, ML_BLOCK)
    eye = jnp.eye(per, dtype=w.dtype)
    return jnp.einsum("cnij,nm->cnimj", w, eye).reshape(n_tiles, BD_CHUNK, BD_CHUNK).astype(BF16)


def _mlstm_in_kernel(x_ref, mod_ref, g_ref, win_ref, cw_ref, cb_ref, bdq_ref, bdk_ref, bdv_ref,
                     wgq_ref, wgk_ref, wgv_ref, bg_ref,
                     q_ref, k_ref, v_ref, xc_ref, z_ref, gates_ref, *, tile_off, ctx_tiles, kscale):
    tm = x_ref.shape[1]
    inner = q_ref.shape[2]
    is_ctx = pl.program_id(1) + tile_off < ctx_tiles
    pos, row_len = _row_position(tm, is_ctx)
    shift, scale = mod_ref[0, 3:4, :], mod_ref[0, 4:5, :]
    h = (_rms(x_ref[0], g_ref[...]) * (1.0 + scale) + shift).astype(BF16)
    gates = bg_ref[...]
    for c in range(inner // BD_CHUNK):
        sl = slice(c * BD_CHUNK, (c + 1) * BD_CHUNK)
        zsl = slice(inner + c * BD_CHUNK, inner + (c + 1) * BD_CHUNK)
        xm = _dot(h, win_ref[:, sl])
        z_ref[0, :, sl] = _dot(h, win_ref[:, zsl]).astype(BF16)
        xc = _silu(_short_conv(xm, cw_ref, cb_ref, sl, pos, row_len))
        xcb, xmb = xc.astype(BF16), xm.astype(BF16)
        xc_ref[0, :, sl] = xcb
        q = _dot(xcb, bdq_ref[c])
        k = _dot(xcb, bdk_ref[c])
        v = _dot(xmb, bdv_ref[c])
        qb, kb, vb = q.astype(BF16), k.astype(BF16), v.astype(BF16)
        q_ref[0, :, sl] = qb
        k_ref[0, :, sl] = (k * kscale).astype(BF16)
        v_ref[0, :, sl] = vb
        gates = gates + _dot(qb, wgq_ref[sl, :]) + _dot(kb, wgk_ref[sl, :]) + _dot(vb, wgv_ref[sl, :])
    gates_ref[0] = gates


def _mlstm_in(xs, mod, g, w_in, conv_w, conv_b, w_q, w_k, w_v, w_gates, b_gates, *, tm, ctx_tiles):
    batch, s_len, d_model = xs.shape
    inner = w_in.shape[1] // 2
    n_gates = w_gates.shape[1]
    win = w_in.astype(BF16)
    cb = conv_b.reshape(1, inner)
    bdq, bdk, bdv = _block_diag(w_q), _block_diag(w_k), _block_diag(w_v)
    wg = jnp.pad(w_gates, ((0, 0), (0, LANES - n_gates))).astype(BF16)
    wgq, wgk, wgv = wg[:inner], wg[inner:2 * inner], wg[2 * inner:]
    bg = jnp.pad(b_gates, (0, LANES - n_gates)).reshape(1, LANES)
    gg = g.reshape(1, d_model)
    tok, out, mod_spec = _tile_specs(batch, tm, 0, ctx_tiles)
    act = jax.ShapeDtypeStruct((batch, s_len, inner), BF16)
    kernel = functools.partial(_mlstm_in_kernel, tile_off=0, ctx_tiles=ctx_tiles,
                               kscale=(inner // ML_HEADS) ** -0.5)
    return pl.pallas_call(
        kernel,
        grid=(batch, s_len // tm),
        in_specs=[tok(d_model), mod_spec(d_model), _resident(gg), _resident(win), _resident(conv_w),
                  _resident(cb), _resident(bdq), _resident(bdk), _resident(bdv),
                  _resident(wgq), _resident(wgk), _resident(wgv), _resident(bg)],
        out_specs=[out(inner)] * 5 + [out(LANES)],
        out_shape=[act] * 5 + [jax.ShapeDtypeStruct((batch, s_len, LANES), F32)],
        compiler_params=_params(("parallel", "parallel")),
        name="mlstm_in",
    )(xs, mod, gg, win, conv_w, cb, bdq, bdk, bdv, wgq, wgk, wgv, bg)


def _chunk_masks(chunk, reverse):
    t_idx = lax.broadcasted_iota(jnp.int32, (chunk, chunk), 0)
    s_idx = lax.broadcasted_iota(jnp.int32, (chunk, chunk), 1)
    if reverse:
        return s_idx >= t_idx, t_idx >= s_idx
    return s_idx <= t_idx, t_idx <= s_idx


def _lane_pick(tile, idx):
    lane = lax.broadcasted_iota(jnp.int32, (1, tile.shape[1]), 1)
    return jnp.sum(jnp.where(lane == idx, tile, 0.0), axis=1, keepdims=True)


def _sublane_pick(tile, idx):
    sub = lax.broadcasted_iota(jnp.int32, (tile.shape[0], 1), 0)
    return jnp.sum(jnp.where(sub == idx, tile, 0.0), axis=0, keepdims=True)


def _mlstm_scan_kernel(qf_ref, kf_ref, vf_ref, gf_ref, qb_ref, kb_ref, vb_ref, gb_ref,
                       hf_ref, hb_ref, c_ref, n_ref, m_ref):
    chunk = qf_ref.shape[1]
    head = pl.program_id(1)

    @pl.when(pl.program_id(2) == 0)
    def _():
        c_ref[...] = jnp.zeros_like(c_ref)
        n_ref[...] = jnp.zeros_like(n_ref)
        m_ref[...] = jnp.zeros_like(m_ref)

    dirs = ((qf_ref, kf_ref, vf_ref, gf_ref, hf_ref), (qb_ref, kb_ref, vb_ref, gb_ref, hb_ref))
    for d, (q_ref, k_ref, v_ref, g_ref, o_ref) in enumerate(dirs):
        tri, tri_t = _chunk_masks(chunk, reverse=(d == 1))
        gates = g_ref[0]
        gates_t = gates.T
        i_lane = d * 2 * ML_HEADS + head
        f_lane = i_lane + ML_HEADS
        ig_col, ig_row = _lane_pick(gates, i_lane), _sublane_pick(gates_t, i_lane)
        lf_col = _log_sigmoid(_lane_pick(gates, f_lane))
        lf_row = _log_sigmoid(_sublane_pick(gates_t, f_lane))
        b_col = jnp.sum(jnp.where(tri, lf_row, 0.0), axis=1, keepdims=True)
        b_row = jnp.sum(jnp.where(tri_t, lf_col, 0.0), axis=0, keepdims=True)
        b_end = jnp.sum(lf_col, axis=0, keepdims=True)
        m_prev = m_ref[d, 0:1, 0:1]
        dm = jnp.where(tri, b_col - b_row + ig_row, -jnp.inf)
        m_inter = b_col + m_prev
        m_t = jnp.maximum(m_inter, jnp.max(dm, axis=1, keepdims=True))
        q, k, v = q_ref[0], k_ref[0], v_ref[0]
        s = _dot_nt(q, k) * jnp.exp(dm - m_t)
        sc = jnp.exp(m_inter - m_t)
        cmat, nvec = c_ref[d], n_ref[d]
        num = _dot(s.astype(BF16), v) + sc * _dot(q, cmat.astype(BF16))
        den = jnp.sum(s, axis=1, keepdims=True) + sc * jnp.sum(q.astype(F32) * nvec, axis=1, keepdims=True)
        o_ref[0] = num * (1.0 / jnp.maximum(jnp.abs(den), jnp.exp(-m_t)))
        gl = b_end - b_col + ig_col
        m_new = jnp.maximum(b_end + m_prev, jnp.max(gl, axis=0, keepdims=True))
        kw = k.astype(F32) * jnp.exp(gl - m_new)
        dec = jnp.exp(b_end + m_prev - m_new)
        c_ref[d] = dec * cmat + _dot_tn(kw.astype(BF16), v)
        n_ref[d] = dec * nvec + jnp.sum(kw, axis=0, keepdims=True)
        m_ref[d] = jnp.broadcast_to(m_new, m_ref.shape[1:])


def _scan_chunk_maps(n_chunks, ctx_chunks):
    def fwd(i):
        return i

    def bwd(i):
        return jnp.where(i < ctx_chunks, ctx_chunks - 1 - i, n_chunks - 1 + ctx_chunks - i)

    return fwd, bwd


def _mlstm_scan(q, k, v, gates, *, chunk, ctx_len):
    batch, s_len, inner = q.shape
    dh = inner // ML_HEADS
    n_chunks = s_len // chunk
    fwd, bwd = _scan_chunk_maps(n_chunks, ctx_len // chunk)

    def specs(order):
        head_spec = pl.BlockSpec((1, chunk, dh), lambda b, h, i: (b, order(i), h))
        gate_spec = pl.BlockSpec((1, chunk, LANES), lambda b, h, i: (b, order(i), 0))
        return head_spec, gate_spec

    hf_spec, gf_spec = specs(fwd)
    hb_spec, gb_spec = specs(bwd)
    out = jax.ShapeDtypeStruct((batch, s_len, inner), F32)
    return pl.pallas_call(
        _mlstm_scan_kernel,
        grid=(batch, ML_HEADS, n_chunks),
        in_specs=[hf_spec, hf_spec, hf_spec, gf_spec, hb_spec, hb_spec, hb_spec, gb_spec],
        out_specs=[hf_spec, hb_spec],
        out_shape=[out, out],
        scratch_shapes=[pltpu.VMEM((2, dh, dh), F32), pltpu.VMEM((2, 1, dh), F32),
                        pltpu.VMEM((2, 8, LANES), F32)],
        compiler_params=_params(("parallel", "parallel", "arbitrary")),
        name="mlstm_scan",
    )(q, k, v, gates, q, k, v, gates)


def _mlstm_out_kernel(x_ref, mod_ref, hf_ref, hb_ref, xc_ref, z_ref, ng_ref, sk_ref, wout_ref, g3_ref, o_ref):
    inner = hf_ref.shape[2]
    dh = inner // ML_HEADS
    y = None
    for hd in range(ML_HEADS):
        sl = slice(hd * dh, (hd + 1) * dh)
        hs = hf_ref[0, :, sl] + hb_ref[0, :, sl]
        cen = hs - jnp.mean(hs, axis=-1, keepdims=True)
        hn = cen * lax.rsqrt(jnp.mean(cen * cen, axis=-1, keepdims=True) + EPS)
        u = hn * ng_ref[:, sl] + sk_ref[:, sl] * xc_ref[0, :, sl].astype(F32)
        u = u * _silu(z_ref[0, :, sl].astype(F32))
        part = _dot(u.astype(BF16), wout_ref[sl, :])
        y = part if y is None else y + part
    o_ref[0] = x_ref[0] + mod_ref[0, 5:6, :] * _rms(y, g3_ref[...])


def _mlstm_out(xs, mod, hf, hb, xc, z, norm_g, skip, w_out, g3, *, tm, tile_off, n_tiles, ctx_tiles):
    batch, _, d_model = xs.shape
    inner = hf.shape[2]
    ng, sk, g3r = norm_g.reshape(1, inner), skip.reshape(1, inner), g3.reshape(1, d_model)
    wout = w_out.astype(BF16)
    tok, out, mod_spec = _tile_specs(batch, tm, tile_off, ctx_tiles)
    return pl.pallas_call(
        _mlstm_out_kernel,
        grid=(batch, n_tiles),
        in_specs=[tok(d_model), mod_spec(d_model), tok(inner), tok(inner), tok(inner), tok(inner),
                  _resident(ng), _resident(sk), _resident(wout), _resident(g3r)],
        out_specs=out(d_model),
        out_shape=jax.ShapeDtypeStruct((batch, n_tiles * tm, d_model), F32),
        compiler_params=_params(("parallel", "parallel")),
        name="mlstm_out",
    )(xs, mod, hf, hb, xc, z, ng, sk, wout, g3r)


def _ssd_in_kernel(x_ref, mod_ref, g_ref, wz_ref, wx_ref, wdt_ref, cw_ref, cb_ref, dtb_ref, alog_ref,
                   z_ref, xs_ref, bm_ref, cm_ref, dtla_ref, *, tile_off, ctx_tiles):
    tm = x_ref.shape[1]
    inner = xs_ref.shape[2]
    gn = bm_ref.shape[2]
    is_ctx = pl.program_id(1) + tile_off < ctx_tiles
    pos, row_len = _row_position(tm, is_ctx)
    shift, scale = mod_ref[0, 3:4, :], mod_ref[0, 4:5, :]
    h = (_rms(x_ref[0], g_ref[...]) * (1.0 + scale) + shift).astype(BF16)
    for c in range(inner // BD_CHUNK):
        sl = slice(c * BD_CHUNK, (c + 1) * BD_CHUNK)
        z_ref[0, :, sl] = _dot(h, wz_ref[:, sl]).astype(BF16)
    for c in range((inner + 2 * gn) // BD_CHUNK):
        sl = slice(c * BD_CHUNK, (c + 1) * BD_CHUNK)
        u = _dot(h, wx_ref[:, sl])
        xbc = _silu(_short_conv(u, cw_ref, cb_ref, sl, pos, row_len)).astype(BF16)
        lo = c * BD_CHUNK
        if lo < inner:
            xs_ref[0, :, sl] = xbc
        elif lo < inner + gn:
            bm_ref[0, :, lo - inner:lo - inner + BD_CHUNK] = xbc
        else:
            cm_ref[0, :, lo - inner - gn:lo - inner - gn + BD_CHUNK] = xbc
    dt = _softplus(_dot(h, wdt_ref[...]) + dtb_ref[...])
    lane = lax.broadcasted_iota(jnp.int32, (1, LANES), 1)
    dtla_ref[0] = jnp.where(lane < LANES // 2, dt, dt * -jnp.exp(alog_ref[...]))


def _ssd_in(xs, mod, g, w_in, conv_w, conv_b, dt_bias, a_log, *, tm, ctx_tiles):
    batch, s_len, d_model = xs.shape
    gn = SSD_GROUPS * SSD_STATE
    n_dt = dt_bias.size
    assert 2 * n_dt == LANES
    conv_dim = conv_w.shape[1]
    inner = conv_dim - 2 * gn
    wz = w_in[:, :inner].astype(BF16)
    wx = w_in[:, inner:inner + conv_dim].astype(BF16)
    wdt = w_in[:, inner + conv_dim:]
    wdt = jnp.concatenate([wdt, wdt], axis=1).astype(BF16)
    dtb = jnp.tile(dt_bias.reshape(1, n_dt), (1, 2))
    alog = jnp.tile(a_log.reshape(1, n_dt), (1, 2))
    cb = conv_b.reshape(1, conv_dim)
    gg = g.reshape(1, d_model)
    tok, out, mod_spec = _tile_specs(batch, tm, 0, ctx_tiles)
    kernel = functools.partial(_ssd_in_kernel, tile_off=0, ctx_tiles=ctx_tiles)
    return pl.pallas_call(
        kernel,
        grid=(batch, s_len // tm),
        in_specs=[tok(d_model), mod_spec(d_model), _resident(gg), _resident(wz), _resident(wx), _resident(wdt),
                  _resident(conv_w), _resident(cb), _resident(dtb), _resident(alog)],
        out_specs=[out(inner), out(inner), out(gn), out(gn), out(LANES)],
        out_shape=[jax.ShapeDtypeStruct((batch, s_len, inner), BF16),
                   jax.ShapeDtypeStruct((batch, s_len, inner), BF16),
                   jax.ShapeDtypeStruct((batch, s_len, gn), BF16),
                   jax.ShapeDtypeStruct((batch, s_len, gn), BF16),
                   jax.ShapeDtypeStruct((batch, s_len, LANES), F32)],
        compiler_params=_params(("parallel", "parallel")),
        name="ssd_in",
    )(xs, mod, gg, wz, wx, wdt, conv_w, cb, dtb, alog)


def _ssd_scan_kernel(xf_ref, bf_ref, cf_ref, df_ref, xb_ref, bb_ref, cb_ref, db_ref,
                     yf_ref, yb_ref, h_ref):
    chunk = xf_ref.shape[1]
    group = pl.program_id(1)
    n_dt = LANES // 2
    p = SSD_HEAD_DIM

    @pl.when(pl.program_id(2) == 0)
    def _():
        h_ref[...] = jnp.zeros_like(h_ref)

    dirs = ((xf_ref, bf_ref, cf_ref, df_ref, yf_ref), (xb_ref, bb_ref, cb_ref, db_ref, yb_ref))
    for d, (x_ref, b_ref, c_ref, d_ref, y_ref) in enumerate(dirs):
        tri, tri_t = _chunk_masks(chunk, reverse=(d == 1))
        dtla = d_ref[0]
        dtla_t = dtla.T
        bm, cm, xg = b_ref[0], c_ref[0], x_ref[0]
        cb = _dot_nt(cm, bm)
        hstate = h_ref[d]
        ch = _dot_nt(cm, hstate.astype(BF16))
        for r in range(SSD_HPG):
            idx = d * (n_dt // 2) + group * SSD_HPG + r
            hs = slice(r * p, (r + 1) * p)
            dt_col, la_col = _lane_pick(dtla, idx), _lane_pick(dtla, n_dt + idx)
            dt_row, la_row = _sublane_pick(dtla_t, idx), _sublane_pick(dtla_t, n_dt + idx)
            cs_col = jnp.sum(jnp.where(tri, la_row, 0.0), axis=1, keepdims=True)
            cs_row = jnp.sum(jnp.where(tri_t, la_col, 0.0), axis=0, keepdims=True)
            cs_end = jnp.sum(la_col, axis=0, keepdims=True)
            mmat = cb * jnp.exp(jnp.where(tri, cs_col - cs_row, -jnp.inf)) * dt_row
            xr = xg[:, hs]
            y_ref[0, :, hs] = _dot(mmat.astype(BF16), xr) + jnp.exp(cs_col) * ch[:, hs]
            xw = xr.astype(F32) * (jnp.exp(cs_end - cs_col) * dt_col)
            h_ref[d, hs, :] = jnp.exp(cs_end) * hstate[hs, :] + _dot_tn(xw.astype(BF16), bm)


def _ssd_scan(xs_in, bm, cm, dtla, *, chunk, ctx_len):
    batch, s_len, inner = xs_in.shape
    gw = SSD_HPG * SSD_HEAD_DIM
    n_chunks = s_len // chunk
    fwd, bwd = _scan_chunk_maps(n_chunks, ctx_len // chunk)

    def specs(order):
        x_spec = pl.BlockSpec((1, chunk, gw), lambda b, g, i: (b, order(i), g))
        n_spec = pl.BlockSpec((1, chunk, SSD_STATE), lambda b, g, i: (b, order(i), g))
        d_spec = pl.BlockSpec((1, chunk, LANES), lambda b, g, i: (b, order(i), 0))
        return x_spec, n_spec, d_spec

    xf, nf, df = specs(fwd)
    xb, nb, db = specs(bwd)
    out = jax.ShapeDtypeStruct((batch, s_len, inner), F32)
    return pl.pallas_call(
        _ssd_scan_kernel,
        grid=(batch, SSD_GROUPS, n_chunks),
        in_specs=[xf, nf, nf, df, xb, nb, nb, db],
        out_specs=[xf, xb],
        out_shape=[out, out],
        scratch_shapes=[pltpu.VMEM((2, gw, SSD_STATE), F32)],
        compiler_params=_params(("parallel", "parallel", "arbitrary")),
        name="ssd_scan",
    )(xs_in, bm, cm, dtla, xs_in, bm, cm, dtla)


def _ssd_out_kernel(x_ref, mod_ref, yf_ref, yb_ref, xs_ref, z_ref, dsk_ref, ng_ref, wout_ref, g3_ref, o_ref):
    y = yf_ref[0] + yb_ref[0] + dsk_ref[...] * xs_ref[0].astype(F32)
    u = _rms(y * _silu(z_ref[0].astype(F32)), ng_ref[...])
    out = _dot(u.astype(BF16), wout_ref[...])
    o_ref[0] = x_ref[0] + mod_ref[0, 5:6, :] * _rms(out, g3_ref[...])


def _ssd_out(xs, mod, yf, yb, xs_in, z, d_skip, norm_g, w_out, g3, *, tm, tile_off, n_tiles, ctx_tiles):
    batch, _, d_model = xs.shape
    inner = yf.shape[2]
    dsk = jnp.repeat(d_skip, SSD_HEAD_DIM).reshape(1, inner)
    ng, g3r = norm_g.reshape(1, inner), g3.reshape(1, d_model)
    wout = w_out.astype(BF16)
    tok, out, mod_spec = _tile_specs(batch, tm, tile_off, ctx_tiles)
    return pl.pallas_call(
        _ssd_out_kernel,
        grid=(batch, n_tiles),
        in_specs=[tok(d_model), mod_spec(d_model), tok(inner), tok(inner), tok(inner), tok(inner),
                  _resident(dsk), _resident(ng), _resident(wout), _resident(g3r)],
        out_specs=out(d_model),
        out_shape=jax.ShapeDtypeStruct((batch, n_tiles * tm, d_model), F32),
        compiler_params=_params(("parallel", "parallel")),
        name="ssd_out",
    )(xs, mod, yf, yb, xs_in, z, dsk, ng, wout, g3r)


def _forward(x, c, ctx, c_ctx, ada_w, ada_b, norm_g, ffn_w_gate, ffn_w_up, ffn_w_down,
             mlstm_w_in, mlstm_conv_w, mlstm_conv_b, mlstm_w_q, mlstm_w_k, mlstm_w_v,
             mlstm_w_gates, mlstm_b_gates, mlstm_norm_g, mlstm_skip, mlstm_w_out,
             ssd_w_in, ssd_conv_w, ssd_conv_b, ssd_dt_bias, ssd_a_log, ssd_d,
             ssd_norm_g, ssd_w_out, *, tm, chunk):
    batch, seq, d_model = x.shape
    ctx_len = ctx.shape[1]
    depth = ada_w.shape[0]
    assert ctx_len == tm and seq % tm == 0 and tm % GRID_W == 0 and tm % chunk == 0
    assert batch < MOD_ROWS
    xs = jnp.concatenate([ctx, x], axis=1)
    ctx_tiles = ctx_len // tm
    c_all = jnp.zeros((MOD_ROWS, d_model), F32).at[:batch].set(c).at[batch].set(c_ctx)
    for i in range(depth):
        last = i == depth - 1
        mod = _modulation(c_all, ada_w[i], ada_b[i]).reshape(MOD_ROWS, N_MOD, d_model)
        g = norm_g[i]
        n_tiles = xs.shape[1] // tm
        xs = _ffn(xs, mod, g[0], g[1], ffn_w_gate[i, 0], ffn_w_up[i, 0], ffn_w_down[i, 0],
                  j0=0, tm=tm, tile_off=0, n_tiles=n_tiles, ctx_tiles=ctx_tiles)
        out_off = ctx_tiles if last else 0
        out_kw = dict(tm=tm, tile_off=out_off, n_tiles=n_tiles - out_off, ctx_tiles=ctx_tiles)
        j = i // 2
        if i % 2 == 0:
            q, k, v, xc, z, gates = _mlstm_in(xs, mod, g[2], mlstm_w_in[j], mlstm_conv_w[j], mlstm_conv_b[j],
                                              mlstm_w_q[j], mlstm_w_k[j], mlstm_w_v[j], mlstm_w_gates[j],
                                              mlstm_b_gates[j], tm=tm, ctx_tiles=ctx_tiles)
            hf, hb = _mlstm_scan(q, k, v, gates, chunk=chunk, ctx_len=ctx_len)
            xs = _mlstm_out(xs, mod, hf, hb, xc, z, mlstm_norm_g[j], mlstm_skip[j], mlstm_w_out[j], g[3], **out_kw)
        else:
            z, xs_in, bm, cm, dtla = _ssd_in(xs, mod, g[2], ssd_w_in[j], ssd_conv_w[j], ssd_conv_b[j],
                                             ssd_dt_bias[j], ssd_a_log[j], tm=tm, ctx_tiles=ctx_tiles)
            yf, yb = _ssd_scan(xs_in, bm, cm, dtla, chunk=chunk, ctx_len=ctx_len)
            xs = _ssd_out(xs, mod, yf, yb, xs_in, z, ssd_d[j], ssd_norm_g[j], ssd_w_out[j], g[3], **out_kw)
        if last:
            ctx_tiles = 0
        xs = _ffn(xs, mod, g[4], g[5], ffn_w_gate[i, 1], ffn_w_up[i, 1], ffn_w_down[i, 1],
                  j0=6, tm=tm, tile_off=0, n_tiles=xs.shape[1] // tm, ctx_tiles=ctx_tiles)
    return xs


def kernel(x, c, ctx, c_ctx, ada_w, ada_b, norm_g, ffn_w_gate, ffn_w_up, ffn_w_down, mlstm_w_in, mlstm_conv_w, mlstm_conv_b, mlstm_w_q, mlstm_w_k, mlstm_w_v, mlstm_w_gates, mlstm_b_gates, mlstm_norm_g, mlstm_skip, mlstm_w_out, ssd_w_in, ssd_conv_w, ssd_conv_b, ssd_dt_bias, ssd_a_log, ssd_d, ssd_norm_g, ssd_w_out):
    return _forward(x, c, ctx, c_ctx, ada_w, ada_b, norm_g, ffn_w_gate, ffn_w_up, ffn_w_down,
                    mlstm_w_in, mlstm_conv_w, mlstm_conv_b, mlstm_w_q, mlstm_w_k, mlstm_w_v,
                    mlstm_w_gates, mlstm_b_gates, mlstm_norm_g, mlstm_skip, mlstm_w_out,
                    ssd_w_in, ssd_conv_w, ssd_conv_b, ssd_dt_bias, ssd_a_log, ssd_d,
                    ssd_norm_g, ssd_w_out, tm=TOKEN_TILE, chunk=SCAN_CHUNK)
```

```python
import functools

import jax
import jax.numpy as jnp
from jax import lax
from jax.experimental import pallas as pl
from jax.experimental.pallas import tpu as pltpu

F32 = jnp.float32
BF16 = jnp.bfloat16

EPS = 1e-6
LOG2_E = 1.4426950408889634
GRID_W = 64
CONV_W = 5
N_MOD = 9
ML_HEADS = 4
ML_BLOCK = 4
SSD_GROUPS = 8
SSD_HPG = 4
SSD_HEAD_DIM = 64
SSD_STATE = 128

LANES = 128
MXU_DIM = 256
TOKEN_TILE = 256
MLSTM_CHUNK = 256
SSD_CHUNK = 128
FFN_TILES = 2
FF_CHUNK = MXU_DIM
BD_CHUNK = MXU_DIM
MOD_ROWS = 16
VMEM_LIMIT_BYTES = 56 * 1024 * 1024


def _rms(x, g):
    return x * lax.rsqrt(jnp.mean(x * x, axis=-1, keepdims=True) + EPS) * g


def _silu(x):
    return x * jax.nn.sigmoid(x)


def _softplus(x):
    return jnp.maximum(x, 0.0) + jnp.log1p(jnp.exp(-jnp.abs(x)))


def _log_sigmoid(x):
    return -_softplus(-x)


def _dot(a, b):
    return jnp.dot(a, b, preferred_element_type=F32)


def _dot_nt(a, b):
    return lax.dot_general(a, b, (((1,), (1,)), ((), ())), preferred_element_type=F32)


def _dot_tn(a, b):
    return lax.dot_general(a, b, (((0,), (0,)), ((), ())), preferred_element_type=F32)


def _resident(arr):
    nd = arr.ndim
    return pl.BlockSpec(arr.shape, lambda *_: (0,) * nd, pipeline_mode=pl.Buffered(1))


def _params(sem):
    return pltpu.CompilerParams(dimension_semantics=sem, vmem_limit_bytes=VMEM_LIMIT_BYTES)


def _tile_specs(batch, tm, tile_off, ctx_tiles):
    def tok(width):
        return pl.BlockSpec((1, tm, width), lambda b, j: (b, j + tile_off, 0))

    def out(width):
        return pl.BlockSpec((1, tm, width), lambda b, j: (b, j, 0))

    def mod_spec(d_model):
        return pl.BlockSpec((1, N_MOD, d_model),
                            lambda b, j: (jnp.where(j + tile_off < ctx_tiles, batch, b), 0, 0))

    return tok, out, mod_spec


def _mod_kernel(c_ref, w_ref, b_ref, o_ref):
    sc = _silu(c_ref[...])
    o_ref[...] = jnp.dot(sc, w_ref[...], preferred_element_type=F32,
                         precision=lax.Precision.HIGHEST) + b_ref[...]


def _modulation(c_all, w, b):
    d_model, n_out = w.shape
    return pl.pallas_call(
        _mod_kernel,
        grid=(n_out // d_model,),
        in_specs=[pl.BlockSpec(c_all.shape, lambda n: (0, 0)),
                  pl.BlockSpec((d_model, d_model), lambda n: (0, n)),
                  pl.BlockSpec((1, d_model), lambda n: (0, n))],
        out_specs=pl.BlockSpec((c_all.shape[0], d_model), lambda n: (0, n)),
        out_shape=jax.ShapeDtypeStruct((c_all.shape[0], n_out), F32),
        compiler_params=_params(("arbitrary",)),
        name="modulation",
    )(c_all, w, b.reshape(1, n_out))


def _ffn_kernel(x_ref, *refs, j0):
    mod_refs = refs[:FFN_TILES]
    gpre_ref, gpost_ref, wg_ref, wu_ref, wd_ref, o_ref, h_ref, acc_ref = refs[FFN_TILES:]
    tm = x_ref.shape[0] // FFN_TILES
    for t, mod_ref in enumerate(mod_refs):
        rows = slice(t * tm, (t + 1) * tm)
        shift, scale = mod_ref[0, j0:j0 + 1, :], mod_ref[0, j0 + 1:j0 + 2, :]
        h_ref[rows, :] = (_rms(x_ref[rows, :], gpre_ref[...]) * (1.0 + scale) + shift).astype(BF16)
    h = h_ref[...]
    for c in range(wg_ref.shape[0]):
        a = (_silu(_dot(h, wg_ref[c])) * _dot(h, wu_ref[c])).astype(BF16)
        contrib = _dot(a, wd_ref[c])
        if c == 0:
            acc_ref[...] = contrib
        else:
            acc_ref[...] += contrib
    for t, mod_ref in enumerate(mod_refs):
        rows = slice(t * tm, (t + 1) * tm)
        gate = mod_ref[0, j0 + 2:j0 + 3, :]
        o_ref[rows, :] = x_ref[rows, :] + 0.5 * gate * _rms(acc_ref[rows, :], gpost_ref[...])


def _ffn(xs, mod, g_pre, g_post, w_gate, w_up, w_down, *, j0, tm, ctx_tiles):
    batch, s_len, d_model = xs.shape
    d_ff = w_gate.shape[1]
    n_chunks = d_ff // FF_CHUNK
    wg = w_gate.reshape(d_model, n_chunks, FF_CHUNK).transpose(1, 0, 2).astype(BF16)
    wu = w_up.reshape(d_model, n_chunks, FF_CHUNK).transpose(1, 0, 2).astype(BF16)
    wd = w_down.reshape(n_chunks, FF_CHUNK, d_model).astype(BF16)
    gpre, gpost = g_pre.reshape(1, d_model), g_post.reshape(1, d_model)
    tiles_per_row = s_len // tm
    n_steps = batch * tiles_per_row // FFN_TILES
    assert n_steps * FFN_TILES == batch * tiles_per_row
    rows = FFN_TILES * tm

    def mod_spec(t):
        def index(i):
            tile = i * FFN_TILES + t
            b, j = tile // tiles_per_row, tile % tiles_per_row
            return jnp.where(j < ctx_tiles, batch, b), 0, 0
        return pl.BlockSpec((1, N_MOD, d_model), index)

    tok = pl.BlockSpec((rows, d_model), lambda i: (i, 0))
    out = pl.pallas_call(
        functools.partial(_ffn_kernel, j0=j0),
        grid=(n_steps,),
        in_specs=[tok] + [mod_spec(t) for t in range(FFN_TILES)]
                 + [_resident(gpre), _resident(gpost), _resident(wg), _resident(wu), _resident(wd)],
        out_specs=tok,
        out_shape=jax.ShapeDtypeStruct((batch * s_len, d_model), F32),
        scratch_shapes=[pltpu.VMEM((rows, d_model), BF16), pltpu.VMEM((rows, d_model), F32)],
        compiler_params=_params(("parallel",)),
        name="ffn",
    )(xs.reshape(batch * s_len, d_model), *([mod] * FFN_TILES), gpre, gpost, wg, wu, wd)
    return out.reshape(batch, s_len, d_model)


def _short_conv(u, cw_ref, cb_ref, sl, pos, row_len):
    tm = u.shape[0]
    pad = CONV_W // 2
    out = cb_ref[:, sl] + u * cw_ref[pad:pad + 1, sl]
    for j in range(CONV_W):
        d = j - pad
        if d == 0:
            continue
        shifted = pltpu.roll(u, shift=(-d) % tm, axis=0)
        ok = jnp.logical_and(pos + d >= 0, pos + d < row_len)
        out = out + jnp.where(ok, shifted, 0.0) * cw_ref[j:j + 1, sl]
    return out


def _row_position(tm, is_ctx):
    row_len = jnp.where(is_ctx, tm, GRID_W)
    pos = jnp.bitwise_and(lax.broadcasted_iota(jnp.int32, (tm, 1), 0), row_len - 1)
    return pos, row_len


def _block_diag(w):
    per = BD_CHUNK // ML_BLOCK
    n_tiles = w.shape[0] // per
    w = w.reshape(n_tiles, per, ML_BLOCK, ML_BLOCK)
    eye = jnp.eye(per, dtype=w.dtype)
    return jnp.einsum("cnij,nm->cnimj", w, eye).reshape(n_tiles, BD_CHUNK, BD_CHUNK).astype(BF16)


def _mlstm_in_kernel(x_ref, mod_ref, g_ref, win_ref, cw_ref, cb_ref, bdq_ref, bdk_ref, bdv_ref,
                     wgq_ref, wgk_ref, wgv_ref, bg_ref,
                     q_ref, k_ref, v_ref, xc_ref, z_ref, gates_ref, *, tile_off, ctx_tiles, kscale):
    tm = x_ref.shape[1]
    inner = q_ref.shape[2]
    is_ctx = pl.program_id(1) + tile_off < ctx_tiles
    pos, row_len = _row_position(tm, is_ctx)
    shift, scale = mod_ref[0, 3:4, :], mod_ref[0, 4:5, :]
    h = (_rms(x_ref[0], g_ref[...]) * (1.0 + scale) + shift).astype(BF16)
    gates = bg_ref[...]
    for c in range(inner // BD_CHUNK):
        sl = slice(c * BD_CHUNK, (c + 1) * BD_CHUNK)
        zsl = slice(inner + c * BD_CHUNK, inner + (c + 1) * BD_CHUNK)
        xm = _dot(h, win_ref[:, sl])
        z_ref[0, :, sl] = _dot(h, win_ref[:, zsl]).astype(BF16)
        xc = _silu(_short_conv(xm, cw_ref, cb_ref, sl, pos, row_len))
        xcb, xmb = xc.astype(BF16), xm.astype(BF16)
        xc_ref[0, :, sl] = xcb
        q = _dot(xcb, bdq_ref[c])
        k = _dot(xcb, bdk_ref[c])
        v = _dot(xmb, bdv_ref[c])
        qb, kb, vb = q.astype(BF16), k.astype(BF16), v.astype(BF16)
        q_ref[0, :, sl] = qb
        k_ref[0, :, sl] = (k * kscale).astype(BF16)
        v_ref[0, :, sl] = vb
        gates = gates + _dot(qb, wgq_ref[sl, :]) + _dot(kb, wgk_ref[sl, :]) + _dot(vb, wgv_ref[sl, :])
    gates_ref[0] = gates


def _mlstm_in(xs, mod, g, w_in, conv_w, conv_b, w_q, w_k, w_v, w_gates, b_gates, *, tm, ctx_tiles):
    batch, s_len, d_model = xs.shape
    inner = w_in.shape[1] // 2
    n_gates = w_gates.shape[1]
    win = w_in.astype(BF16)
    cb = conv_b.reshape(1, inner)
    bdq, bdk, bdv = _block_diag(w_q), _block_diag(w_k), _block_diag(w_v)
    wg = jnp.pad(w_gates, ((0, 0), (0, LANES - n_gates))).astype(BF16)
    wgq, wgk, wgv = wg[:inner], wg[inner:2 * inner], wg[2 * inner:]
    bg = jnp.pad(b_gates, (0, LANES - n_gates)).reshape(1, LANES)
    gg = g.reshape(1, d_model)
    tok, out, mod_spec = _tile_specs(batch, tm, 0, ctx_tiles)
    act = jax.ShapeDtypeStruct((batch, s_len, inner), BF16)
    kernel = functools.partial(_mlstm_in_kernel, tile_off=0, ctx_tiles=ctx_tiles,
                               kscale=(inner // ML_HEADS) ** -0.5)
    return pl.pallas_call(
        kernel,
        grid=(batch, s_len // tm),
        in_specs=[tok(d_model), mod_spec(d_model), _resident(gg), _resident(win), _resident(conv_w),
                  _resident(cb), _resident(bdq), _resident(bdk), _resident(bdv),
                  _resident(wgq), _resident(wgk), _resident(wgv), _resident(bg)],
        out_specs=[out(inner)] * 5 + [out(LANES)],
        out_shape=[act] * 5 + [jax.ShapeDtypeStruct((batch, s_len, LANES), F32)],
        compiler_params=_params(("parallel", "parallel")),
        name="mlstm_in",
    )(xs, mod, gg, win, conv_w, cb, bdq, bdk, bdv, wgq, wgk, wgv, bg)


def _chunk_masks(chunk, reverse):
    t_idx = lax.broadcasted_iota(jnp.int32, (chunk, chunk), 0)
    s_idx = lax.broadcasted_iota(jnp.int32, (chunk, chunk), 1)
    if reverse:
        return s_idx >= t_idx, t_idx >= s_idx
    return s_idx <= t_idx, t_idx <= s_idx


def _lane_pick(tile, idx):
    lane = lax.broadcasted_iota(jnp.int32, (1, tile.shape[1]), 1)
    return jnp.sum(jnp.where(lane == idx, tile, 0.0), axis=1, keepdims=True)


def _sublane_pick(tile, idx):
    sub = lax.broadcasted_iota(jnp.int32, (tile.shape[0], 1), 0)
    return jnp.sum(jnp.where(sub == idx, tile, 0.0), axis=0, keepdims=True)


def _mlstm_scan_kernel(qf_ref, kf_ref, vf_ref, gf_ref, qb_ref, kb_ref, vb_ref, gb_ref,
                       hf_ref, hb_ref, c_ref, n_ref, m_ref):
    chunk = qf_ref.shape[1]
    head = pl.program_id(1)

    @pl.when(pl.program_id(2) == 0)
    def _():
        c_ref[...] = jnp.zeros_like(c_ref)
        n_ref[...] = jnp.zeros_like(n_ref)
        m_ref[...] = jnp.zeros_like(m_ref)

    dirs = ((qf_ref, kf_ref, vf_ref, gf_ref, hf_ref), (qb_ref, kb_ref, vb_ref, gb_ref, hb_ref))
    for d, (q_ref, k_ref, v_ref, g_ref, o_ref) in enumerate(dirs):
        tri, tri_t = _chunk_masks(chunk, reverse=(d == 1))
        gates = g_ref[0]
        gates_t = gates.T
        i_lane = d * 2 * ML_HEADS + head
        f_lane = i_lane + ML_HEADS
        ig_col, ig_row = _lane_pick(gates, i_lane), _sublane_pick(gates_t, i_lane)
        lf_col = _log_sigmoid(_lane_pick(gates, f_lane))
        lf_row = _log_sigmoid(_sublane_pick(gates_t, f_lane))
        b_col = jnp.sum(jnp.where(tri, lf_row, 0.0), axis=1, keepdims=True)
        b_row = jnp.sum(jnp.where(tri_t, lf_col, 0.0), axis=0, keepdims=True)
        b_end = jnp.sum(lf_col, axis=0, keepdims=True)
        m_prev = m_ref[d, 0:1, 0:1]
        dm = jnp.where(tri, b_col - b_row + ig_row, -jnp.inf)
        m_inter = b_col + m_prev
        m_t = jnp.maximum(m_inter, jnp.max(dm, axis=1, keepdims=True))
        q, k, v = q_ref[0], k_ref[0], v_ref[0]
        s = _dot_nt(q, k) * jnp.exp(dm - m_t)
        sc = jnp.exp(m_inter - m_t)
        cmat, nvec = c_ref[d], n_ref[d]
        num = _dot(s.astype(BF16), v) + sc * _dot(q, cmat.astype(BF16))
        den = jnp.sum(s, axis=1, keepdims=True) + sc * jnp.sum(q.astype(F32) * nvec, axis=1, keepdims=True)
        o_ref[0] = num * (1.0 / jnp.maximum(jnp.abs(den), jnp.exp(-m_t)))
        gl = b_end - b_col + ig_col
        m_new = jnp.maximum(b_end + m_prev, jnp.max(gl, axis=0, keepdims=True))
        kw = k.astype(F32) * jnp.exp(gl - m_new)
        dec = jnp.exp(b_end + m_prev - m_new)
        c_ref[d] = dec * cmat + _dot_tn(kw.astype(BF16), v)
        n_ref[d] = dec * nvec + jnp.sum(kw, axis=0, keepdims=True)
        m_ref[d] = jnp.broadcast_to(m_new, m_ref.shape[1:])


def _scan_chunk_maps(n_chunks, ctx_chunks):
    def fwd(i):
        return i

    def bwd(i):
        return jnp.where(i < ctx_chunks, ctx_chunks - 1 - i, n_chunks - 1 + ctx_chunks - i)

    return fwd, bwd


def _mlstm_scan(q, k, v, gates, *, chunk, ctx_len):
    batch, s_len, inner = q.shape
    dh = inner // ML_HEADS
    n_chunks = s_len // chunk
    fwd, bwd = _scan_chunk_maps(n_chunks, ctx_len // chunk)

    def specs(order):
        head_spec = pl.BlockSpec((1, chunk, dh), lambda b, h, i: (b, order(i), h))
        gate_spec = pl.BlockSpec((1, chunk, LANES), lambda b, h, i: (b, order(i), 0))
        return head_spec, gate_spec

    hf_spec, gf_spec = specs(fwd)
    hb_spec, gb_spec = specs(bwd)
    out = jax.ShapeDtypeStruct((batch, s_len, inner), F32)
    return pl.pallas_call(
        _mlstm_scan_kernel,
        grid=(batch, ML_HEADS, n_chunks),
        in_specs=[hf_spec, hf_spec, hf_spec, gf_spec, hb_spec, hb_spec, hb_spec, gb_spec],
        out_specs=[hf_spec, hb_spec],
        out_shape=[out, out],
        scratch_shapes=[pltpu.VMEM((2, dh, dh), F32), pltpu.VMEM((2, 1, dh), F32),
                        pltpu.VMEM((2, 8, LANES), F32)],
        compiler_params=_params(("parallel", "parallel", "arbitrary")),
        name="mlstm_scan",
    )(q, k, v, gates, q, k, v, gates)


def _mlstm_out_kernel(x_ref, mod_ref, hf_ref, hb_ref, xc_ref, z_ref, ng_ref, sk_ref, wout_ref, g3_ref, o_ref):
    inner = hf_ref.shape[2]
    dh = inner // ML_HEADS
    y = None
    for hd in range(ML_HEADS):
        sl = slice(hd * dh, (hd + 1) * dh)
        hs = hf_ref[0, :, sl] + hb_ref[0, :, sl]
        cen = hs - jnp.mean(hs, axis=-1, keepdims=True)
        hn = cen * lax.rsqrt(jnp.mean(cen * cen, axis=-1, keepdims=True) + EPS)
        u = hn * ng_ref[:, sl] + sk_ref[:, sl] * xc_ref[0, :, sl].astype(F32)
        u = u * _silu(z_ref[0, :, sl].astype(F32))
        part = _dot(u.astype(BF16), wout_ref[sl, :])
        y = part if y is None else y + part
    o_ref[0] = x_ref[0] + mod_ref[0, 5:6, :] * _rms(y, g3_ref[...])


def _mlstm_out(xs, mod, hf, hb, xc, z, norm_g, skip, w_out, g3, *, tm, tile_off, n_tiles, ctx_tiles):
    batch, _, d_model = xs.shape
    inner = hf.shape[2]
    ng, sk, g3r = norm_g.reshape(1, inner), skip.reshape(1, inner), g3.reshape(1, d_model)
    wout = w_out.astype(BF16)
    tok, out, mod_spec = _tile_specs(batch, tm, tile_off, ctx_tiles)
    return pl.pallas_call(
        _mlstm_out_kernel,
        grid=(batch, n_tiles),
        in_specs=[tok(d_model), mod_spec(d_model), tok(inner), tok(inner), tok(inner), tok(inner),
                  _resident(ng), _resident(sk), _resident(wout), _resident(g3r)],
        out_specs=out(d_model),
        out_shape=jax.ShapeDtypeStruct((batch, n_tiles * tm, d_model), F32),
        compiler_params=_params(("parallel", "parallel")),
        name="mlstm_out",
    )(xs, mod, hf, hb, xc, z, ng, sk, wout, g3r)


def _split3(x):
    hi = x.astype(BF16)
    r1 = x - hi.astype(F32)
    mid = r1.astype(BF16)
    lo = (r1 - mid.astype(F32)).astype(BF16)
    return hi, mid, lo


def _dot3(a, pieces):
    return _dot(a, pieces[0]) + _dot(a, pieces[1]) + _dot(a, pieces[2])


def _ssd_lane_source():
    src = []
    for d in range(2):
        for g in range(SSD_GROUPS):
            for _ in range(2):
                for r in range(SSD_HPG):
                    src.append(d * SSD_GROUPS * SSD_HPG + g * SSD_HPG + r)
    return jnp.array(src, jnp.int32)


def _ssd_in_kernel(x_ref, mod_ref, g_ref, wz_ref, wx_ref, wdt_ref, cw_ref, cb_ref, dtb_ref, alog_ref,
                   tril_ref, triu_ref,
                   z_ref, xs_ref, xst_ref, bm_ref, cm_ref, dcol_ref, drow_ref, *, tile_off, ctx_tiles):
    tm = x_ref.shape[1]
    groups = xs_ref.shape[1]
    is_ctx = pl.program_id(1) + tile_off < ctx_tiles
    pos, row_len = _row_position(tm, is_ctx)
    shift, scale = mod_ref[0, 3:4, :], mod_ref[0, 4:5, :]
    h = (_rms(x_ref[0], g_ref[...]) * (1.0 + scale) + shift).astype(BF16)
    for c in range(wz_ref.shape[1] // BD_CHUNK):
        sl = slice(c * BD_CHUNK, (c + 1) * BD_CHUNK)
        z_ref[0, :, sl] = _dot(h, wz_ref[:, sl]).astype(BF16)
    half = BD_CHUNK // 2
    for c in range(wx_ref.shape[1] // BD_CHUNK):
        sl = slice(c * BD_CHUNK, (c + 1) * BD_CHUNK)
        xbc = _silu(_short_conv(_dot(h, wx_ref[:, sl]), cw_ref, cb_ref, sl, pos, row_len))
        if c < groups:
            xs_ref[0, c] = xbc.astype(BF16)
            xst_ref[0, c] = xbc.T.astype(BF16)
        else:
            gi = 2 * (c - groups)
            ref, gi = (bm_ref, gi) if gi < groups else (cm_ref, gi - groups)
            ref[0, gi] = xbc[:, :half].astype(BF16)
            ref[0, gi + 1] = xbc[:, half:].astype(BF16)
    dt = _softplus(_dot(h, wdt_ref[...]) + dtb_ref[...])
    la = _split3(dt * (-LOG2_E * jnp.exp(alog_ref[...])))
    lane = lax.broadcasted_iota(jnp.int32, (1, LANES), 1)
    cs = jnp.where(lane < LANES // 2, _dot3(tril_ref[...], la), _dot3(triu_ref[...], la))
    tile = jnp.where(jnp.bitwise_and(lane, SSD_HPG) == 0, dt, cs)
    dcol_ref[0] = tile
    drow_ref[0] = tile.T


def _ssd_in(xs, mod, g, w_in, conv_w, conv_b, dt_bias, a_log, *, tm, chunk, ctx_tiles):
    batch, s_len, d_model = xs.shape
    groups, state = SSD_GROUPS, SSD_STATE
    gw = SSD_HPG * SSD_HEAD_DIM
    gn = groups * state
    conv_dim = conv_w.shape[1]
    inner = conv_dim - 2 * gn
    assert gw == BD_CHUNK and 2 * state == BD_CHUNK and inner == groups * gw
    assert 4 * dt_bias.size == 2 * LANES
    wz = w_in[:, :inner].astype(BF16)
    wx = w_in[:, inner:inner + conv_dim].astype(BF16)
    src = _ssd_lane_source()
    wdt = w_in[:, inner + conv_dim:][:, src].astype(BF16)
    dtb = dt_bias.reshape(-1)[src].reshape(1, LANES)
    alog = a_log.reshape(-1)[src].reshape(1, LANES)
    t_idx = lax.broadcasted_iota(jnp.int32, (tm, tm), 0)
    u_idx = lax.broadcasted_iota(jnp.int32, (tm, tm), 1)
    same = (t_idx // chunk) == (u_idx // chunk)
    tril = jnp.logical_and(same, u_idx <= t_idx).astype(BF16)
    triu = jnp.logical_and(same, u_idx >= t_idx).astype(BF16)
    cb = conv_b.reshape(1, conv_dim)
    gg = g.reshape(1, d_model)
    tok, out, mod_spec = _tile_specs(batch, tm, 0, ctx_tiles)
    kernel = functools.partial(_ssd_in_kernel, tile_off=0, ctx_tiles=ctx_tiles)
    grp = lambda w: pl.BlockSpec((1, groups, tm, w), lambda b, j: (b, 0, j, 0))
    return pl.pallas_call(
        kernel,
        grid=(batch, s_len // tm),
        in_specs=[tok(d_model), mod_spec(d_model), _resident(gg), _resident(wz), _resident(wx), _resident(wdt),
                  _resident(conv_w), _resident(cb), _resident(dtb), _resident(alog), _resident(tril), _resident(triu)],
        out_specs=[out(inner), grp(gw), pl.BlockSpec((1, groups, gw, tm), lambda b, j: (b, 0, 0, j)),
                   grp(state), grp(state), out(LANES), pl.BlockSpec((1, LANES, tm), lambda b, j: (b, 0, j))],
        out_shape=[jax.ShapeDtypeStruct((batch, s_len, inner), BF16),
                   jax.ShapeDtypeStruct((batch, groups, s_len, gw), BF16),
                   jax.ShapeDtypeStruct((batch, groups, gw, s_len), BF16),
                   jax.ShapeDtypeStruct((batch, groups, s_len, state), BF16),
                   jax.ShapeDtypeStruct((batch, groups, s_len, state), BF16),
                   jax.ShapeDtypeStruct((batch, s_len, LANES), F32),
                   jax.ShapeDtypeStruct((batch, LANES, s_len), F32)],
        compiler_params=_params(("parallel", "parallel")),
        name="ssd_in",
    )(xs, mod, gg, wz, wx, wdt, conv_w, cb, dtb, alog, tril, triu)


def _ssd_scan_kernel(xf_ref, xtf_ref, bf_ref, cf_ref, dcf_ref, drf_ref,
                     xb_ref, xtb_ref, bb_ref, cb_ref, dcb_ref, drb_ref,
                     yf_ref, yb_ref, h_ref):
    groups, chunk = xf_ref.shape[1], xf_ref.shape[2]
    p = SSD_HEAD_DIM

    @pl.when(pl.program_id(1) == 0)
    def _():
        h_ref[...] = jnp.zeros_like(h_ref)

    dirs = ((xf_ref, xtf_ref, bf_ref, cf_ref, dcf_ref, drf_ref, yf_ref),
            (xb_ref, xtb_ref, bb_ref, cb_ref, dcb_ref, drb_ref, yb_ref))
    masks = (_chunk_masks(chunk, reverse=False)[0], _chunk_masks(chunk, reverse=True)[0])

    def group_step(g, carry):
        for d, (x_ref, xt_ref, b_ref, c_ref, dc_ref, dr_ref, y_ref) in enumerate(dirs):
            tri = masks[d]
            base = d * (LANES // 2) + g * 2 * SSD_HPG
            dcol = dc_ref[0]
            rows = dr_ref[0, pl.ds(pl.multiple_of(base, 2 * SSD_HPG), 2 * SSD_HPG), :]
            bm, cm, xg, xt = b_ref[0, g], c_ref[0, g], x_ref[0, g], xt_ref[0, g]
            cbm = _dot_nt(cm, bm)
            end = 0 if d == 1 else chunk - 1
            low_half = lax.broadcasted_iota(jnp.int32, (1, 2 * p), 1) < p
            xw, dec, y_head = [], [], []
            for r in range(SSD_HPG):
                hs = slice(r * p, (r + 1) * p)
                pair = slice((r // 2) * 2 * p, (r // 2 + 1) * 2 * p)
                if r % 2 == 0:
                    ch = _dot_nt(cm, h_ref[d, g, pair, :].astype(BF16))
                dt_row, cs_row = rows[r:r + 1, :], rows[SSD_HPG + r:SSD_HPG + r + 1, :]
                cs_col = _lane_pick(dcol, base + SSD_HPG + r)
                mmat = cbm * jnp.exp2(jnp.where(tri, cs_col - cs_row, -jnp.inf)) * dt_row
                y_head.append(_dot(mmat.astype(BF16), xg[:, pair]) + jnp.exp2(cs_col) * ch)
                if r % 2 == 1:
                    y_ref[0, g, :, pair] = jnp.where(low_half, y_head[r - 1], y_head[r])
                cs_end = cs_row[:, end:end + 1]
                xw.append((xt[hs, :].astype(F32) * (jnp.exp2(cs_end - cs_row) * dt_row)).astype(BF16))
                dec.append(jnp.exp2(cs_end))
            upd = _dot(jnp.concatenate(xw, axis=0), bm)
            for r in range(SSD_HPG):
                hs = slice(r * p, (r + 1) * p)
                h_ref[d, g, hs, :] = dec[r] * h_ref[d, g, hs, :] + upd[hs, :]
        return carry

    lax.fori_loop(0, groups, group_step, 0, unroll=2)


def _ssd_scan(xs_in, xs_t, bm, cm, dcol, drow, *, chunk, ctx_len):
    batch, groups, s_len, gw = xs_in.shape
    state = bm.shape[3]
    n_chunks = s_len // chunk
    fwd, bwd = _scan_chunk_maps(n_chunks, ctx_len // chunk)

    def specs(order):
        return [pl.BlockSpec((1, groups, chunk, gw), lambda b, i: (b, 0, order(i), 0)),
                pl.BlockSpec((1, groups, gw, chunk), lambda b, i: (b, 0, 0, order(i))),
                pl.BlockSpec((1, groups, chunk, state), lambda b, i: (b, 0, order(i), 0)),
                pl.BlockSpec((1, groups, chunk, state), lambda b, i: (b, 0, order(i), 0)),
                pl.BlockSpec((1, chunk, LANES), lambda b, i: (b, order(i), 0)),
                pl.BlockSpec((1, LANES, chunk), lambda b, i: (b, 0, order(i)))]

    sf, sb = specs(fwd), specs(bwd)
    out = jax.ShapeDtypeStruct((batch, groups, s_len, gw), F32)
    args = (xs_in, xs_t, bm, cm, dcol, drow)
    return pl.pallas_call(
        _ssd_scan_kernel,
        grid=(batch, n_chunks),
        in_specs=sf + sb,
        out_specs=[sf[0], sb[0]],
        out_shape=[out, out],
        scratch_shapes=[pltpu.VMEM((2, groups, gw, state), F32)],
        compiler_params=_params(("parallel", "arbitrary")),
        name="ssd_scan",
    )(*args, *args)


def _ssd_out_kernel(x_ref, mod_ref, yf_ref, yb_ref, xs_ref, z_ref, dsk_ref, ng_ref, wout_ref, g3_ref, o_ref, u_ref):
    groups, gw = yf_ref.shape[1], yf_ref.shape[3]
    ssq = None
    for g in range(groups):
        sl = slice(g * gw, (g + 1) * gw)
        y = yf_ref[0, g] + yb_ref[0, g] + dsk_ref[:, sl] * xs_ref[0, g].astype(F32)
        u = y * _silu(z_ref[0, :, sl].astype(F32))
        u_ref[:, sl] = u
        part = jnp.sum(u * u, axis=-1, keepdims=True)
        ssq = part if ssq is None else ssq + part
    inv = lax.rsqrt(ssq / (groups * gw) + EPS)
    out = None
    for g in range(groups):
        sl = slice(g * gw, (g + 1) * gw)
        part = _dot((u_ref[:, sl] * inv * ng_ref[:, sl]).astype(BF16), wout_ref[sl, :])
        out = part if out is None else out + part
    o_ref[0] = x_ref[0] + mod_ref[0, 5:6, :] * _rms(out, g3_ref[...])


def _ssd_out(xs, mod, yf, yb, xs_in, z, d_skip, norm_g, w_out, g3, *, tm, tile_off, n_tiles, ctx_tiles):
    batch, _, d_model = xs.shape
    groups, gw = yf.shape[1], yf.shape[3]
    inner = groups * gw
    dsk = jnp.repeat(d_skip, SSD_HEAD_DIM).reshape(1, inner)
    ng, g3r = norm_g.reshape(1, inner), g3.reshape(1, d_model)
    wout = w_out.astype(BF16)
    tok, out, mod_spec = _tile_specs(batch, tm, tile_off, ctx_tiles)
    grp = pl.BlockSpec((1, groups, tm, gw), lambda b, j: (b, 0, j + tile_off, 0))
    return pl.pallas_call(
        _ssd_out_kernel,
        grid=(batch, n_tiles),
        in_specs=[tok(d_model), mod_spec(d_model), grp, grp, grp, tok(inner),
                  _resident(dsk), _resident(ng), _resident(wout), _resident(g3r)],
        out_specs=out(d_model),
        out_shape=jax.ShapeDtypeStruct((batch, n_tiles * tm, d_model), F32),
        scratch_shapes=[pltpu.VMEM((tm, inner), F32)],
        compiler_params=_params(("parallel", "parallel")),
        name="ssd_out",
    )(xs, mod, yf, yb, xs_in, z, dsk, ng, wout, g3r)


def _forward(x, c, ctx, c_ctx, ada_w, ada_b, norm_g, ffn_w_gate, ffn_w_up, ffn_w_down,
             mlstm_w_in, mlstm_conv_w, mlstm_conv_b, mlstm_w_q, mlstm_w_k, mlstm_w_v,
             mlstm_w_gates, mlstm_b_gates, mlstm_norm_g, mlstm_skip, mlstm_w_out,
             ssd_w_in, ssd_conv_w, ssd_conv_b, ssd_dt_bias, ssd_a_log, ssd_d,
             ssd_norm_g, ssd_w_out, *, tm, ml_chunk, ssd_chunk):
    batch, seq, d_model = x.shape
    ctx_len = ctx.shape[1]
    depth = ada_w.shape[0]
    assert ctx_len == tm and seq % tm == 0 and tm % GRID_W == 0 and tm % ml_chunk == 0 and tm % ssd_chunk == 0
    assert batch < MOD_ROWS
    xs = jnp.concatenate([ctx, x], axis=1)
    ctx_tiles = ctx_len // tm
    c_all = jnp.zeros((MOD_ROWS, d_model), F32).at[:batch].set(c).at[batch].set(c_ctx)
    for i in range(depth):
        last = i == depth - 1
        mod = _modulation(c_all, ada_w[i], ada_b[i]).reshape(MOD_ROWS, N_MOD, d_model)
        g = norm_g[i]
        n_tiles = xs.shape[1] // tm
        xs = _ffn(xs, mod, g[0], g[1], ffn_w_gate[i, 0], ffn_w_up[i, 0], ffn_w_down[i, 0],
                  j0=0, tm=tm, ctx_tiles=ctx_tiles)
        out_off = ctx_tiles if last else 0
        out_kw = dict(tm=tm, tile_off=out_off, n_tiles=n_tiles - out_off, ctx_tiles=ctx_tiles)
        j = i // 2
        if i % 2 == 0:
            q, k, v, xc, z, gates = _mlstm_in(xs, mod, g[2], mlstm_w_in[j], mlstm_conv_w[j], mlstm_conv_b[j],
                                              mlstm_w_q[j], mlstm_w_k[j], mlstm_w_v[j], mlstm_w_gates[j],
                                              mlstm_b_gates[j], tm=tm, ctx_tiles=ctx_tiles)
            hf, hb = _mlstm_scan(q, k, v, gates, chunk=ml_chunk, ctx_len=ctx_len)
            xs = _mlstm_out(xs, mod, hf, hb, xc, z, mlstm_norm_g[j], mlstm_skip[j], mlstm_w_out[j], g[3], **out_kw)
        else:
            z, xs_in, xs_t, bm, cm, dcol, drow = _ssd_in(xs, mod, g[2], ssd_w_in[j], ssd_conv_w[j], ssd_conv_b[j],
                                                         ssd_dt_bias[j], ssd_a_log[j], tm=tm, chunk=ssd_chunk,
                                                         ctx_tiles=ctx_tiles)
            yf, yb = _ssd_scan(xs_in, xs_t, bm, cm, dcol, drow, chunk=ssd_chunk, ctx_len=ctx_len)
            xs = _ssd_out(xs, mod, yf, yb, xs_in, z, ssd_d[j], ssd_norm_g[j], ssd_w_out[j], g[3], **out_kw)
        if last:
            ctx_tiles = 0
        xs = _ffn(xs, mod, g[4], g[5], ffn_w_gate[i, 1], ffn_w_up[i, 1], ffn_w_down[i, 1],
                  j0=6, tm=tm, ctx_tiles=ctx_tiles)
    return xs


def kernel(x, c, ctx, c_ctx, ada_w, ada_b, norm_g, ffn_w_gate, ffn_w_up, ffn_w_down, mlstm_w_in, mlstm_conv_w, mlstm_conv_b, mlstm_w_q, mlstm_w_k, mlstm_w_v, mlstm_w_gates, mlstm_b_gates, mlstm_norm_g, mlstm_skip, mlstm_w_out, ssd_w_in, ssd_conv_w, ssd_conv_b, ssd_dt_bias, ssd_a_log, ssd_d, ssd_norm_g, ssd_w_out):
    return _forward(x, c, ctx, c_ctx, ada_w, ada_b, norm_g, ffn_w_gate, ffn_w_up, ffn_w_down,
                    mlstm_w_in, mlstm_conv_w, mlstm_conv_b, mlstm_w_q, mlstm_w_k, mlstm_w_v,
                    mlstm_w_gates, mlstm_b_gates, mlstm_norm_g, mlstm_skip, mlstm_w_out,
                    ssd_w_in, ssd_conv_w, ssd_conv_b, ssd_dt_bias, ssd_a_log, ssd_d,
                    ssd_norm_g, ssd_w_out, tm=TOKEN_TILE, ml_chunk=MLSTM_CHUNK, ssd_chunk=SSD_CHUNK)
```

```python
import functools

import jax
import jax.numpy as jnp
from jax import lax
from jax.experimental import pallas as pl
from jax.experimental.pallas import tpu as pltpu

F32 = jnp.float32
BF16 = jnp.bfloat16

EPS = 1e-6
LOG2_E = 1.4426950408889634
GRID_W = 64
CONV_W = 5
N_MOD = 9
ML_HEADS = 4
ML_BLOCK = 4
SSD_GROUPS = 8
SSD_HPG = 4
SSD_HEAD_DIM = 64
SSD_STATE = 128

LANES = 128
MXU_DIM = 256
TOKEN_TILE = 256
MLSTM_CHUNK = 256
SSD_CHUNK = 128
FFN_TILES = 2
FF_CHUNK = MXU_DIM
BD_CHUNK = MXU_DIM
MOD_ROWS = 16
VMEM_LIMIT_BYTES = 56 * 1024 * 1024


def _rms(x, g):
    return x * lax.rsqrt(jnp.mean(x * x, axis=-1, keepdims=True) + EPS) * g


def _silu(x):
    return x * jax.nn.sigmoid(x)


def _softplus(x):
    return jnp.maximum(x, 0.0) + jnp.log1p(jnp.exp(-jnp.abs(x)))


def _log_sigmoid(x):
    return -_softplus(-x)


def _dot(a, b):
    return jnp.dot(a, b, preferred_element_type=F32)


def _dot_nt(a, b):
    return lax.dot_general(a, b, (((1,), (1,)), ((), ())), preferred_element_type=F32)


def _dot_tn(a, b):
    return lax.dot_general(a, b, (((0,), (0,)), ((), ())), preferred_element_type=F32)


def _resident(arr):
    nd = arr.ndim
    return pl.BlockSpec(arr.shape, lambda *_: (0,) * nd, pipeline_mode=pl.Buffered(1))


def _params(sem):
    return pltpu.CompilerParams(dimension_semantics=sem, vmem_limit_bytes=VMEM_LIMIT_BYTES)


def _tile_specs(batch, tm, tile_off, ctx_tiles):
    def tok(width):
        return pl.BlockSpec((1, tm, width), lambda b, j: (b, j + tile_off, 0))

    def out(width):
        return pl.BlockSpec((1, tm, width), lambda b, j: (b, j, 0))

    def mod_spec(d_model):
        return pl.BlockSpec((1, N_MOD, d_model),
                            lambda b, j: (jnp.where(j + tile_off < ctx_tiles, batch, b), 0, 0))

    return tok, out, mod_spec


def _mod_kernel(c_ref, w_ref, b_ref, o_ref):
    sc = _silu(c_ref[...])
    o_ref[...] = jnp.dot(sc, w_ref[...], preferred_element_type=F32,
                         precision=lax.Precision.HIGHEST) + b_ref[...]


def _modulation(c_all, w, b):
    d_model, n_out = w.shape
    return pl.pallas_call(
        _mod_kernel,
        grid=(n_out // d_model,),
        in_specs=[pl.BlockSpec(c_all.shape, lambda n: (0, 0)),
                  pl.BlockSpec((d_model, d_model), lambda n: (0, n)),
                  pl.BlockSpec((1, d_model), lambda n: (0, n))],
        out_specs=pl.BlockSpec((c_all.shape[0], d_model), lambda n: (0, n)),
        out_shape=jax.ShapeDtypeStruct((c_all.shape[0], n_out), F32),
        compiler_params=_params(("arbitrary",)),
        name="modulation",
    )(c_all, w, b.reshape(1, n_out))


def _ffn_kernel(x_ref, *refs, j0):
    mod_refs = refs[:FFN_TILES]
    gpre_ref, gpost_ref, wg_ref, wu_ref, wd_ref, o_ref, h_ref, acc_ref = refs[FFN_TILES:]
    tm = x_ref.shape[0] // FFN_TILES
    for t, mod_ref in enumerate(mod_refs):
        rows = slice(t * tm, (t + 1) * tm)
        shift, scale = mod_ref[0, j0:j0 + 1, :], mod_ref[0, j0 + 1:j0 + 2, :]
        h_ref[rows, :] = (_rms(x_ref[rows, :], gpre_ref[...]) * (1.0 + scale) + shift).astype(BF16)
    h = h_ref[...]
    for c in range(wg_ref.shape[1] // FF_CHUNK):
        sl = slice(c * FF_CHUNK, (c + 1) * FF_CHUNK)
        a = (_silu(_dot(h, wg_ref[:, sl])) * _dot(h, wu_ref[:, sl])).astype(BF16)
        contrib = _dot(a, wd_ref[sl, :])
        if c == 0:
            acc_ref[...] = contrib
        else:
            acc_ref[...] += contrib
    for t, mod_ref in enumerate(mod_refs):
        rows = slice(t * tm, (t + 1) * tm)
        gate = mod_ref[0, j0 + 2:j0 + 3, :]
        o_ref[rows, :] = x_ref[rows, :] + 0.5 * gate * _rms(acc_ref[rows, :], gpost_ref[...])


def _ffn(xs, mod, g_pre, g_post, w_gate, w_up, w_down, *, j0, tm, ctx_tiles):
    batch, s_len, d_model = xs.shape
    d_ff = w_gate.shape[1]
    assert d_ff % FF_CHUNK == 0
    wg, wu, wd = w_gate.astype(BF16), w_up.astype(BF16), w_down.astype(BF16)
    gpre, gpost = g_pre.reshape(1, d_model), g_post.reshape(1, d_model)
    tiles_per_row = s_len // tm
    n_steps = batch * tiles_per_row // FFN_TILES
    assert n_steps * FFN_TILES == batch * tiles_per_row
    rows = FFN_TILES * tm

    def mod_spec(t):
        def index(i):
            tile = i * FFN_TILES + t
            b, j = tile // tiles_per_row, tile % tiles_per_row
            return jnp.where(j < ctx_tiles, batch, b), 0, 0
        return pl.BlockSpec((1, N_MOD, d_model), index)

    tok = pl.BlockSpec((rows, d_model), lambda i: (i, 0))
    out = pl.pallas_call(
        functools.partial(_ffn_kernel, j0=j0),
        grid=(n_steps,),
        in_specs=[tok] + [mod_spec(t) for t in range(FFN_TILES)]
                 + [_resident(gpre), _resident(gpost), _resident(wg), _resident(wu), _resident(wd)],
        out_specs=tok,
        out_shape=jax.ShapeDtypeStruct((batch * s_len, d_model), F32),
        scratch_shapes=[pltpu.VMEM((rows, d_model), BF16), pltpu.VMEM((rows, d_model), F32)],
        compiler_params=_params(("parallel",)),
        name="ffn",
    )(xs.reshape(batch * s_len, d_model), *([mod] * FFN_TILES), gpre, gpost, wg, wu, wd)
    return out.reshape(batch, s_len, d_model)


def _short_conv(u, cw_ref, cb_ref, sl, pos, row_len):
    tm = u.shape[0]
    pad = CONV_W // 2
    out = cb_ref[:, sl] + u * cw_ref[pad:pad + 1, sl]
    for j in range(CONV_W):
        d = j - pad
        if d == 0:
            continue
        shifted = pltpu.roll(u, shift=(-d) % tm, axis=0)
        ok = jnp.logical_and(pos + d >= 0, pos + d < row_len)
        out = out + jnp.where(ok, shifted, 0.0) * cw_ref[j:j + 1, sl]
    return out


def _row_position(tm, is_ctx):
    row_len = jnp.where(is_ctx, tm, GRID_W)
    pos = jnp.bitwise_and(lax.broadcasted_iota(jnp.int32, (tm, 1), 0), row_len - 1)
    return pos, row_len


def _block_diag(w):
    per = BD_CHUNK // ML_BLOCK
    n_tiles = w.shape[0] // per
    w = w.reshape(n_tiles, per, ML_BLOCK, ML_BLOCK)
    eye = jnp.eye(per, dtype=w.dtype)
    return jnp.einsum("cnij,nm->cnimj", w, eye).reshape(n_tiles, BD_CHUNK, BD_CHUNK).astype(BF16)


def _mlstm_in_kernel(x_ref, mod_ref, g_ref, win_ref, cw_ref, cb_ref, bdq_ref, bdk_ref, bdv_ref,
                     wgq_ref, wgk_ref, wgv_ref, bg_ref,
                     q_ref, k_ref, v_ref, xc_ref, z_ref, gates_ref, *, tile_off, ctx_tiles, kscale):
    tm = x_ref.shape[1]
    inner = q_ref.shape[2]
    is_ctx = pl.program_id(1) + tile_off < ctx_tiles
    pos, row_len = _row_position(tm, is_ctx)
    shift, scale = mod_ref[0, 3:4, :], mod_ref[0, 4:5, :]
    h = (_rms(x_ref[0], g_ref[...]) * (1.0 + scale) + shift).astype(BF16)
    gates = bg_ref[...]
    n_chunks = inner // BD_CHUNK
    cols = lambda c, base=0: slice(base + c * BD_CHUNK, base + (c + 1) * BD_CHUNK)
    def after_conv(c, xcb, vb, gates):
        sl = cols(c)
        q = _dot(xcb, bdq_ref[c])
        k = _dot(xcb, bdk_ref[c])
        qb, kb = q.astype(BF16), k.astype(BF16)
        q_ref[0, :, sl] = qb
        k_ref[0, :, sl] = (k * kscale).astype(BF16)
        return gates + _dot(qb, wgq_ref[sl, :]) + _dot(kb, wgk_ref[sl, :]) + _dot(vb, wgv_ref[sl, :])

    xm_next = _dot(h, win_ref[:, cols(0)])
    pending = None
    for c in range(n_chunks):
        sl = cols(c)
        xm = xm_next
        z_ref[0, :, sl] = _dot(h, win_ref[:, cols(c, inner)]).astype(BF16)
        vb = _dot(xm.astype(BF16), bdv_ref[c]).astype(BF16)
        v_ref[0, :, sl] = vb
        if c + 1 < n_chunks:
            xm_next = _dot(h, win_ref[:, cols(c + 1)])
        if pending is not None:
            gates = after_conv(*pending, gates)
        xcb = _silu(_short_conv(xm, cw_ref, cb_ref, sl, pos, row_len)).astype(BF16)
        xc_ref[0, :, sl] = xcb
        pending = (c, xcb, vb)
    gates_ref[0] = after_conv(*pending, gates)


def _mlstm_in(xs, mod, g, w_in, conv_w, conv_b, w_q, w_k, w_v, w_gates, b_gates, *, tm, ctx_tiles):
    batch, s_len, d_model = xs.shape
    inner = w_in.shape[1] // 2
    n_gates = w_gates.shape[1]
    win = w_in.astype(BF16)
    cb = conv_b.reshape(1, inner)
    bdq, bdk, bdv = _block_diag(w_q), _block_diag(w_k), _block_diag(w_v)
    wg = jnp.pad(w_gates, ((0, 0), (0, LANES - n_gates))).astype(BF16)
    wgq, wgk, wgv = wg[:inner], wg[inner:2 * inner], wg[2 * inner:]
    bg = jnp.pad(b_gates, (0, LANES - n_gates)).reshape(1, LANES)
    gg = g.reshape(1, d_model)
    tok, out, mod_spec = _tile_specs(batch, tm, 0, ctx_tiles)
    act = jax.ShapeDtypeStruct((batch, s_len, inner), BF16)
    kernel = functools.partial(_mlstm_in_kernel, tile_off=0, ctx_tiles=ctx_tiles,
                               kscale=(inner // ML_HEADS) ** -0.5)
    return pl.pallas_call(
        kernel,
        grid=(batch, s_len // tm),
        in_specs=[tok(d_model), mod_spec(d_model), _resident(gg), _resident(win), _resident(conv_w),
                  _resident(cb), _resident(bdq), _resident(bdk), _resident(bdv),
                  _resident(wgq), _resident(wgk), _resident(wgv), _resident(bg)],
        out_specs=[out(inner)] * 5 + [out(LANES)],
        out_shape=[act] * 5 + [jax.ShapeDtypeStruct((batch, s_len, LANES), F32)],
        compiler_params=_params(("parallel", "parallel")),
        name="mlstm_in",
    )(xs, mod, gg, win, conv_w, cb, bdq, bdk, bdv, wgq, wgk, wgv, bg)


def _chunk_masks(chunk, reverse):
    t_idx = lax.broadcasted_iota(jnp.int32, (chunk, chunk), 0)
    s_idx = lax.broadcasted_iota(jnp.int32, (chunk, chunk), 1)
    if reverse:
        return s_idx >= t_idx, t_idx >= s_idx
    return s_idx <= t_idx, t_idx <= s_idx


def _lane_pick(tile, idx):
    lane = lax.broadcasted_iota(jnp.int32, (1, tile.shape[1]), 1)
    return jnp.sum(jnp.where(lane == idx, tile, 0.0), axis=1, keepdims=True)


def _sublane_pick(tile, idx):
    sub = lax.broadcasted_iota(jnp.int32, (tile.shape[0], 1), 0)
    return jnp.sum(jnp.where(sub == idx, tile, 0.0), axis=0, keepdims=True)


def _mlstm_scan_kernel(qf_ref, kf_ref, vf_ref, gf_ref, qb_ref, kb_ref, vb_ref, gb_ref,
                       hf_ref, hb_ref, c_ref, n_ref, m_ref):
    chunk = qf_ref.shape[1]
    head = pl.program_id(1)

    @pl.when(pl.program_id(2) == 0)
    def _():
        c_ref[...] = jnp.zeros_like(c_ref)
        n_ref[...] = jnp.zeros_like(n_ref)
        m_ref[...] = jnp.zeros_like(m_ref)

    dirs = ((qf_ref, kf_ref, vf_ref, gf_ref, hf_ref), (qb_ref, kb_ref, vb_ref, gb_ref, hb_ref))
    for d, (q_ref, k_ref, v_ref, g_ref, o_ref) in enumerate(dirs):
        tri, tri_t = _chunk_masks(chunk, reverse=(d == 1))
        gates = g_ref[0]
        gates_t = gates.T
        i_lane = d * 2 * ML_HEADS + head
        f_lane = i_lane + ML_HEADS
        ig_col, ig_row = _lane_pick(gates, i_lane), _sublane_pick(gates_t, i_lane)
        lf_col = _log_sigmoid(_lane_pick(gates, f_lane))
        lf_row = _log_sigmoid(_sublane_pick(gates_t, f_lane))
        b_col = jnp.sum(jnp.where(tri, lf_row, 0.0), axis=1, keepdims=True)
        b_row = jnp.sum(jnp.where(tri_t, lf_col, 0.0), axis=0, keepdims=True)
        b_end = jnp.sum(lf_col, axis=0, keepdims=True)
        m_prev = m_ref[d, 0:1, 0:1]
        dm = jnp.where(tri, b_col - b_row + ig_row, -jnp.inf)
        m_inter = b_col + m_prev
        m_t = jnp.maximum(m_inter, jnp.max(dm, axis=1, keepdims=True))
        q, k, v = q_ref[0], k_ref[0], v_ref[0]
        s = _dot_nt(q, k) * jnp.exp(dm - m_t)
        sc = jnp.exp(m_inter - m_t)
        cmat, nvec = c_ref[d], n_ref[d]
        num = _dot(s.astype(BF16), v) + sc * _dot(q, cmat.astype(BF16))
        den = jnp.sum(s, axis=1, keepdims=True) + sc * jnp.sum(q.astype(F32) * nvec, axis=1, keepdims=True)
        o_ref[0] = (num * (1.0 / jnp.maximum(jnp.abs(den), jnp.exp(-m_t)))).astype(o_ref.dtype)
        gl = b_end - b_col + ig_col
        m_new = jnp.maximum(b_end + m_prev, jnp.max(gl, axis=0, keepdims=True))
        kw = k.astype(F32) * jnp.exp(gl - m_new)
        dec = jnp.exp(b_end + m_prev - m_new)
        c_ref[d] = dec * cmat + _dot_tn(kw.astype(BF16), v)
        n_ref[d] = dec * nvec + jnp.sum(kw, axis=0, keepdims=True)
        m_ref[d] = jnp.broadcast_to(m_new, m_ref.shape[1:])


def _scan_chunk_maps(n_chunks, ctx_chunks):
    def fwd(i):
        return i

    def bwd(i):
        return jnp.where(i < ctx_chunks, ctx_chunks - 1 - i, n_chunks - 1 + ctx_chunks - i)

    return fwd, bwd


def _mlstm_scan(q, k, v, gates, *, chunk, ctx_len):
    batch, s_len, inner = q.shape
    dh = inner // ML_HEADS
    n_chunks = s_len // chunk
    fwd, bwd = _scan_chunk_maps(n_chunks, ctx_len // chunk)

    def specs(order):
        head_spec = pl.BlockSpec((1, chunk, dh), lambda b, h, i: (b, order(i), h))
        gate_spec = pl.BlockSpec((1, chunk, LANES), lambda b, h, i: (b, order(i), 0))
        return head_spec, gate_spec

    hf_spec, gf_spec = specs(fwd)
    hb_spec, gb_spec = specs(bwd)
    out = jax.ShapeDtypeStruct((batch, s_len, inner), BF16)
    return pl.pallas_call(
        _mlstm_scan_kernel,
        grid=(batch, ML_HEADS, n_chunks),
        in_specs=[hf_spec, hf_spec, hf_spec, gf_spec, hb_spec, hb_spec, hb_spec, gb_spec],
        out_specs=[hf_spec, hb_spec],
        out_shape=[out, out],
        scratch_shapes=[pltpu.VMEM((2, dh, dh), F32), pltpu.VMEM((2, 1, dh), F32),
                        pltpu.VMEM((2, 8, LANES), F32)],
        compiler_params=_params(("parallel", "parallel", "arbitrary")),
        name="mlstm_scan",
    )(q, k, v, gates, q, k, v, gates)


def _mlstm_out_kernel(x_ref, mod_ref, hf_ref, hb_ref, xc_ref, z_ref, ng_ref, sk_ref, wout_ref, g3_ref, o_ref):
    inner = hf_ref.shape[2]
    dh = inner // ML_HEADS
    y = None
    for hd in range(ML_HEADS):
        sl = slice(hd * dh, (hd + 1) * dh)
        hs = hf_ref[0, :, sl].astype(F32) + hb_ref[0, :, sl].astype(F32)
        cen = hs - jnp.mean(hs, axis=-1, keepdims=True)
        hn = cen * lax.rsqrt(jnp.mean(cen * cen, axis=-1, keepdims=True) + EPS)
        u = hn * ng_ref[:, sl] + sk_ref[:, sl] * xc_ref[0, :, sl].astype(F32)
        u = u * _silu(z_ref[0, :, sl].astype(F32))
        part = _dot(u.astype(BF16), wout_ref[sl, :])
        y = part if y is None else y + part
    o_ref[0] = x_ref[0] + mod_ref[0, 5:6, :] * _rms(y, g3_ref[...])


def _mlstm_out(xs, mod, hf, hb, xc, z, norm_g, skip, w_out, g3, *, tm, tile_off, n_tiles, ctx_tiles):
    batch, _, d_model = xs.shape
    inner = hf.shape[2]
    ng, sk, g3r = norm_g.reshape(1, inner), skip.reshape(1, inner), g3.reshape(1, d_model)
    wout = w_out.astype(BF16)
    tok, out, mod_spec = _tile_specs(batch, tm, tile_off, ctx_tiles)
    return pl.pallas_call(
        _mlstm_out_kernel,
        grid=(batch, n_tiles),
        in_specs=[tok(d_model), mod_spec(d_model), tok(inner), tok(inner), tok(inner), tok(inner),
                  _resident(ng), _resident(sk), _resident(wout), _resident(g3r)],
        out_specs=out(d_model),
        out_shape=jax.ShapeDtypeStruct((batch, n_tiles * tm, d_model), F32),
        compiler_params=_params(("parallel", "parallel")),
        name="mlstm_out",
    )(xs, mod, hf, hb, xc, z, ng, sk, wout, g3r)


def _split3(x):
    hi = x.astype(BF16)
    r1 = x - hi.astype(F32)
    mid = r1.astype(BF16)
    lo = (r1 - mid.astype(F32)).astype(BF16)
    return hi, mid, lo


def _dot3(a, pieces):
    return _dot(a, pieces[0]) + _dot(a, pieces[1]) + _dot(a, pieces[2])


def _ssd_lane_source():
    src = []
    for d in range(2):
        for g in range(SSD_GROUPS):
            for _ in range(2):
                for r in range(SSD_HPG):
                    src.append(d * SSD_GROUPS * SSD_HPG + g * SSD_HPG + r)
    return jnp.array(src, jnp.int32)


def _ssd_in_kernel(x_ref, mod_ref, g_ref, wz_ref, wx_ref, wdt_ref, cw_ref, cb_ref, dtb_ref, alog_ref,
                   tril_ref, triu_ref,
                   z_ref, xs_ref, xst_ref, bm_ref, cm_ref, dcol_ref, drow_ref, *, tile_off, ctx_tiles):
    tm = x_ref.shape[1]
    groups = xs_ref.shape[1]
    is_ctx = pl.program_id(1) + tile_off < ctx_tiles
    pos, row_len = _row_position(tm, is_ctx)
    shift, scale = mod_ref[0, 3:4, :], mod_ref[0, 4:5, :]
    h = (_rms(x_ref[0], g_ref[...]) * (1.0 + scale) + shift).astype(BF16)
    half = BD_CHUNK // 2
    n_z, n_x = wz_ref.shape[1] // BD_CHUNK, wx_ref.shape[1] // BD_CHUNK
    for c in range(n_x):
        sl = slice(c * BD_CHUNK, (c + 1) * BD_CHUNK)
        for cz in range(c * n_z // n_x, (c + 1) * n_z // n_x):
            zsl = slice(cz * BD_CHUNK, (cz + 1) * BD_CHUNK)
            z_ref[0, :, zsl] = _dot(h, wz_ref[:, zsl]).astype(BF16)
        xbc = _silu(_short_conv(_dot(h, wx_ref[:, sl]), cw_ref, cb_ref, sl, pos, row_len))
        if c < groups:
            xs_ref[0, c] = xbc.astype(BF16)
            xst_ref[0, c] = xbc.T.astype(BF16)
        else:
            gi = 2 * (c - groups)
            ref, gi = (bm_ref, gi) if gi < groups else (cm_ref, gi - groups)
            ref[0, gi] = xbc[:, :half].astype(BF16)
            ref[0, gi + 1] = xbc[:, half:].astype(BF16)
    dt = _softplus(_dot(h, wdt_ref[...]) + dtb_ref[...])
    la = _split3(dt * (-LOG2_E * jnp.exp(alog_ref[...])))
    lane = lax.broadcasted_iota(jnp.int32, (1, LANES), 1)
    cs = jnp.where(lane < LANES // 2, _dot3(tril_ref[...], la), _dot3(triu_ref[...], la))
    tile = jnp.where(jnp.bitwise_and(lane, SSD_HPG) == 0, dt, cs)
    dcol_ref[0] = tile
    drow_ref[0] = tile.T


def _ssd_in(xs, mod, g, w_in, conv_w, conv_b, dt_bias, a_log, *, tm, chunk, ctx_tiles):
    batch, s_len, d_model = xs.shape
    groups, state = SSD_GROUPS, SSD_STATE
    gw = SSD_HPG * SSD_HEAD_DIM
    gn = groups * state
    conv_dim = conv_w.shape[1]
    inner = conv_dim - 2 * gn
    assert gw == BD_CHUNK and 2 * state == BD_CHUNK and inner == groups * gw
    assert 4 * dt_bias.size == 2 * LANES
    wz = w_in[:, :inner].astype(BF16)
    wx = w_in[:, inner:inner + conv_dim].astype(BF16)
    src = _ssd_lane_source()
    wdt = w_in[:, inner + conv_dim:][:, src].astype(BF16)
    dtb = dt_bias.reshape(-1)[src].reshape(1, LANES)
    alog = a_log.reshape(-1)[src].reshape(1, LANES)
    t_idx = lax.broadcasted_iota(jnp.int32, (tm, tm), 0)
    u_idx = lax.broadcasted_iota(jnp.int32, (tm, tm), 1)
    same = (t_idx // chunk) == (u_idx // chunk)
    tril = jnp.logical_and(same, u_idx <= t_idx).astype(BF16)
    triu = jnp.logical_and(same, u_idx >= t_idx).astype(BF16)
    cb = conv_b.reshape(1, conv_dim)
    gg = g.reshape(1, d_model)
    tok, out, mod_spec = _tile_specs(batch, tm, 0, ctx_tiles)
    kernel = functools.partial(_ssd_in_kernel, tile_off=0, ctx_tiles=ctx_tiles)
    grp = lambda w: pl.BlockSpec((1, groups, tm, w), lambda b, j: (b, 0, j, 0))
    return pl.pallas_call(
        kernel,
        grid=(batch, s_len // tm),
        in_specs=[tok(d_model), mod_spec(d_model), _resident(gg), _resident(wz), _resident(wx), _resident(wdt),
                  _resident(conv_w), _resident(cb), _resident(dtb), _resident(alog), _resident(tril), _resident(triu)],
        out_specs=[out(inner), grp(gw), pl.BlockSpec((1, groups, gw, tm), lambda b, j: (b, 0, 0, j)),
                   grp(state), grp(state), out(LANES), pl.BlockSpec((1, LANES, tm), lambda b, j: (b, 0, j))],
        out_shape=[jax.ShapeDtypeStruct((batch, s_len, inner), BF16),
                   jax.ShapeDtypeStruct((batch, groups, s_len, gw), BF16),
                   jax.ShapeDtypeStruct((batch, groups, gw, s_len), BF16),
                   jax.ShapeDtypeStruct((batch, groups, s_len, state), BF16),
                   jax.ShapeDtypeStruct((batch, groups, s_len, state), BF16),
                   jax.ShapeDtypeStruct((batch, s_len, LANES), F32),
                   jax.ShapeDtypeStruct((batch, LANES, s_len), F32)],
        compiler_params=_params(("parallel", "parallel")),
        name="ssd_in",
    )(xs, mod, gg, wz, wx, wdt, conv_w, cb, dtb, alog, tril, triu)


def _ssd_scan_kernel(xf_ref, xtf_ref, bf_ref, cf_ref, dcf_ref, drf_ref,
                     xb_ref, xtb_ref, bb_ref, cb_ref, dcb_ref, drb_ref,
                     yf_ref, yb_ref, h_ref):
    groups, chunk = xf_ref.shape[1], xf_ref.shape[2]
    p = SSD_HEAD_DIM

    @pl.when(pl.program_id(1) == 0)
    def _():
        h_ref[...] = jnp.zeros_like(h_ref)

    dirs = ((xf_ref, xtf_ref, bf_ref, cf_ref, dcf_ref, drf_ref, yf_ref),
            (xb_ref, xtb_ref, bb_ref, cb_ref, dcb_ref, drb_ref, yb_ref))
    masks = (_chunk_masks(chunk, reverse=False)[0], _chunk_masks(chunk, reverse=True)[0])

    def group_step(g, carry):
        for d, (x_ref, xt_ref, b_ref, c_ref, dc_ref, dr_ref, y_ref) in enumerate(dirs):
            tri = masks[d]
            base = d * (LANES // 2) + g * 2 * SSD_HPG
            dcol = dc_ref[0]
            rows = dr_ref[0, pl.ds(pl.multiple_of(base, 2 * SSD_HPG), 2 * SSD_HPG), :]
            bm, cm, xg, xt = b_ref[0, g], c_ref[0, g], x_ref[0, g], xt_ref[0, g]
            cbm = _dot_nt(cm, bm)
            end = 0 if d == 1 else chunk - 1
            low_half = lax.broadcasted_iota(jnp.int32, (1, 2 * p), 1) < p
            xw, dec, y_head = [], [], []
            for r in range(SSD_HPG):
                hs = slice(r * p, (r + 1) * p)
                pair = slice((r // 2) * 2 * p, (r // 2 + 1) * 2 * p)
                if r % 2 == 0:
                    ch = _dot_nt(cm, h_ref[d, g, pair, :].astype(BF16))
                dt_row, cs_row = rows[r:r + 1, :], rows[SSD_HPG + r:SSD_HPG + r + 1, :]
                cs_col = _lane_pick(dcol, base + SSD_HPG + r)
                mmat = cbm * jnp.exp2(jnp.where(tri, cs_col - cs_row, -jnp.inf)) * dt_row
                y_head.append(_dot(mmat.astype(BF16), xg[:, pair]) + jnp.exp2(cs_col) * ch)
                if r % 2 == 1:
                    y_ref[0, g, :, pair] = jnp.where(low_half, y_head[r - 1], y_head[r]).astype(y_ref.dtype)
                cs_end = cs_row[:, end:end + 1]
                xw.append((xt[hs, :].astype(F32) * (jnp.exp2(cs_end - cs_row) * dt_row)).astype(BF16))
                dec.append(jnp.exp2(cs_end))
            upd = _dot(jnp.concatenate(xw, axis=0), bm)
            for r in range(SSD_HPG):
                hs = slice(r * p, (r + 1) * p)
                h_ref[d, g, hs, :] = dec[r] * h_ref[d, g, hs, :] + upd[hs, :]
        return carry

    lax.fori_loop(0, groups, group_step, 0, unroll=2)


def _ssd_scan(xs_in, xs_t, bm, cm, dcol, drow, *, chunk, ctx_len):
    batch, groups, s_len, gw = xs_in.shape
    state = bm.shape[3]
    n_chunks = s_len // chunk
    fwd, bwd = _scan_chunk_maps(n_chunks, ctx_len // chunk)

    def specs(order):
        return [pl.BlockSpec((1, groups, chunk, gw), lambda b, i: (b, 0, order(i), 0)),
                pl.BlockSpec((1, groups, gw, chunk), lambda b, i: (b, 0, 0, order(i))),
                pl.BlockSpec((1, groups, chunk, state), lambda b, i: (b, 0, order(i), 0)),
                pl.BlockSpec((1, groups, chunk, state), lambda b, i: (b, 0, order(i), 0)),
                pl.BlockSpec((1, chunk, LANES), lambda b, i: (b, order(i), 0)),
                pl.BlockSpec((1, LANES, chunk), lambda b, i: (b, 0, order(i)))]

    sf, sb = specs(fwd), specs(bwd)
    out = jax.ShapeDtypeStruct((batch, groups, s_len, gw), BF16)
    args = (xs_in, xs_t, bm, cm, dcol, drow)
    return pl.pallas_call(
        _ssd_scan_kernel,
        grid=(batch, n_chunks),
        in_specs=sf + sb,
        out_specs=[sf[0], sb[0]],
        out_shape=[out, out],
        scratch_shapes=[pltpu.VMEM((2, groups, gw, state), F32)],
        compiler_params=_params(("parallel", "arbitrary")),
        name="ssd_scan",
    )(*args, *args)


def _ssd_out_kernel(x_ref, mod_ref, yf_ref, yb_ref, xs_ref, z_ref, dsk_ref, ng_ref, wout_ref, g3_ref, o_ref, u_ref):
    groups, gw = yf_ref.shape[1], yf_ref.shape[3]
    ssq = None
    for g in range(groups):
        sl = slice(g * gw, (g + 1) * gw)
        y = yf_ref[0, g].astype(F32) + yb_ref[0, g].astype(F32) + dsk_ref[:, sl] * xs_ref[0, g].astype(F32)
        u = y * _silu(z_ref[0, :, sl].astype(F32))
        u_ref[:, sl] = u
        part = jnp.sum(u * u, axis=-1, keepdims=True)
        ssq = part if ssq is None else ssq + part
    inv = lax.rsqrt(ssq / (groups * gw) + EPS)
    out = None
    for g in range(groups):
        sl = slice(g * gw, (g + 1) * gw)
        part = _dot((u_ref[:, sl] * inv * ng_ref[:, sl]).astype(BF16), wout_ref[sl, :])
        out = part if out is None else out + part
    o_ref[0] = x_ref[0] + mod_ref[0, 5:6, :] * _rms(out, g3_ref[...])


def _ssd_out(xs, mod, yf, yb, xs_in, z, d_skip, norm_g, w_out, g3, *, tm, tile_off, n_tiles, ctx_tiles):
    batch, _, d_model = xs.shape
    groups, gw = yf.shape[1], yf.shape[3]
    inner = groups * gw
    dsk = jnp.repeat(d_skip, SSD_HEAD_DIM).reshape(1, inner)
    ng, g3r = norm_g.reshape(1, inner), g3.reshape(1, d_model)
    wout = w_out.astype(BF16)
    tok, out, mod_spec = _tile_specs(batch, tm, tile_off, ctx_tiles)
    grp = pl.BlockSpec((1, groups, tm, gw), lambda b, j: (b, 0, j + tile_off, 0))
    return pl.pallas_call(
        _ssd_out_kernel,
        grid=(batch, n_tiles),
        in_specs=[tok(d_model), mod_spec(d_model), grp, grp, grp, tok(inner),
                  _resident(dsk), _resident(ng), _resident(wout), _resident(g3r)],
        out_specs=out(d_model),
        out_shape=jax.ShapeDtypeStruct((batch, n_tiles * tm, d_model), F32),
        scratch_shapes=[pltpu.VMEM((tm, inner), F32)],
        compiler_params=_params(("parallel", "parallel")),
        name="ssd_out",
    )(xs, mod, yf, yb, xs_in, z, dsk, ng, wout, g3r)


def _forward(x, c, ctx, c_ctx, ada_w, ada_b, norm_g, ffn_w_gate, ffn_w_up, ffn_w_down,
             mlstm_w_in, mlstm_conv_w, mlstm_conv_b, mlstm_w_q, mlstm_w_k, mlstm_w_v,
             mlstm_w_gates, mlstm_b_gates, mlstm_norm_g, mlstm_skip, mlstm_w_out,
             ssd_w_in, ssd_conv_w, ssd_conv_b, ssd_dt_bias, ssd_a_log, ssd_d,
             ssd_norm_g, ssd_w_out, *, tm, ml_chunk, ssd_chunk):
    batch, seq, d_model = x.shape
    ctx_len = ctx.shape[1]
    depth = ada_w.shape[0]
    assert ctx_len == tm and seq % tm == 0 and tm % GRID_W == 0 and tm % ml_chunk == 0 and tm % ssd_chunk == 0
    assert batch < MOD_ROWS
    xs = jnp.concatenate([ctx, x], axis=1)
    ctx_tiles = ctx_len // tm
    c_all = jnp.zeros((MOD_ROWS, d_model), F32).at[:batch].set(c).at[batch].set(c_ctx)
    for i in range(depth):
        last = i == depth - 1
        mod = _modulation(c_all, ada_w[i], ada_b[i]).reshape(MOD_ROWS, N_MOD, d_model)
        g = norm_g[i]
        n_tiles = xs.shape[1] // tm
        xs = _ffn(xs, mod, g[0], g[1], ffn_w_gate[i, 0], ffn_w_up[i, 0], ffn_w_down[i, 0],
                  j0=0, tm=tm, ctx_tiles=ctx_tiles)
        out_off = ctx_tiles if last else 0
        out_kw = dict(tm=tm, tile_off=out_off, n_tiles=n_tiles - out_off, ctx_tiles=ctx_tiles)
        j = i // 2
        if i % 2 == 0:
            q, k, v, xc, z, gates = _mlstm_in(xs, mod, g[2], mlstm_w_in[j], mlstm_conv_w[j], mlstm_conv_b[j],
                                              mlstm_w_q[j], mlstm_w_k[j], mlstm_w_v[j], mlstm_w_gates[j],
                                              mlstm_b_gates[j], tm=tm, ctx_tiles=ctx_tiles)
            hf, hb = _mlstm_scan(q, k, v, gates, chunk=ml_chunk, ctx_len=ctx_len)
            xs = _mlstm_out(xs, mod, hf, hb, xc, z, mlstm_norm_g[j], mlstm_skip[j], mlstm_w_out[j], g[3], **out_kw)
        else:
            z, xs_in, xs_t, bm, cm, dcol, drow = _ssd_in(xs, mod, g[2], ssd_w_in[j], ssd_conv_w[j], ssd_conv_b[j],
                                                         ssd_dt_bias[j], ssd_a_log[j], tm=tm, chunk=ssd_chunk,
                                                         ctx_tiles=ctx_tiles)
            yf, yb = _ssd_scan(xs_in, xs_t, bm, cm, dcol, drow, chunk=ssd_chunk, ctx_len=ctx_len)
            xs = _ssd_out(xs, mod, yf, yb, xs_in, z, ssd_d[j], ssd_norm_g[j], ssd_w_out[j], g[3], **out_kw)
        if last:
            ctx_tiles = 0
        xs = _ffn(xs, mod, g[4], g[5], ffn_w_gate[i, 1], ffn_w_up[i, 1], ffn_w_down[i, 1],
                  j0=6, tm=tm, ctx_tiles=ctx_tiles)
    return xs


def kernel(x, c, ctx, c_ctx, ada_w, ada_b, norm_g, ffn_w_gate, ffn_w_up, ffn_w_down, mlstm_w_in, mlstm_conv_w, mlstm_conv_b, mlstm_w_q, mlstm_w_k, mlstm_w_v, mlstm_w_gates, mlstm_b_gates, mlstm_norm_g, mlstm_skip, mlstm_w_out, ssd_w_in, ssd_conv_w, ssd_conv_b, ssd_dt_bias, ssd_a_log, ssd_d, ssd_norm_g, ssd_w_out):
    return _forward(x, c, ctx, c_ctx, ada_w, ada_b, norm_g, ffn_w_gate, ffn_w_up, ffn_w_down,
                    mlstm_w_in, mlstm_conv_w, mlstm_conv_b, mlstm_w_q, mlstm_w_k, mlstm_w_v,
                    mlstm_w_gates, mlstm_b_gates, mlstm_norm_g, mlstm_skip, mlstm_w_out,
                    ssd_w_in, ssd_conv_w, ssd_conv_b, ssd_dt_bias, ssd_a_log, ssd_d,
                    ssd_norm_g, ssd_w_out, tm=TOKEN_TILE, ml_chunk=MLSTM_CHUNK, ssd_chunk=SSD_CHUNK)
```

```python
import functools

import jax
import jax.numpy as jnp
from jax import lax
from jax.experimental import pallas as pl
from jax.experimental.pallas import tpu as pltpu

F32 = jnp.float32
BF16 = jnp.bfloat16

EPS = 1e-6
LOG2_E = 1.4426950408889634
GRID_W = 64
CONV_W = 5
N_MOD = 9
ML_HEADS = 4
ML_BLOCK = 4
SSD_GROUPS = 8
SSD_HPG = 4
SSD_HEAD_DIM = 64
SSD_STATE = 128

LANES = 128
SUBLANES = 8
MXU_DIM = 256
TOKEN_TILE = 256
MLSTM_CHUNK = 256
SSD_CHUNK = 128
FFN_TILES = 2
FF_CHUNK = MXU_DIM
BD_CHUNK = MXU_DIM
MOD_ROWS = 16
VMEM_LIMIT_BYTES = 56 * 1024 * 1024


def _rms(x, g):
    return x * lax.rsqrt(jnp.mean(x * x, axis=-1, keepdims=True) + EPS) * g


def _silu(x):
    return x * jax.nn.sigmoid(x)


def _softplus(x):
    return jnp.maximum(x, 0.0) + jnp.log1p(jnp.exp(-jnp.abs(x)))


def _log_sigmoid(x):
    return -_softplus(-x)


def _dot(a, b):
    return jnp.dot(a, b, preferred_element_type=F32)


def _dot_nt(a, b):
    return lax.dot_general(a, b, (((1,), (1,)), ((), ())), preferred_element_type=F32)


def _dot_tn(a, b):
    return lax.dot_general(a, b, (((0,), (0,)), ((), ())), preferred_element_type=F32)


def _resident(arr):
    nd = arr.ndim
    return pl.BlockSpec(arr.shape, lambda *_: (0,) * nd, pipeline_mode=pl.Buffered(1))


def _params(sem):
    return pltpu.CompilerParams(dimension_semantics=sem, vmem_limit_bytes=VMEM_LIMIT_BYTES)


def _tile_specs(batch, tm, tile_off, ctx_tiles):
    def tok(width):
        return pl.BlockSpec((1, tm, width), lambda b, j: (b, j + tile_off, 0))

    def out(width):
        return pl.BlockSpec((1, tm, width), lambda b, j: (b, j, 0))

    def mod_spec(d_model):
        return pl.BlockSpec((1, N_MOD, d_model),
                            lambda b, j: (jnp.where(j + tile_off < ctx_tiles, batch, b), 0, 0))

    return tok, out, mod_spec


def _mod_kernel(c_ref, w_ref, b_ref, o_ref):
    sc = _silu(c_ref[...])
    o_ref[...] = jnp.dot(sc, w_ref[...], preferred_element_type=F32,
                         precision=lax.Precision.HIGHEST) + b_ref[...]


def _modulation(c_all, w, b):
    d_model, n_out = w.shape
    return pl.pallas_call(
        _mod_kernel,
        grid=(n_out // d_model,),
        in_specs=[pl.BlockSpec(c_all.shape, lambda n: (0, 0)),
                  pl.BlockSpec((d_model, d_model), lambda n: (0, n)),
                  pl.BlockSpec((1, d_model), lambda n: (0, n))],
        out_specs=pl.BlockSpec((c_all.shape[0], d_model), lambda n: (0, n)),
        out_shape=jax.ShapeDtypeStruct((c_all.shape[0], n_out), F32),
        compiler_params=_params(("arbitrary",)),
        name="modulation",
    )(c_all, w, b.reshape(1, n_out))


def _ffn_kernel(x_ref, *refs, j0):
    mod_refs = refs[:FFN_TILES]
    gpre_ref, gpost_ref, wg_ref, wu_ref, wd_ref, o_ref, h_ref, acc_ref = refs[FFN_TILES:]
    tm = x_ref.shape[0] // FFN_TILES
    for t, mod_ref in enumerate(mod_refs):
        rows = slice(t * tm, (t + 1) * tm)
        shift, scale = mod_ref[0, j0:j0 + 1, :], mod_ref[0, j0 + 1:j0 + 2, :]
        h_ref[rows, :] = (_rms(x_ref[rows, :], gpre_ref[...]) * (1.0 + scale) + shift).astype(BF16)
    h = h_ref[...]
    for c in range(wg_ref.shape[1] // FF_CHUNK):
        sl = slice(c * FF_CHUNK, (c + 1) * FF_CHUNK)
        a = (_silu(_dot(h, wg_ref[:, sl])) * _dot(h, wu_ref[:, sl])).astype(BF16)
        contrib = _dot(a, wd_ref[sl, :])
        if c == 0:
            acc_ref[...] = contrib
        else:
            acc_ref[...] += contrib
    for t, mod_ref in enumerate(mod_refs):
        rows = slice(t * tm, (t + 1) * tm)
        gate = mod_ref[0, j0 + 2:j0 + 3, :]
        o_ref[rows, :] = x_ref[rows, :] + 0.5 * gate * _rms(acc_ref[rows, :], gpost_ref[...])


def _ffn(xs, mod, g_pre, g_post, w_gate, w_up, w_down, *, j0, tm, ctx_tiles):
    batch, s_len, d_model = xs.shape
    d_ff = w_gate.shape[1]
    assert d_ff % FF_CHUNK == 0
    wg, wu, wd = w_gate.astype(BF16), w_up.astype(BF16), w_down.astype(BF16)
    gpre, gpost = g_pre.reshape(1, d_model), g_post.reshape(1, d_model)
    tiles_per_row = s_len // tm
    n_steps = batch * tiles_per_row // FFN_TILES
    assert n_steps * FFN_TILES == batch * tiles_per_row
    rows = FFN_TILES * tm

    def mod_spec(t):
        def index(i):
            tile = i * FFN_TILES + t
            b, j = tile // tiles_per_row, tile % tiles_per_row
            return jnp.where(j < ctx_tiles, batch, b), 0, 0
        return pl.BlockSpec((1, N_MOD, d_model), index)

    tok = pl.BlockSpec((rows, d_model), lambda i: (i, 0))
    out = pl.pallas_call(
        functools.partial(_ffn_kernel, j0=j0),
        grid=(n_steps,),
        in_specs=[tok] + [mod_spec(t) for t in range(FFN_TILES)]
                 + [_resident(gpre), _resident(gpost), _resident(wg), _resident(wu), _resident(wd)],
        out_specs=tok,
        out_shape=jax.ShapeDtypeStruct((batch * s_len, d_model), F32),
        scratch_shapes=[pltpu.VMEM((rows, d_model), BF16), pltpu.VMEM((rows, d_model), F32)],
        compiler_params=_params(("parallel",)),
        name="ffn",
    )(xs.reshape(batch * s_len, d_model), *([mod] * FFN_TILES), gpre, gpost, wg, wu, wd)
    return out.reshape(batch, s_len, d_model)


def _short_conv(u, cw_ref, cb_ref, sl, is_ctx):
    tm, width = u.shape
    n_grp, per_row = tm // SUBLANES, GRID_W // SUBLANES
    pad = CONV_W // 2
    u3 = u.reshape(n_grp, SUBLANES, width)
    sub = lax.broadcasted_iota(jnp.int32, (1, SUBLANES, 1), 1)
    zero = jnp.zeros((1, SUBLANES, width), F32)
    out = cb_ref[:, sl] + u3 * cw_ref[pad:pad + 1, sl]
    for j in range(CONV_W):
        d = j - pad
        if d == 0:
            continue
        rot = pltpu.roll(u3, shift=(-d) % SUBLANES, axis=1)
        step = 1 if d > 0 else -1
        parts, run = [], []
        for g in range(n_grp):
            src = g + step
            edge = (src % per_row == 0) if d > 0 else (g % per_row == 0)
            if not edge:
                run.append(src)
                continue
            if run:
                parts.append(rot[run[0]:run[-1] + 1])
                run = []
            inside = 0 <= src < n_grp
            parts.append(jnp.where(is_ctx, rot[src:src + 1], zero) if inside else zero)
        if run:
            parts.append(rot[run[0]:run[-1] + 1])
        neighbour = jnp.concatenate(parts, axis=0)
        from_neighbour = (sub >= SUBLANES - d) if d > 0 else (sub < -d)
        out = out + jnp.where(from_neighbour, neighbour, rot) * cw_ref[j:j + 1, sl]
    return out.reshape(tm, width)


def _block_diag(w):
    per = BD_CHUNK // ML_BLOCK
    n_tiles = w.shape[0] // per
    w = w.reshape(n_tiles, per, ML_BLOCK, ML_BLOCK)
    eye = jnp.eye(per, dtype=w.dtype)
    return jnp.einsum("cnij,nm->cnimj", w, eye).reshape(n_tiles, BD_CHUNK, BD_CHUNK).astype(BF16)


def _mlstm_in_kernel(x_ref, mod_ref, g_ref, win_ref, cw_ref, cb_ref, bdq_ref, bdk_ref, bdv_ref,
                     wgq_ref, wgk_ref, wgv_ref, bg_ref,
                     q_ref, k_ref, v_ref, xc_ref, z_ref, gates_ref, *, tile_off, ctx_tiles, kscale):
    tm = x_ref.shape[1]
    inner = q_ref.shape[2]
    is_ctx = pl.program_id(1) + tile_off < ctx_tiles
    shift, scale = mod_ref[0, 3:4, :], mod_ref[0, 4:5, :]
    h = (_rms(x_ref[0], g_ref[...]) * (1.0 + scale) + shift).astype(BF16)
    gates = bg_ref[...]
    n_chunks = inner // BD_CHUNK
    cols = lambda c, base=0: slice(base + c * BD_CHUNK, base + (c + 1) * BD_CHUNK)
    def after_conv(c, xcb, vb, gates):
        sl = cols(c)
        q = _dot(xcb, bdq_ref[c])
        k = _dot(xcb, bdk_ref[c])
        qb, kb = q.astype(BF16), k.astype(BF16)
        q_ref[0, :, sl] = qb
        k_ref[0, :, sl] = (k * kscale).astype(BF16)
        return gates + _dot(qb, wgq_ref[sl, :]) + _dot(kb, wgk_ref[sl, :]) + _dot(vb, wgv_ref[sl, :])

    xm_next = _dot(h, win_ref[:, cols(0)])
    pending = None
    for c in range(n_chunks):
        sl = cols(c)
        xm = xm_next
        z_ref[0, :, sl] = _dot(h, win_ref[:, cols(c, inner)]).astype(BF16)
        vb = _dot(xm.astype(BF16), bdv_ref[c]).astype(BF16)
        v_ref[0, :, sl] = vb
        if c + 1 < n_chunks:
            xm_next = _dot(h, win_ref[:, cols(c + 1)])
        if pending is not None:
            gates = after_conv(*pending, gates)
        xcb = _silu(_short_conv(xm, cw_ref, cb_ref, sl, is_ctx)).astype(BF16)
        xc_ref[0, :, sl] = xcb
        pending = (c, xcb, vb)
    gates_ref[0] = after_conv(*pending, gates)


def _mlstm_in(xs, mod, g, w_in, conv_w, conv_b, w_q, w_k, w_v, w_gates, b_gates, *, tm, ctx_tiles):
    batch, s_len, d_model = xs.shape
    inner = w_in.shape[1] // 2
    n_gates = w_gates.shape[1]
    win = w_in.astype(BF16)
    cb = conv_b.reshape(1, inner)
    bdq, bdk, bdv = _block_diag(w_q), _block_diag(w_k), _block_diag(w_v)
    wg = jnp.pad(w_gates, ((0, 0), (0, LANES - n_gates))).astype(BF16)
    wgq, wgk, wgv = wg[:inner], wg[inner:2 * inner], wg[2 * inner:]
    bg = jnp.pad(b_gates, (0, LANES - n_gates)).reshape(1, LANES)
    gg = g.reshape(1, d_model)
    tok, out, mod_spec = _tile_specs(batch, tm, 0, ctx_tiles)
    act = jax.ShapeDtypeStruct((batch, s_len, inner), BF16)
    kernel = functools.partial(_mlstm_in_kernel, tile_off=0, ctx_tiles=ctx_tiles,
                               kscale=(inner // ML_HEADS) ** -0.5)
    return pl.pallas_call(
        kernel,
        grid=(batch, s_len // tm),
        in_specs=[tok(d_model), mod_spec(d_model), _resident(gg), _resident(win), _resident(conv_w),
                  _resident(cb), _resident(bdq), _resident(bdk), _resident(bdv),
                  _resident(wgq), _resident(wgk), _resident(wgv), _resident(bg)],
        out_specs=[out(inner)] * 5 + [out(LANES)],
        out_shape=[act] * 5 + [jax.ShapeDtypeStruct((batch, s_len, LANES), F32)],
        compiler_params=_params(("parallel", "parallel")),
        name="mlstm_in",
    )(xs, mod, gg, win, conv_w, cb, bdq, bdk, bdv, wgq, wgk, wgv, bg)


def _chunk_masks(chunk, reverse):
    t_idx = lax.broadcasted_iota(jnp.int32, (chunk, chunk), 0)
    s_idx = lax.broadcasted_iota(jnp.int32, (chunk, chunk), 1)
    if reverse:
        return s_idx >= t_idx, t_idx >= s_idx
    return s_idx <= t_idx, t_idx <= s_idx


def _lane_pick(tile, idx):
    lane = lax.broadcasted_iota(jnp.int32, (1, tile.shape[1]), 1)
    return jnp.sum(jnp.where(lane == idx, tile, 0.0), axis=1, keepdims=True)


def _sublane_pick(tile, idx):
    sub = lax.broadcasted_iota(jnp.int32, (tile.shape[0], 1), 0)
    return jnp.sum(jnp.where(sub == idx, tile, 0.0), axis=0, keepdims=True)


def _mlstm_scan_kernel(qf_ref, kf_ref, vf_ref, gf_ref, qb_ref, kb_ref, vb_ref, gb_ref,
                       hf_ref, hb_ref, c_ref, n_ref, m_ref):
    chunk = qf_ref.shape[1]
    head = pl.program_id(1)

    @pl.when(pl.program_id(2) == 0)
    def _():
        c_ref[...] = jnp.zeros_like(c_ref)
        n_ref[...] = jnp.zeros_like(n_ref)
        m_ref[...] = jnp.zeros_like(m_ref)

    dirs = ((qf_ref, kf_ref, vf_ref, gf_ref, hf_ref), (qb_ref, kb_ref, vb_ref, gb_ref, hb_ref))
    for d, (q_ref, k_ref, v_ref, g_ref, o_ref) in enumerate(dirs):
        tri, tri_t = _chunk_masks(chunk, reverse=(d == 1))
        gates = g_ref[0]
        gates_t = gates.T
        i_lane = d * 2 * ML_HEADS + head
        f_lane = i_lane + ML_HEADS
        ig_col, ig_row = _lane_pick(gates, i_lane), _sublane_pick(gates_t, i_lane)
        lf_col = _log_sigmoid(_lane_pick(gates, f_lane))
        lf_row = _log_sigmoid(_sublane_pick(gates_t, f_lane))
        b_col = jnp.sum(jnp.where(tri, lf_row, 0.0), axis=1, keepdims=True)
        b_row = jnp.sum(jnp.where(tri_t, lf_col, 0.0), axis=0, keepdims=True)
        b_end = jnp.sum(lf_col, axis=0, keepdims=True)
        m_prev = m_ref[d, 0:1, 0:1]
        dm = jnp.where(tri, b_col - b_row + ig_row, -jnp.inf)
        m_inter = b_col + m_prev
        m_t = jnp.maximum(m_inter, jnp.max(dm, axis=1, keepdims=True))
        q, k, v = q_ref[0], k_ref[0], v_ref[0]
        s = _dot_nt(q, k) * jnp.exp(dm - m_t)
        sc = jnp.exp(m_inter - m_t)
        cmat, nvec = c_ref[d], n_ref[d]
        num = _dot(s.astype(BF16), v) + sc * _dot(q, cmat.astype(BF16))
        den = jnp.sum(s, axis=1, keepdims=True) + sc * jnp.sum(q.astype(F32) * nvec, axis=1, keepdims=True)
        o_ref[0] = (num * (1.0 / jnp.maximum(jnp.abs(den), jnp.exp(-m_t)))).astype(o_ref.dtype)
        gl = b_end - b_col + ig_col
        m_new = jnp.maximum(b_end + m_prev, jnp.max(gl, axis=0, keepdims=True))
        kw = k.astype(F32) * jnp.exp(gl - m_new)
        dec = jnp.exp(b_end + m_prev - m_new)
        c_ref[d] = dec * cmat + _dot_tn(kw.astype(BF16), v)
        n_ref[d] = dec * nvec + jnp.sum(kw, axis=0, keepdims=True)
        m_ref[d] = jnp.broadcast_to(m_new, m_ref.shape[1:])


def _scan_chunk_maps(n_chunks, ctx_chunks):
    def fwd(i):
        return i

    def bwd(i):
        return jnp.where(i < ctx_chunks, ctx_chunks - 1 - i, n_chunks - 1 + ctx_chunks - i)

    return fwd, bwd


def _mlstm_scan(q, k, v, gates, *, chunk, ctx_len):
    batch, s_len, inner = q.shape
    dh = inner // ML_HEADS
    n_chunks = s_len // chunk
    fwd, bwd = _scan_chunk_maps(n_chunks, ctx_len // chunk)

    def specs(order):
        head_spec = pl.BlockSpec((1, chunk, dh), lambda b, h, i: (b, order(i), h))
        gate_spec = pl.BlockSpec((1, chunk, LANES), lambda b, h, i: (b, order(i), 0))
        return head_spec, gate_spec

    hf_spec, gf_spec = specs(fwd)
    hb_spec, gb_spec = specs(bwd)
    out = jax.ShapeDtypeStruct((batch, s_len, inner), BF16)
    return pl.pallas_call(
        _mlstm_scan_kernel,
        grid=(batch, ML_HEADS, n_chunks),
        in_specs=[hf_spec, hf_spec, hf_spec, gf_spec, hb_spec, hb_spec, hb_spec, gb_spec],
        out_specs=[hf_spec, hb_spec],
        out_shape=[out, out],
        scratch_shapes=[pltpu.VMEM((2, dh, dh), F32), pltpu.VMEM((2, 1, dh), F32),
                        pltpu.VMEM((2, 8, LANES), F32)],
        compiler_params=_params(("parallel", "parallel", "arbitrary")),
        name="mlstm_scan",
    )(q, k, v, gates, q, k, v, gates)


def _mlstm_out_kernel(x_ref, mod_ref, hf_ref, hb_ref, xc_ref, z_ref, ng_ref, sk_ref, wout_ref, g3_ref, o_ref):
    inner = hf_ref.shape[2]
    dh = inner // ML_HEADS
    y = None
    for hd in range(ML_HEADS):
        sl = slice(hd * dh, (hd + 1) * dh)
        hs = hf_ref[0, :, sl].astype(F32) + hb_ref[0, :, sl].astype(F32)
        cen = hs - jnp.mean(hs, axis=-1, keepdims=True)
        hn = cen * lax.rsqrt(jnp.mean(cen * cen, axis=-1, keepdims=True) + EPS)
        u = hn * ng_ref[:, sl] + sk_ref[:, sl] * xc_ref[0, :, sl].astype(F32)
        u = u * _silu(z_ref[0, :, sl].astype(F32))
        part = _dot(u.astype(BF16), wout_ref[sl, :])
        y = part if y is None else y + part
    o_ref[0] = x_ref[0] + mod_ref[0, 5:6, :] * _rms(y, g3_ref[...])


def _mlstm_out(xs, mod, hf, hb, xc, z, norm_g, skip, w_out, g3, *, tm, tile_off, n_tiles, ctx_tiles):
    batch, _, d_model = xs.shape
    inner = hf.shape[2]
    ng, sk, g3r = norm_g.reshape(1, inner), skip.reshape(1, inner), g3.reshape(1, d_model)
    wout = w_out.astype(BF16)
    tok, out, mod_spec = _tile_specs(batch, tm, tile_off, ctx_tiles)
    return pl.pallas_call(
        _mlstm_out_kernel,
        grid=(batch, n_tiles),
        in_specs=[tok(d_model), mod_spec(d_model), tok(inner), tok(inner), tok(inner), tok(inner),
                  _resident(ng), _resident(sk), _resident(wout), _resident(g3r)],
        out_specs=out(d_model),
        out_shape=jax.ShapeDtypeStruct((batch, n_tiles * tm, d_model), F32),
        compiler_params=_params(("parallel", "parallel")),
        name="mlstm_out",
    )(xs, mod, hf, hb, xc, z, ng, sk, wout, g3r)


def _split3(x):
    hi = x.astype(BF16)
    r1 = x - hi.astype(F32)
    mid = r1.astype(BF16)
    lo = (r1 - mid.astype(F32)).astype(BF16)
    return hi, mid, lo


def _dot3(a, pieces):
    return _dot(a, pieces[0]) + _dot(a, pieces[1]) + _dot(a, pieces[2])


def _ssd_lane_source():
    src = []
    for d in range(2):
        for g in range(SSD_GROUPS):
            for _ in range(2):
                for r in range(SSD_HPG):
                    src.append(d * SSD_GROUPS * SSD_HPG + g * SSD_HPG + r)
    return jnp.array(src, jnp.int32)


def _ssd_in_kernel(x_ref, mod_ref, g_ref, wz_ref, wx_ref, wdt_ref, cw_ref, cb_ref, dtb_ref, alog_ref,
                   tril_ref, triu_ref,
                   z_ref, xs_ref, xst_ref, bm_ref, cm_ref, dcol_ref, drow_ref, *, tile_off, ctx_tiles):
    tm = x_ref.shape[1]
    groups = xs_ref.shape[1]
    is_ctx = pl.program_id(1) + tile_off < ctx_tiles
    shift, scale = mod_ref[0, 3:4, :], mod_ref[0, 4:5, :]
    h = (_rms(x_ref[0], g_ref[...]) * (1.0 + scale) + shift).astype(BF16)
    half = BD_CHUNK // 2
    n_z, n_x = wz_ref.shape[1] // BD_CHUNK, wx_ref.shape[1] // BD_CHUNK
    cols = lambda c: slice(c * BD_CHUNK, (c + 1) * BD_CHUNK)
    u_next = _dot(h, wx_ref[:, cols(0)])
    for c in range(n_x):
        sl = cols(c)
        u = u_next
        for cz in range(c * n_z // n_x, (c + 1) * n_z // n_x):
            z_ref[0, :, cols(cz)] = _dot(h, wz_ref[:, cols(cz)]).astype(BF16)
        if c + 1 < n_x:
            u_next = _dot(h, wx_ref[:, cols(c + 1)])
        xbc = _silu(_short_conv(u, cw_ref, cb_ref, sl, is_ctx))
        if c < groups:
            xs_ref[0, c] = xbc.astype(BF16)
            xst_ref[0, c] = xbc.T.astype(BF16)
        else:
            gi = 2 * (c - groups)
            ref, gi = (bm_ref, gi) if gi < groups else (cm_ref, gi - groups)
            ref[0, gi] = xbc[:, :half].astype(BF16)
            ref[0, gi + 1] = xbc[:, half:].astype(BF16)
    dt = _softplus(_dot(h, wdt_ref[...]) + dtb_ref[...])
    la = _split3(dt * (-LOG2_E * jnp.exp(alog_ref[...])))
    lane = lax.broadcasted_iota(jnp.int32, (1, LANES), 1)
    cs = jnp.where(lane < LANES // 2, _dot3(tril_ref[...], la), _dot3(triu_ref[...], la))
    tile = jnp.where(jnp.bitwise_and(lane, SSD_HPG) == 0, dt, cs)
    dcol_ref[0] = tile
    drow_ref[0] = tile.T


def _ssd_in(xs, mod, g, w_in, conv_w, conv_b, dt_bias, a_log, *, tm, chunk, ctx_tiles):
    batch, s_len, d_model = xs.shape
    groups, state = SSD_GROUPS, SSD_STATE
    gw = SSD_HPG * SSD_HEAD_DIM
    gn = groups * state
    conv_dim = conv_w.shape[1]
    inner = conv_dim - 2 * gn
    assert gw == BD_CHUNK and 2 * state == BD_CHUNK and inner == groups * gw
    assert 4 * dt_bias.size == 2 * LANES
    wz = w_in[:, :inner].astype(BF16)
    wx = w_in[:, inner:inner + conv_dim].astype(BF16)
    src = _ssd_lane_source()
    wdt = w_in[:, inner + conv_dim:][:, src].astype(BF16)
    dtb = dt_bias.reshape(-1)[src].reshape(1, LANES)
    alog = a_log.reshape(-1)[src].reshape(1, LANES)
    t_idx = lax.broadcasted_iota(jnp.int32, (tm, tm), 0)
    u_idx = lax.broadcasted_iota(jnp.int32, (tm, tm), 1)
    same = (t_idx // chunk) == (u_idx // chunk)
    tril = jnp.logical_and(same, u_idx <= t_idx).astype(BF16)
    triu = jnp.logical_and(same, u_idx >= t_idx).astype(BF16)
    cb = conv_b.reshape(1, conv_dim)
    gg = g.reshape(1, d_model)
    tok, out, mod_spec = _tile_specs(batch, tm, 0, ctx_tiles)
    kernel = functools.partial(_ssd_in_kernel, tile_off=0, ctx_tiles=ctx_tiles)
    grp = lambda w: pl.BlockSpec((1, groups, tm, w), lambda b, j: (b, 0, j, 0))
    return pl.pallas_call(
        kernel,
        grid=(batch, s_len // tm),
        in_specs=[tok(d_model), mod_spec(d_model), _resident(gg), _resident(wz), _resident(wx), _resident(wdt),
                  _resident(conv_w), _resident(cb), _resident(dtb), _resident(alog), _resident(tril), _resident(triu)],
        out_specs=[out(inner), grp(gw), pl.BlockSpec((1, groups, gw, tm), lambda b, j: (b, 0, 0, j)),
                   grp(state), grp(state), out(LANES), pl.BlockSpec((1, LANES, tm), lambda b, j: (b, 0, j))],
        out_shape=[jax.ShapeDtypeStruct((batch, s_len, inner), BF16),
                   jax.ShapeDtypeStruct((batch, groups, s_len, gw), BF16),
                   jax.ShapeDtypeStruct((batch, groups, gw, s_len), BF16),
                   jax.ShapeDtypeStruct((batch, groups, s_len, state), BF16),
                   jax.ShapeDtypeStruct((batch, groups, s_len, state), BF16),
                   jax.ShapeDtypeStruct((batch, s_len, LANES), F32),
                   jax.ShapeDtypeStruct((batch, LANES, s_len), F32)],
        compiler_params=_params(("parallel", "parallel")),
        name="ssd_in",
    )(xs, mod, gg, wz, wx, wdt, conv_w, cb, dtb, alog, tril, triu)


def _ssd_scan_kernel(xf_ref, xtf_ref, bf_ref, cf_ref, dcf_ref, drf_ref,
                     xb_ref, xtb_ref, bb_ref, cb_ref, dcb_ref, drb_ref,
                     yf_ref, yb_ref, h_ref):
    groups, chunk = xf_ref.shape[1], xf_ref.shape[2]
    p = SSD_HEAD_DIM

    @pl.when(pl.program_id(1) == 0)
    def _():
        h_ref[...] = jnp.zeros_like(h_ref)

    dirs = ((xf_ref, xtf_ref, bf_ref, cf_ref, dcf_ref, drf_ref, yf_ref),
            (xb_ref, xtb_ref, bb_ref, cb_ref, dcb_ref, drb_ref, yb_ref))
    masks = (_chunk_masks(chunk, reverse=False)[0], _chunk_masks(chunk, reverse=True)[0])

    low_half = lax.broadcasted_iota(jnp.int32, (1, 2 * p), 1) < p
    pairs = [slice(k * 2 * p, (k + 1) * 2 * p) for k in range(SSD_HPG // 2)]

    def front(g, d):
        bm, cm = dirs[d][2][0, g], dirs[d][3][0, g]
        return dict(g=g, d=d, bm=bm, cm=cm, cbm=_dot_nt(cm, bm))

    def middle(st):
        g, d = st["g"], st["d"]
        _, xt_ref, _, _, dc_ref, dr_ref, _ = dirs[d]
        base = d * (LANES // 2) + g * 2 * SSD_HPG
        dcol, xt = dc_ref[0], xt_ref[0, g]
        end = 0 if d == 1 else chunk - 1
        mm, ecol, xw, dec = [], [], [], []
        for r in range(SSD_HPG):
            dt_row = dr_ref[0, base + r:base + r + 1, :]
            cs_row = dr_ref[0, base + SSD_HPG + r:base + SSD_HPG + r + 1, :]
            cs_col = _lane_pick(dcol, base + SSD_HPG + r)
            mmat = st["cbm"] * jnp.exp2(jnp.where(masks[d], cs_col - cs_row, -jnp.inf)) * dt_row
            mm.append(mmat.astype(BF16))
            ecol.append(jnp.exp2(cs_col))
            cs_end = cs_row[:, end:end + 1]
            xw.append((xt[r * p:(r + 1) * p, :].astype(F32) * (jnp.exp2(cs_end - cs_row) * dt_row)).astype(BF16))
            dec.append(jnp.exp2(cs_end))
        st.update(mm=mm, ecol=ecol, xw=jnp.concatenate(xw, axis=0), dec=dec)

    def back(st):
        g, d = st["g"], st["d"]
        x_ref, y_ref = dirs[d][0], dirs[d][6]
        for k, pair in enumerate(pairs):
            ch = _dot_nt(st["cm"], h_ref[d, g, pair, :].astype(BF16))
            xp = x_ref[0, g, :, pair]
            y0 = _dot(st["mm"][2 * k], xp) + st["ecol"][2 * k] * ch
            y1 = _dot(st["mm"][2 * k + 1], xp) + st["ecol"][2 * k + 1] * ch
            y_ref[0, g, :, pair] = jnp.where(low_half, y0, y1).astype(y_ref.dtype)
        upd = _dot(st["xw"], st["bm"])
        for r in range(SSD_HPG):
            hs = slice(r * p, (r + 1) * p)
            h_ref[d, g, hs, :] = st["dec"][r] * h_ref[d, g, hs, :] + upd[hs, :]

    prev = None
    for g in range(groups):
        for d in range(2):
            cur = front(g, d)
            if prev is not None:
                back(prev)
            middle(cur)
            prev = cur
    back(prev)


def _ssd_scan(xs_in, xs_t, bm, cm, dcol, drow, *, chunk, ctx_len):
    batch, groups, s_len, gw = xs_in.shape
    state = bm.shape[3]
    n_chunks = s_len // chunk
    fwd, bwd = _scan_chunk_maps(n_chunks, ctx_len // chunk)

    def specs(order):
        return [pl.BlockSpec((1, groups, chunk, gw), lambda b, i: (b, 0, order(i), 0)),
                pl.BlockSpec((1, groups, gw, chunk), lambda b, i: (b, 0, 0, order(i))),
                pl.BlockSpec((1, groups, chunk, state), lambda b, i: (b, 0, order(i), 0)),
                pl.BlockSpec((1, groups, chunk, state), lambda b, i: (b, 0, order(i), 0)),
                pl.BlockSpec((1, chunk, LANES), lambda b, i: (b, order(i), 0)),
                pl.BlockSpec((1, LANES, chunk), lambda b, i: (b, 0, order(i)))]

    sf, sb = specs(fwd), specs(bwd)
    out = jax.ShapeDtypeStruct((batch, groups, s_len, gw), BF16)
    args = (xs_in, xs_t, bm, cm, dcol, drow)
    return pl.pallas_call(
        _ssd_scan_kernel,
        grid=(batch, n_chunks),
        in_specs=sf + sb,
        out_specs=[sf[0], sb[0]],
        out_shape=[out, out],
        scratch_shapes=[pltpu.VMEM((2, groups, gw, state), F32)],
        compiler_params=_params(("parallel", "arbitrary")),
        name="ssd_scan",
    )(*args, *args)


def _ssd_out_kernel(x_ref, mod_ref, yf_ref, yb_ref, xs_ref, z_ref, dsk_ref, ng_ref, wout_ref, g3_ref, o_ref, u_ref):
    groups, gw = yf_ref.shape[1], yf_ref.shape[3]
    ssq = None
    for g in range(groups):
        sl = slice(g * gw, (g + 1) * gw)
        y = yf_ref[0, g].astype(F32) + yb_ref[0, g].astype(F32) + dsk_ref[:, sl] * xs_ref[0, g].astype(F32)
        u = y * _silu(z_ref[0, :, sl].astype(F32))
        u_ref[:, sl] = u
        part = jnp.sum(u * u, axis=-1, keepdims=True)
        ssq = part if ssq is None else ssq + part
    inv = lax.rsqrt(ssq / (groups * gw) + EPS)
    out = None
    for g in range(groups):
        sl = slice(g * gw, (g + 1) * gw)
        part = _dot((u_ref[:, sl] * inv * ng_ref[:, sl]).astype(BF16), wout_ref[sl, :])
        out = part if out is None else out + part
    o_ref[0] = x_ref[0] + mod_ref[0, 5:6, :] * _rms(out, g3_ref[...])


def _ssd_out(xs, mod, yf, yb, xs_in, z, d_skip, norm_g, w_out, g3, *, tm, tile_off, n_tiles, ctx_tiles):
    batch, _, d_model = xs.shape
    groups, gw = yf.shape[1], yf.shape[3]
    inner = groups * gw
    dsk = jnp.repeat(d_skip, SSD_HEAD_DIM).reshape(1, inner)
    ng, g3r = norm_g.reshape(1, inner), g3.reshape(1, d_model)
    wout = w_out.astype(BF16)
    tok, out, mod_spec = _tile_specs(batch, tm, tile_off, ctx_tiles)
    grp = pl.BlockSpec((1, groups, tm, gw), lambda b, j: (b, 0, j + tile_off, 0))
    return pl.pallas_call(
        _ssd_out_kernel,
        grid=(batch, n_tiles),
        in_specs=[tok(d_model), mod_spec(d_model), grp, grp, grp, tok(inner),
                  _resident(dsk), _resident(ng), _resident(wout), _resident(g3r)],
        out_specs=out(d_model),
        out_shape=jax.ShapeDtypeStruct((batch, n_tiles * tm, d_model), F32),
        scratch_shapes=[pltpu.VMEM((tm, inner), F32)],
        compiler_params=_params(("parallel", "parallel")),
        name="ssd_out",
    )(xs, mod, yf, yb, xs_in, z, dsk, ng, wout, g3r)


def _forward(x, c, ctx, c_ctx, ada_w, ada_b, norm_g, ffn_w_gate, ffn_w_up, ffn_w_down,
             mlstm_w_in, mlstm_conv_w, mlstm_conv_b, mlstm_w_q, mlstm_w_k, mlstm_w_v,
             mlstm_w_gates, mlstm_b_gates, mlstm_norm_g, mlstm_skip, mlstm_w_out,
             ssd_w_in, ssd_conv_w, ssd_conv_b, ssd_dt_bias, ssd_a_log, ssd_d,
             ssd_norm_g, ssd_w_out, *, tm, ml_chunk, ssd_chunk):
    batch, seq, d_model = x.shape
    ctx_len = ctx.shape[1]
    depth = ada_w.shape[0]
    assert ctx_len == tm and seq % tm == 0 and tm % GRID_W == 0 and tm % ml_chunk == 0 and tm % ssd_chunk == 0
    assert batch < MOD_ROWS
    xs = jnp.concatenate([ctx, x], axis=1)
    ctx_tiles = ctx_len // tm
    c_all = jnp.zeros((MOD_ROWS, d_model), F32).at[:batch].set(c).at[batch].set(c_ctx)
    for i in range(depth):
        last = i == depth - 1
        mod = _modulation(c_all, ada_w[i], ada_b[i]).reshape(MOD_ROWS, N_MOD, d_model)
        g = norm_g[i]
        n_tiles = xs.shape[1] // tm
        xs = _ffn(xs, mod, g[0], g[1], ffn_w_gate[i, 0], ffn_w_up[i, 0], ffn_w_down[i, 0],
                  j0=0, tm=tm, ctx_tiles=ctx_tiles)
        out_off = ctx_tiles if last else 0
        out_kw = dict(tm=tm, tile_off=out_off, n_tiles=n_tiles - out_off, ctx_tiles=ctx_tiles)
        j = i // 2
        if i % 2 == 0:
            q, k, v, xc, z, gates = _mlstm_in(xs, mod, g[2], mlstm_w_in[j], mlstm_conv_w[j], mlstm_conv_b[j],
                                              mlstm_w_q[j], mlstm_w_k[j], mlstm_w_v[j], mlstm_w_gates[j],
                                              mlstm_b_gates[j], tm=tm, ctx_tiles=ctx_tiles)
            hf, hb = _mlstm_scan(q, k, v, gates, chunk=ml_chunk, ctx_len=ctx_len)
            xs = _mlstm_out(xs, mod, hf, hb, xc, z, mlstm_norm_g[j], mlstm_skip[j], mlstm_w_out[j], g[3], **out_kw)
        else:
            z, xs_in, xs_t, bm, cm, dcol, drow = _ssd_in(xs, mod, g[2], ssd_w_in[j], ssd_conv_w[j], ssd_conv_b[j],
                                                         ssd_dt_bias[j], ssd_a_log[j], tm=tm, chunk=ssd_chunk,
                                                         ctx_tiles=ctx_tiles)
            yf, yb = _ssd_scan(xs_in, xs_t, bm, cm, dcol, drow, chunk=ssd_chunk, ctx_len=ctx_len)
            xs = _ssd_out(xs, mod, yf, yb, xs_in, z, ssd_d[j], ssd_norm_g[j], ssd_w_out[j], g[3], **out_kw)
        if last:
            ctx_tiles = 0
        xs = _ffn(xs, mod, g[4], g[5], ffn_w_gate[i, 1], ffn_w_up[i, 1], ffn_w_down[i, 1],
                  j0=6, tm=tm, ctx_tiles=ctx_tiles)
    return xs


def kernel(x, c, ctx, c_ctx, ada_w, ada_b, norm_g, ffn_w_gate, ffn_w_up, ffn_w_down, mlstm_w_in, mlstm_conv_w, mlstm_conv_b, mlstm_w_q, mlstm_w_k, mlstm_w_v, mlstm_w_gates, mlstm_b_gates, mlstm_norm_g, mlstm_skip, mlstm_w_out, ssd_w_in, ssd_conv_w, ssd_conv_b, ssd_dt_bias, ssd_a_log, ssd_d, ssd_norm_g, ssd_w_out):
    return _forward(x, c, ctx, c_ctx, ada_w, ada_b, norm_g, ffn_w_gate, ffn_w_up, ffn_w_down,
                    mlstm_w_in, mlstm_conv_w, mlstm_conv_b, mlstm_w_q, mlstm_w_k, mlstm_w_v,
                    mlstm_w_gates, mlstm_b_gates, mlstm_norm_g, mlstm_skip, mlstm_w_out,
                    ssd_w_in, ssd_conv_w, ssd_conv_b, ssd_dt_bias, ssd_a_log, ssd_d,
                    ssd_norm_g, ssd_w_out, tm=TOKEN_TILE, ml_chunk=MLSTM_CHUNK, ssd_chunk=SSD_CHUNK)
```

```python
import functools

import jax
import jax.numpy as jnp
from jax import lax
from jax.experimental import pallas as pl
from jax.experimental.pallas import tpu as pltpu

F32 = jnp.float32
BF16 = jnp.bfloat16

EPS = 1e-6
LOG2_E = 1.4426950408889634
GRID_W = 64
CONV_W = 5
N_MOD = 9
ML_HEADS = 4
ML_BLOCK = 4
SSD_GROUPS = 8
SSD_HPG = 4
SSD_HEAD_DIM = 64
SSD_STATE = 128

LANES = 128
SUBLANES = 8
MXU_DIM = 256
TOKEN_TILE = 256
MLSTM_CHUNK = 256
SSD_CHUNK = 128
FFN_TILES = 2
FF_CHUNK = MXU_DIM
BD_CHUNK = MXU_DIM
MOD_ROWS = 16
VMEM_LIMIT_BYTES = 56 * 1024 * 1024


def _rms(x, g):
    return x * lax.rsqrt(jnp.mean(x * x, axis=-1, keepdims=True) + EPS) * g


def _silu(x):
    return x * jax.nn.sigmoid(x)


def _softplus(x):
    return jnp.maximum(x, 0.0) + jnp.log1p(jnp.exp(-jnp.abs(x)))


def _log_sigmoid(x):
    return -_softplus(-x)


def _dot(a, b):
    return jnp.dot(a, b, preferred_element_type=F32)


def _dot_nt(a, b):
    return lax.dot_general(a, b, (((1,), (1,)), ((), ())), preferred_element_type=F32)


def _dot_tn(a, b):
    return lax.dot_general(a, b, (((0,), (0,)), ((), ())), preferred_element_type=F32)


def _resident(arr):
    nd = arr.ndim
    return pl.BlockSpec(arr.shape, lambda *_: (0,) * nd, pipeline_mode=pl.Buffered(1))


def _params(sem):
    return pltpu.CompilerParams(dimension_semantics=sem, vmem_limit_bytes=VMEM_LIMIT_BYTES)


def _tile_specs(batch, tm, tile_off, ctx_tiles):
    def tok(width):
        return pl.BlockSpec((1, tm, width), lambda b, j: (b, j + tile_off, 0))

    def out(width):
        return pl.BlockSpec((1, tm, width), lambda b, j: (b, j, 0))

    def mod_spec(d_model):
        return pl.BlockSpec((1, N_MOD, d_model),
                            lambda b, j: (jnp.where(j + tile_off < ctx_tiles, batch, b), 0, 0))

    return tok, out, mod_spec


def _mod_kernel(c_ref, w_ref, b_ref, o_ref):
    sc = _silu(c_ref[...])
    o_ref[...] = jnp.dot(sc, w_ref[...], preferred_element_type=F32,
                         precision=lax.Precision.HIGHEST) + b_ref[...]


def _modulation(c_all, w, b):
    d_model, n_out = w.shape
    return pl.pallas_call(
        _mod_kernel,
        grid=(n_out // d_model,),
        in_specs=[pl.BlockSpec(c_all.shape, lambda n: (0, 0)),
                  pl.BlockSpec((d_model, d_model), lambda n: (0, n)),
                  pl.BlockSpec((1, d_model), lambda n: (0, n))],
        out_specs=pl.BlockSpec((c_all.shape[0], d_model), lambda n: (0, n)),
        out_shape=jax.ShapeDtypeStruct((c_all.shape[0], n_out), F32),
        compiler_params=_params(("arbitrary",)),
        name="modulation",
    )(c_all, w, b.reshape(1, n_out))


def _ffn_kernel(*refs, j0, tiles_per_row, ctx_tiles, split_ctx):
    n_x = FFN_TILES * (2 if split_ctx else 1)
    x_refs, mod_refs = refs[:n_x], refs[n_x:n_x + FFN_TILES]
    gpre_ref, gpost_ref, wg_ref, wu_ref, wd_ref, o_ref, h_ref, acc_ref = refs[n_x + FFN_TILES:]
    tm = x_refs[0].shape[0]

    def tile_input(t):
        if not split_ctx:
            return x_refs[t][...]
        j = (pl.program_id(0) * FFN_TILES + t) % tiles_per_row
        return jnp.where(j < ctx_tiles, x_refs[FFN_TILES + t][...], x_refs[t][...])

    for t, mod_ref in enumerate(mod_refs):
        rows = slice(t * tm, (t + 1) * tm)
        shift, scale = mod_ref[0, j0:j0 + 1, :], mod_ref[0, j0 + 1:j0 + 2, :]
        h_ref[rows, :] = (_rms(tile_input(t), gpre_ref[...]) * (1.0 + scale) + shift).astype(BF16)
    h = h_ref[...]
    for c in range(wg_ref.shape[1] // FF_CHUNK):
        sl = slice(c * FF_CHUNK, (c + 1) * FF_CHUNK)
        a = (_silu(_dot(h, wg_ref[:, sl])) * _dot(h, wu_ref[:, sl])).astype(BF16)
        contrib = _dot(a, wd_ref[sl, :])
        if c == 0:
            acc_ref[...] = contrib
        else:
            acc_ref[...] += contrib
    for t, mod_ref in enumerate(mod_refs):
        rows = slice(t * tm, (t + 1) * tm)
        gate = mod_ref[0, j0 + 2:j0 + 3, :]
        o_ref[rows, :] = tile_input(t) + 0.5 * gate * _rms(acc_ref[rows, :], gpost_ref[...])


def _ffn(xs, mod, g_pre, g_post, w_gate, w_up, w_down, *, j0, tm, ctx_tiles, ctx_src=None):
    batch, s_len, d_model = xs.shape
    d_ff = w_gate.shape[1]
    assert d_ff % FF_CHUNK == 0
    wg, wu, wd = w_gate.astype(BF16), w_up.astype(BF16), w_down.astype(BF16)
    gpre, gpost = g_pre.reshape(1, d_model), g_post.reshape(1, d_model)
    split_ctx = ctx_src is not None
    lat_per_row = s_len // tm
    tiles_per_row = lat_per_row + (ctx_tiles if split_ctx else 0)
    n_steps = batch * tiles_per_row // FFN_TILES
    assert n_steps * FFN_TILES == batch * tiles_per_row
    rows = FFN_TILES * tm

    def split(i, t):
        tile = i * FFN_TILES + t
        return tile // tiles_per_row, tile % tiles_per_row

    def mod_spec(t):
        def index(i):
            b, j = split(i, t)
            return jnp.where(j < ctx_tiles, batch, b), 0, 0
        return pl.BlockSpec((1, N_MOD, d_model), index)

    def x_spec(t):
        if not split_ctx:
            return pl.BlockSpec((tm, d_model), lambda i: (i * FFN_TILES + t, 0))

        def index(i):
            b, j = split(i, t)
            return b * lat_per_row + jnp.maximum(j - ctx_tiles, 0), 0
        return pl.BlockSpec((tm, d_model), index)

    def ctx_spec(t):
        def index(i):
            b, j = split(i, t)
            return b * ctx_tiles + jnp.minimum(j, ctx_tiles - 1), 0
        return pl.BlockSpec((tm, d_model), index)

    x_specs = [x_spec(t) for t in range(FFN_TILES)]
    x_args = [xs.reshape(batch * s_len, d_model)] * FFN_TILES
    if split_ctx:
        x_specs += [ctx_spec(t) for t in range(FFN_TILES)]
        x_args += [ctx_src.reshape(-1, d_model)] * FFN_TILES
    out = pl.pallas_call(
        functools.partial(_ffn_kernel, j0=j0, tiles_per_row=tiles_per_row, ctx_tiles=ctx_tiles,
                          split_ctx=split_ctx),
        grid=(n_steps,),
        in_specs=x_specs + [mod_spec(t) for t in range(FFN_TILES)]
                 + [_resident(gpre), _resident(gpost), _resident(wg), _resident(wu), _resident(wd)],
        out_specs=pl.BlockSpec((rows, d_model), lambda i: (i, 0)),
        out_shape=jax.ShapeDtypeStruct((batch * tiles_per_row * tm, d_model), F32),
        scratch_shapes=[pltpu.VMEM((rows, d_model), BF16), pltpu.VMEM((rows, d_model), F32)],
        compiler_params=_params(("parallel",)),
        name="ffn",
    )(*x_args, *([mod] * FFN_TILES), gpre, gpost, wg, wu, wd)
    return out.reshape(batch, tiles_per_row * tm, d_model)


def _short_conv(u, cw_ref, cb_ref, sl, is_ctx):
    tm, width = u.shape
    n_grp, per_row = tm // SUBLANES, GRID_W // SUBLANES
    pad = CONV_W // 2
    u3 = u.reshape(n_grp, SUBLANES, width)
    sub = lax.broadcasted_iota(jnp.int32, (1, SUBLANES, 1), 1)
    zero = jnp.zeros((1, SUBLANES, width), F32)
    out = cb_ref[:, sl] + u3 * cw_ref[pad:pad + 1, sl]
    for j in range(CONV_W):
        d = j - pad
        if d == 0:
            continue
        rot = pltpu.roll(u3, shift=(-d) % SUBLANES, axis=1)
        step = 1 if d > 0 else -1
        parts, run = [], []
        for g in range(n_grp):
            src = g + step
            edge = (src % per_row == 0) if d > 0 else (g % per_row == 0)
            if not edge:
                run.append(src)
                continue
            if run:
                parts.append(rot[run[0]:run[-1] + 1])
                run = []
            inside = 0 <= src < n_grp
            parts.append(jnp.where(is_ctx, rot[src:src + 1], zero) if inside else zero)
        if run:
            parts.append(rot[run[0]:run[-1] + 1])
        neighbour = jnp.concatenate(parts, axis=0)
        from_neighbour = (sub >= SUBLANES - d) if d > 0 else (sub < -d)
        out = out + jnp.where(from_neighbour, neighbour, rot) * cw_ref[j:j + 1, sl]
    return out.reshape(tm, width)


def _block_diag(w):
    per = BD_CHUNK // ML_BLOCK
    n_tiles = w.shape[0] // per
    w = w.reshape(n_tiles, per, ML_BLOCK, ML_BLOCK)
    eye = jnp.eye(per, dtype=w.dtype)
    return jnp.einsum("cnij,nm->cnimj", w, eye).reshape(n_tiles, BD_CHUNK, BD_CHUNK).astype(BF16)


def _mlstm_in_kernel(x_ref, mod_ref, g_ref, win_ref, cw_ref, cb_ref, bdq_ref, bdk_ref, bdv_ref,
                     wgq_ref, wgk_ref, wgv_ref, bg_ref,
                     q_ref, k_ref, v_ref, xc_ref, z_ref, gates_ref, *, tile_off, ctx_tiles, kscale):
    tm = x_ref.shape[1]
    inner = q_ref.shape[2]
    is_ctx = pl.program_id(1) + tile_off < ctx_tiles
    shift, scale = mod_ref[0, 3:4, :], mod_ref[0, 4:5, :]
    h = (_rms(x_ref[0], g_ref[...]) * (1.0 + scale) + shift).astype(BF16)
    gates = bg_ref[...]
    n_chunks = inner // BD_CHUNK
    cols = lambda c, base=0: slice(base + c * BD_CHUNK, base + (c + 1) * BD_CHUNK)
    def after_conv(c, xcb, vb, gates):
        sl = cols(c)
        q = _dot(xcb, bdq_ref[c])
        k = _dot(xcb, bdk_ref[c])
        qb, kb = q.astype(BF16), k.astype(BF16)
        q_ref[0, :, sl] = qb
        k_ref[0, :, sl] = (k * kscale).astype(BF16)
        return gates + _dot(qb, wgq_ref[sl, :]) + _dot(kb, wgk_ref[sl, :]) + _dot(vb, wgv_ref[sl, :])

    xm_next = _dot(h, win_ref[:, cols(0)])
    pending = None
    for c in range(n_chunks):
        sl = cols(c)
        xm = xm_next
        z_ref[0, :, sl] = _dot(h, win_ref[:, cols(c, inner)]).astype(BF16)
        vb = _dot(xm.astype(BF16), bdv_ref[c]).astype(BF16)
        v_ref[0, :, sl] = vb
        if c + 1 < n_chunks:
            xm_next = _dot(h, win_ref[:, cols(c + 1)])
        if pending is not None:
            gates = after_conv(*pending, gates)
        xcb = _silu(_short_conv(xm, cw_ref, cb_ref, sl, is_ctx)).astype(BF16)
        xc_ref[0, :, sl] = xcb
        pending = (c, xcb, vb)
    gates_ref[0] = after_conv(*pending, gates)


def _mlstm_in(xs, mod, g, w_in, conv_w, conv_b, w_q, w_k, w_v, w_gates, b_gates, *, tm, ctx_tiles):
    batch, s_len, d_model = xs.shape
    inner = w_in.shape[1] // 2
    n_gates = w_gates.shape[1]
    win = w_in.astype(BF16)
    cb = conv_b.reshape(1, inner)
    bdq, bdk, bdv = _block_diag(w_q), _block_diag(w_k), _block_diag(w_v)
    wg = jnp.pad(w_gates, ((0, 0), (0, LANES - n_gates))).astype(BF16)
    wgq, wgk, wgv = wg[:inner], wg[inner:2 * inner], wg[2 * inner:]
    bg = jnp.pad(b_gates, (0, LANES - n_gates)).reshape(1, LANES)
    gg = g.reshape(1, d_model)
    tok, out, mod_spec = _tile_specs(batch, tm, 0, ctx_tiles)
    act = jax.ShapeDtypeStruct((batch, s_len, inner), BF16)
    kernel = functools.partial(_mlstm_in_kernel, tile_off=0, ctx_tiles=ctx_tiles,
                               kscale=(inner // ML_HEADS) ** -0.5)
    return pl.pallas_call(
        kernel,
        grid=(batch, s_len // tm),
        in_specs=[tok(d_model), mod_spec(d_model), _resident(gg), _resident(win), _resident(conv_w),
                  _resident(cb), _resident(bdq), _resident(bdk), _resident(bdv),
                  _resident(wgq), _resident(wgk), _resident(wgv), _resident(bg)],
        out_specs=[out(inner)] * 5 + [out(LANES)],
        out_shape=[act] * 5 + [jax.ShapeDtypeStruct((batch, s_len, LANES), F32)],
        compiler_params=_params(("parallel", "parallel")),
        name="mlstm_in",
    )(xs, mod, gg, win, conv_w, cb, bdq, bdk, bdv, wgq, wgk, wgv, bg)


def _chunk_masks(chunk, reverse):
    t_idx = lax.broadcasted_iota(jnp.int32, (chunk, chunk), 0)
    s_idx = lax.broadcasted_iota(jnp.int32, (chunk, chunk), 1)
    if reverse:
        return s_idx >= t_idx, t_idx >= s_idx
    return s_idx <= t_idx, t_idx <= s_idx


def _lane_pick(tile, idx):
    lane = lax.broadcasted_iota(jnp.int32, (1, tile.shape[1]), 1)
    return jnp.sum(jnp.where(lane == idx, tile, 0.0), axis=1, keepdims=True)


def _sublane_pick(tile, idx):
    sub = lax.broadcasted_iota(jnp.int32, (tile.shape[0], 1), 0)
    return jnp.sum(jnp.where(sub == idx, tile, 0.0), axis=0, keepdims=True)


def _mlstm_scan_kernel(qf_ref, kf_ref, vf_ref, gf_ref, qb_ref, kb_ref, vb_ref, gb_ref,
                       hf_ref, hb_ref, c_ref, n_ref, m_ref):
    chunk = qf_ref.shape[1]
    dh = c_ref.shape[2]

    @pl.when(pl.program_id(1) == 0)
    def _():
        c_ref[...] = jnp.zeros_like(c_ref)
        n_ref[...] = jnp.zeros_like(n_ref)
        m_ref[...] = jnp.zeros_like(m_ref)

    dirs = ((qf_ref, kf_ref, vf_ref, gf_ref, hf_ref), (qb_ref, kb_ref, vb_ref, gb_ref, hb_ref))
    masks = (_chunk_masks(chunk, reverse=False), _chunk_masks(chunk, reverse=True))
    gates = (gf_ref[0], gb_ref[0])
    gates_t = (gates[0].T, gates[1].T)

    def front(hd, d):
        hs = slice(hd * dh, (hd + 1) * dh)
        return dict(hd=hd, d=d, hs=hs, qk=_dot_nt(dirs[d][0][0, :, hs], dirs[d][1][0, :, hs]))

    def middle(st):
        hd, d, hs = st["hd"], st["d"], st["hs"]
        q_ref, k_ref = dirs[d][0], dirs[d][1]
        tri, tri_t = masks[d]
        i_lane = d * 2 * ML_HEADS + hd
        f_lane = i_lane + ML_HEADS
        ig_col, ig_row = _lane_pick(gates[d], i_lane), _sublane_pick(gates_t[d], i_lane)
        lf_col = _log_sigmoid(_lane_pick(gates[d], f_lane))
        lf_row = _log_sigmoid(_sublane_pick(gates_t[d], f_lane))
        b_col = jnp.sum(jnp.where(tri, lf_row, 0.0), axis=1, keepdims=True)
        b_row = jnp.sum(jnp.where(tri_t, lf_col, 0.0), axis=0, keepdims=True)
        b_end = jnp.sum(lf_col, axis=0, keepdims=True)
        m_prev = m_ref[d, hd, 0:1, 0:1]
        dm = jnp.where(tri, b_col - b_row + ig_row, -jnp.inf)
        m_inter = b_col + m_prev
        m_t = jnp.maximum(m_inter, jnp.max(dm, axis=1, keepdims=True))
        s = st["qk"] * jnp.exp(dm - m_t)
        sc = jnp.exp(m_inter - m_t)
        nvec = n_ref[d, hd]
        den = (jnp.sum(s, axis=1, keepdims=True)
               + sc * jnp.sum(q_ref[0, :, hs].astype(F32) * nvec, axis=1, keepdims=True))
        gl = b_end - b_col + ig_col
        m_new = jnp.maximum(b_end + m_prev, jnp.max(gl, axis=0, keepdims=True))
        kw = k_ref[0, :, hs].astype(F32) * jnp.exp(gl - m_new)
        dec = jnp.exp(b_end + m_prev - m_new)
        n_ref[d, hd] = dec * nvec + jnp.sum(kw, axis=0, keepdims=True)
        m_ref[d, hd] = jnp.broadcast_to(m_new, m_ref.shape[2:])
        st.update(s=s.astype(BF16), sc=sc, kw=kw.astype(BF16), dec=dec,
                  inv=1.0 / jnp.maximum(jnp.abs(den), jnp.exp(-m_t)))

    def back(st):
        hd, d, hs = st["hd"], st["d"], st["hs"]
        q_ref, _, v_ref, _, o_ref = dirs[d]
        v, cmat = v_ref[0, :, hs], c_ref[d, hd]
        num = _dot(st["s"], v) + st["sc"] * _dot(q_ref[0, :, hs], cmat.astype(BF16))
        o_ref[0, :, hs] = (num * st["inv"]).astype(o_ref.dtype)
        c_ref[d, hd] = st["dec"] * cmat + _dot_tn(st["kw"], v)

    prev = None
    for hd in range(ML_HEADS):
        for d in range(2):
            cur = front(hd, d)
            if prev is not None:
                back(prev)
            middle(cur)
            prev = cur
    back(prev)


def _scan_chunk_maps(n_chunks, ctx_chunks):
    def fwd(i):
        return i

    def bwd(i):
        return jnp.where(i < ctx_chunks, ctx_chunks - 1 - i, n_chunks - 1 + ctx_chunks - i)

    return fwd, bwd


def _mlstm_scan(q, k, v, gates, *, chunk, ctx_len):
    batch, s_len, inner = q.shape
    dh = inner // ML_HEADS
    n_chunks = s_len // chunk
    fwd, bwd = _scan_chunk_maps(n_chunks, ctx_len // chunk)

    def specs(order):
        head_spec = pl.BlockSpec((1, chunk, inner), lambda b, i: (b, order(i), 0))
        gate_spec = pl.BlockSpec((1, chunk, LANES), lambda b, i: (b, order(i), 0))
        return head_spec, gate_spec

    hf_spec, gf_spec = specs(fwd)
    hb_spec, gb_spec = specs(bwd)
    out = jax.ShapeDtypeStruct((batch, s_len, inner), BF16)
    return pl.pallas_call(
        _mlstm_scan_kernel,
        grid=(batch, n_chunks),
        in_specs=[hf_spec, hf_spec, hf_spec, gf_spec, hb_spec, hb_spec, hb_spec, gb_spec],
        out_specs=[hf_spec, hb_spec],
        out_shape=[out, out],
        scratch_shapes=[pltpu.VMEM((2, ML_HEADS, dh, dh), F32), pltpu.VMEM((2, ML_HEADS, 1, dh), F32),
                        pltpu.VMEM((2, ML_HEADS, 8, LANES), F32)],
        compiler_params=_params(("parallel", "arbitrary")),
        name="mlstm_scan",
    )(q, k, v, gates, q, k, v, gates)


def _mlstm_out_kernel(x_ref, mod_ref, hf_ref, hb_ref, xc_ref, z_ref, ng_ref, sk_ref, wout_ref, g3_ref, o_ref):
    inner = hf_ref.shape[2]
    dh = inner // ML_HEADS
    y = None
    for hd in range(ML_HEADS):
        sl = slice(hd * dh, (hd + 1) * dh)
        hs = hf_ref[0, :, sl].astype(F32) + hb_ref[0, :, sl].astype(F32)
        cen = hs - jnp.mean(hs, axis=-1, keepdims=True)
        hn = cen * lax.rsqrt(jnp.mean(cen * cen, axis=-1, keepdims=True) + EPS)
        u = hn * ng_ref[:, sl] + sk_ref[:, sl] * xc_ref[0, :, sl].astype(F32)
        u = u * _silu(z_ref[0, :, sl].astype(F32))
        part = _dot(u.astype(BF16), wout_ref[sl, :])
        y = part if y is None else y + part
    o_ref[0] = x_ref[0] + mod_ref[0, 5:6, :] * _rms(y, g3_ref[...])


def _mlstm_out(xs, mod, hf, hb, xc, z, norm_g, skip, w_out, g3, *, tm, tile_off, n_tiles, ctx_tiles):
    batch, _, d_model = xs.shape
    inner = hf.shape[2]
    ng, sk, g3r = norm_g.reshape(1, inner), skip.reshape(1, inner), g3.reshape(1, d_model)
    wout = w_out.astype(BF16)
    tok, out, mod_spec = _tile_specs(batch, tm, tile_off, ctx_tiles)
    return pl.pallas_call(
        _mlstm_out_kernel,
        grid=(batch, n_tiles),
        in_specs=[tok(d_model), mod_spec(d_model), tok(inner), tok(inner), tok(inner), tok(inner),
                  _resident(ng), _resident(sk), _resident(wout), _resident(g3r)],
        out_specs=out(d_model),
        out_shape=jax.ShapeDtypeStruct((batch, n_tiles * tm, d_model), F32),
        compiler_params=_params(("parallel", "parallel")),
        name="mlstm_out",
    )(xs, mod, hf, hb, xc, z, ng, sk, wout, g3r)


def _split3(x):
    hi = x.astype(BF16)
    r1 = x - hi.astype(F32)
    mid = r1.astype(BF16)
    lo = (r1 - mid.astype(F32)).astype(BF16)
    return hi, mid, lo


def _dot3(a, pieces):
    return _dot(a, pieces[0]) + _dot(a, pieces[1]) + _dot(a, pieces[2])


def _ssd_lane_source():
    src = []
    for d in range(2):
        for g in range(SSD_GROUPS):
            for _ in range(2):
                for r in range(SSD_HPG):
                    src.append(d * SSD_GROUPS * SSD_HPG + g * SSD_HPG + r)
    return jnp.array(src, jnp.int32)


def _ssd_in_kernel(x_ref, mod_ref, g_ref, wz_ref, wx_ref, wdt_ref, cw_ref, cb_ref, dtb_ref, alog_ref,
                   tril_ref, triu_ref,
                   z_ref, xs_ref, xst_ref, bm_ref, cm_ref, dcol_ref, drow_ref, *, tile_off, ctx_tiles):
    tm = x_ref.shape[1]
    groups = xs_ref.shape[1]
    is_ctx = pl.program_id(1) + tile_off < ctx_tiles
    shift, scale = mod_ref[0, 3:4, :], mod_ref[0, 4:5, :]
    h = (_rms(x_ref[0], g_ref[...]) * (1.0 + scale) + shift).astype(BF16)
    half = BD_CHUNK // 2
    n_z, n_x = wz_ref.shape[1] // BD_CHUNK, wx_ref.shape[1] // BD_CHUNK
    cols = lambda c: slice(c * BD_CHUNK, (c + 1) * BD_CHUNK)
    u_next = _dot(h, wx_ref[:, cols(0)])
    for c in range(n_x):
        sl = cols(c)
        u = u_next
        for cz in range(c * n_z // n_x, (c + 1) * n_z // n_x):
            z_ref[0, :, cols(cz)] = _dot(h, wz_ref[:, cols(cz)]).astype(BF16)
        if c + 1 < n_x:
            u_next = _dot(h, wx_ref[:, cols(c + 1)])
        xbc = _silu(_short_conv(u, cw_ref, cb_ref, sl, is_ctx))
        if c < groups:
            xs_ref[0, c] = xbc.astype(BF16)
            xst_ref[0, c] = xbc.T.astype(BF16)
        else:
            gi = 2 * (c - groups)
            ref, gi = (bm_ref, gi) if gi < groups else (cm_ref, gi - groups)
            ref[0, gi] = xbc[:, :half].astype(BF16)
            ref[0, gi + 1] = xbc[:, half:].astype(BF16)
    dt = _softplus(_dot(h, wdt_ref[...]) + dtb_ref[...])
    la = _split3(dt * (-LOG2_E * jnp.exp(alog_ref[...])))
    lane = lax.broadcasted_iota(jnp.int32, (1, LANES), 1)
    cs = jnp.where(lane < LANES // 2, _dot3(tril_ref[...], la), _dot3(triu_ref[...], la))
    tile = jnp.where(jnp.bitwise_and(lane, SSD_HPG) == 0, dt, cs)
    dcol_ref[0] = tile
    drow_ref[0] = tile.T


def _ssd_in(xs, mod, g, w_in, conv_w, conv_b, dt_bias, a_log, *, tm, chunk, ctx_tiles):
    batch, s_len, d_model = xs.shape
    groups, state = SSD_GROUPS, SSD_STATE
    gw = SSD_HPG * SSD_HEAD_DIM
    gn = groups * state
    conv_dim = conv_w.shape[1]
    inner = conv_dim - 2 * gn
    assert gw == BD_CHUNK and 2 * state == BD_CHUNK and inner == groups * gw
    assert 4 * dt_bias.size == 2 * LANES
    wz = w_in[:, :inner].astype(BF16)
    wx = w_in[:, inner:inner + conv_dim].astype(BF16)
    src = _ssd_lane_source()
    wdt = w_in[:, inner + conv_dim:][:, src].astype(BF16)
    dtb = dt_bias.reshape(-1)[src].reshape(1, LANES)
    alog = a_log.reshape(-1)[src].reshape(1, LANES)
    t_idx = lax.broadcasted_iota(jnp.int32, (tm, tm), 0)
    u_idx = lax.broadcasted_iota(jnp.int32, (tm, tm), 1)
    same = (t_idx // chunk) == (u_idx // chunk)
    tril = jnp.logical_and(same, u_idx <= t_idx).astype(BF16)
    triu = jnp.logical_and(same, u_idx >= t_idx).astype(BF16)
    cb = conv_b.reshape(1, conv_dim)
    gg = g.reshape(1, d_model)
    tok, out, mod_spec = _tile_specs(batch, tm, 0, ctx_tiles)
    kernel = functools.partial(_ssd_in_kernel, tile_off=0, ctx_tiles=ctx_tiles)
    grp = lambda w: pl.BlockSpec((1, groups, tm, w), lambda b, j: (b, 0, j, 0))
    return pl.pallas_call(
        kernel,
        grid=(batch, s_len // tm),
        in_specs=[tok(d_model), mod_spec(d_model), _resident(gg), _resident(wz), _resident(wx), _resident(wdt),
                  _resident(conv_w), _resident(cb), _resident(dtb), _resident(alog), _resident(tril), _resident(triu)],
        out_specs=[out(inner), grp(gw), pl.BlockSpec((1, groups, gw, tm), lambda b, j: (b, 0, 0, j)),
                   grp(state), grp(state), out(LANES), pl.BlockSpec((1, LANES, tm), lambda b, j: (b, 0, j))],
        out_shape=[jax.ShapeDtypeStruct((batch, s_len, inner), BF16),
                   jax.ShapeDtypeStruct((batch, groups, s_len, gw), BF16),
                   jax.ShapeDtypeStruct((batch, groups, gw, s_len), BF16),
                   jax.ShapeDtypeStruct((batch, groups, s_len, state), BF16),
                   jax.ShapeDtypeStruct((batch, groups, s_len, state), BF16),
                   jax.ShapeDtypeStruct((batch, s_len, LANES), F32),
                   jax.ShapeDtypeStruct((batch, LANES, s_len), F32)],
        compiler_params=_params(("parallel", "parallel")),
        name="ssd_in",
    )(xs, mod, gg, wz, wx, wdt, conv_w, cb, dtb, alog, tril, triu)


def _ssd_scan_kernel(xf_ref, xtf_ref, bf_ref, cf_ref, dcf_ref, drf_ref,
                     xb_ref, xtb_ref, bb_ref, cb_ref, dcb_ref, drb_ref,
                     yf_ref, yb_ref, h_ref):
    groups, chunk = xf_ref.shape[1], xf_ref.shape[2]
    p = SSD_HEAD_DIM

    @pl.when(pl.program_id(1) == 0)
    def _():
        h_ref[...] = jnp.zeros_like(h_ref)

    dirs = ((xf_ref, xtf_ref, bf_ref, cf_ref, dcf_ref, drf_ref, yf_ref),
            (xb_ref, xtb_ref, bb_ref, cb_ref, dcb_ref, drb_ref, yb_ref))
    masks = (_chunk_masks(chunk, reverse=False)[0], _chunk_masks(chunk, reverse=True)[0])

    low_half = lax.broadcasted_iota(jnp.int32, (1, 2 * p), 1) < p
    pairs = [slice(k * 2 * p, (k + 1) * 2 * p) for k in range(SSD_HPG // 2)]

    def front(g, d):
        bm, cm = dirs[d][2][0, g], dirs[d][3][0, g]
        return dict(g=g, d=d, bm=bm, cm=cm, cbm=_dot_nt(cm, bm))

    def middle(st):
        g, d = st["g"], st["d"]
        _, xt_ref, _, _, dc_ref, dr_ref, _ = dirs[d]
        base = d * (LANES // 2) + g * 2 * SSD_HPG
        dcol, xt = dc_ref[0], xt_ref[0, g]
        end = 0 if d == 1 else chunk - 1
        mm, ecol, xw, dec = [], [], [], []
        for r in range(SSD_HPG):
            dt_row = dr_ref[0, base + r:base + r + 1, :]
            cs_row = dr_ref[0, base + SSD_HPG + r:base + SSD_HPG + r + 1, :]
            cs_col = _lane_pick(dcol, base + SSD_HPG + r)
            mmat = st["cbm"] * jnp.exp2(jnp.where(masks[d], cs_col - cs_row, -jnp.inf)) * dt_row
            mm.append(mmat.astype(BF16))
            ecol.append(jnp.exp2(cs_col))
            cs_end = cs_row[:, end:end + 1]
            xw.append((xt[r * p:(r + 1) * p, :].astype(F32) * (jnp.exp2(cs_end - cs_row) * dt_row)).astype(BF16))
            dec.append(jnp.exp2(cs_end))
        st.update(mm=mm, ecol=ecol, xw=jnp.concatenate(xw, axis=0), dec=dec)

    def back(st):
        g, d = st["g"], st["d"]
        x_ref, y_ref = dirs[d][0], dirs[d][6]
        for k, pair in enumerate(pairs):
            ch = _dot_nt(st["cm"], h_ref[d, g, pair, :].astype(BF16))
            xp = x_ref[0, g, :, pair]
            y0 = _dot(st["mm"][2 * k], xp) + st["ecol"][2 * k] * ch
            y1 = _dot(st["mm"][2 * k + 1], xp) + st["ecol"][2 * k + 1] * ch
            y_ref[0, g, :, pair] = jnp.where(low_half, y0, y1).astype(y_ref.dtype)
        upd = _dot(st["xw"], st["bm"])
        for r in range(SSD_HPG):
            hs = slice(r * p, (r + 1) * p)
            h_ref[d, g, hs, :] = st["dec"][r] * h_ref[d, g, hs, :] + upd[hs, :]

    prev = None
    for g in range(groups):
        for d in range(2):
            cur = front(g, d)
            if prev is not None:
                back(prev)
            middle(cur)
            prev = cur
    back(prev)


def _ssd_scan(xs_in, xs_t, bm, cm, dcol, drow, *, chunk, ctx_len):
    batch, groups, s_len, gw = xs_in.shape
    state = bm.shape[3]
    n_chunks = s_len // chunk
    fwd, bwd = _scan_chunk_maps(n_chunks, ctx_len // chunk)

    def specs(order):
        return [pl.BlockSpec((1, groups, chunk, gw), lambda b, i: (b, 0, order(i), 0)),
                pl.BlockSpec((1, groups, gw, chunk), lambda b, i: (b, 0, 0, order(i))),
                pl.BlockSpec((1, groups, chunk, state), lambda b, i: (b, 0, order(i), 0)),
                pl.BlockSpec((1, groups, chunk, state), lambda b, i: (b, 0, order(i), 0)),
                pl.BlockSpec((1, chunk, LANES), lambda b, i: (b, order(i), 0)),
                pl.BlockSpec((1, LANES, chunk), lambda b, i: (b, 0, order(i)))]

    sf, sb = specs(fwd), specs(bwd)
    out = jax.ShapeDtypeStruct((batch, groups, s_len, gw), BF16)
    args = (xs_in, xs_t, bm, cm, dcol, drow)
    return pl.pallas_call(
        _ssd_scan_kernel,
        grid=(batch, n_chunks),
        in_specs=sf + sb,
        out_specs=[sf[0], sb[0]],
        out_shape=[out, out],
        scratch_shapes=[pltpu.VMEM((2, groups, gw, state), F32)],
        compiler_params=_params(("parallel", "arbitrary")),
        name="ssd_scan",
    )(*args, *args)


def _ssd_out_kernel(x_ref, mod_ref, yf_ref, yb_ref, xs_ref, z_ref, dsk_ref, ng_ref, wout_ref, g3_ref, o_ref, u_ref):
    groups, gw = yf_ref.shape[1], yf_ref.shape[3]
    ssq = None
    for g in range(groups):
        sl = slice(g * gw, (g + 1) * gw)
        y = yf_ref[0, g].astype(F32) + yb_ref[0, g].astype(F32) + dsk_ref[:, sl] * xs_ref[0, g].astype(F32)
        u = y * _silu(z_ref[0, :, sl].astype(F32))
        u_ref[:, sl] = u
        part = jnp.sum(u * u, axis=-1, keepdims=True)
        ssq = part if ssq is None else ssq + part
    inv = lax.rsqrt(ssq / (groups * gw) + EPS)
    out = None
    for g in range(groups):
        sl = slice(g * gw, (g + 1) * gw)
        part = _dot((u_ref[:, sl] * inv * ng_ref[:, sl]).astype(BF16), wout_ref[sl, :])
        out = part if out is None else out + part
    o_ref[0] = x_ref[0] + mod_ref[0, 5:6, :] * _rms(out, g3_ref[...])


def _ssd_out(xs, mod, yf, yb, xs_in, z, d_skip, norm_g, w_out, g3, *, tm, tile_off, n_tiles, ctx_tiles):
    batch, _, d_model = xs.shape
    groups, gw = yf.shape[1], yf.shape[3]
    inner = groups * gw
    dsk = jnp.repeat(d_skip, SSD_HEAD_DIM).reshape(1, inner)
    ng, g3r = norm_g.reshape(1, inner), g3.reshape(1, d_model)
    wout = w_out.astype(BF16)
    tok, out, mod_spec = _tile_specs(batch, tm, tile_off, ctx_tiles)
    grp = pl.BlockSpec((1, groups, tm, gw), lambda b, j: (b, 0, j + tile_off, 0))
    return pl.pallas_call(
        _ssd_out_kernel,
        grid=(batch, n_tiles),
        in_specs=[tok(d_model), mod_spec(d_model), grp, grp, grp, tok(inner),
                  _resident(dsk), _resident(ng), _resident(wout), _resident(g3r)],
        out_specs=out(d_model),
        out_shape=jax.ShapeDtypeStruct((batch, n_tiles * tm, d_model), F32),
        scratch_shapes=[pltpu.VMEM((tm, inner), F32)],
        compiler_params=_params(("parallel", "parallel")),
        name="ssd_out",
    )(xs, mod, yf, yb, xs_in, z, dsk, ng, wout, g3r)


def _forward(x, c, ctx, c_ctx, ada_w, ada_b, norm_g, ffn_w_gate, ffn_w_up, ffn_w_down,
             mlstm_w_in, mlstm_conv_w, mlstm_conv_b, mlstm_w_q, mlstm_w_k, mlstm_w_v,
             mlstm_w_gates, mlstm_b_gates, mlstm_norm_g, mlstm_skip, mlstm_w_out,
             ssd_w_in, ssd_conv_w, ssd_conv_b, ssd_dt_bias, ssd_a_log, ssd_d,
             ssd_norm_g, ssd_w_out, *, tm, ml_chunk, ssd_chunk):
    batch, seq, d_model = x.shape
    ctx_len = ctx.shape[1]
    depth = ada_w.shape[0]
    assert ctx_len == tm and seq % tm == 0 and tm % GRID_W == 0 and tm % ml_chunk == 0 and tm % ssd_chunk == 0
    assert batch < MOD_ROWS
    xs = x
    ctx_tiles = ctx_len // tm
    c_all = jnp.zeros((MOD_ROWS, d_model), F32).at[:batch].set(c).at[batch].set(c_ctx)
    for i in range(depth):
        last = i == depth - 1
        mod = _modulation(c_all, ada_w[i], ada_b[i]).reshape(MOD_ROWS, N_MOD, d_model)
        g = norm_g[i]
        xs = _ffn(xs, mod, g[0], g[1], ffn_w_gate[i, 0], ffn_w_up[i, 0], ffn_w_down[i, 0],
                  j0=0, tm=tm, ctx_tiles=ctx_tiles, ctx_src=ctx if i == 0 else None)
        n_tiles = xs.shape[1] // tm
        out_off = ctx_tiles if last else 0
        out_kw = dict(tm=tm, tile_off=out_off, n_tiles=n_tiles - out_off, ctx_tiles=ctx_tiles)
        j = i // 2
        if i % 2 == 0:
            q, k, v, xc, z, gates = _mlstm_in(xs, mod, g[2], mlstm_w_in[j], mlstm_conv_w[j], mlstm_conv_b[j],
                                              mlstm_w_q[j], mlstm_w_k[j], mlstm_w_v[j], mlstm_w_gates[j],
                                              mlstm_b_gates[j], tm=tm, ctx_tiles=ctx_tiles)
            hf, hb = _mlstm_scan(q, k, v, gates, chunk=ml_chunk, ctx_len=ctx_len)
            xs = _mlstm_out(xs, mod, hf, hb, xc, z, mlstm_norm_g[j], mlstm_skip[j], mlstm_w_out[j], g[3], **out_kw)
        else:
            z, xs_in, xs_t, bm, cm, dcol, drow = _ssd_in(xs, mod, g[2], ssd_w_in[j], ssd_conv_w[j], ssd_conv_b[j],
                                                         ssd_dt_bias[j], ssd_a_log[j], tm=tm, chunk=ssd_chunk,
                                                         ctx_tiles=ctx_tiles)
            yf, yb = _ssd_scan(xs_in, xs_t, bm, cm, dcol, drow, chunk=ssd_chunk, ctx_len=ctx_len)
            xs = _ssd_out(xs, mod, yf, yb, xs_in, z, ssd_d[j], ssd_norm_g[j], ssd_w_out[j], g[3], **out_kw)
        if last:
            ctx_tiles = 0
        xs = _ffn(xs, mod, g[4], g[5], ffn_w_gate[i, 1], ffn_w_up[i, 1], ffn_w_down[i, 1],
                  j0=6, tm=tm, ctx_tiles=ctx_tiles)
    return xs


def kernel(x, c, ctx, c_ctx, ada_w, ada_b, norm_g, ffn_w_gate, ffn_w_up, ffn_w_down, mlstm_w_in, mlstm_conv_w, mlstm_conv_b, mlstm_w_q, mlstm_w_k, mlstm_w_v, mlstm_w_gates, mlstm_b_gates, mlstm_norm_g, mlstm_skip, mlstm_w_out, ssd_w_in, ssd_conv_w, ssd_conv_b, ssd_dt_bias, ssd_a_log, ssd_d, ssd_norm_g, ssd_w_out):
    return _forward(x, c, ctx, c_ctx, ada_w, ada_b, norm_g, ffn_w_gate, ffn_w_up, ffn_w_down,
                    mlstm_w_in, mlstm_conv_w, mlstm_conv_b, mlstm_w_q, mlstm_w_k, mlstm_w_v,
                    mlstm_w_gates, mlstm_b_gates, mlstm_norm_g, mlstm_skip, mlstm_w_out,
                    ssd_w_in, ssd_conv_w, ssd_conv_b, ssd_dt_bias, ssd_a_log, ssd_d,
                    ssd_norm_g, ssd_w_out, tm=TOKEN_TILE, ml_chunk=MLSTM_CHUNK, ssd_chunk=SSD_CHUNK)
```

```python
import functools

import jax
import jax.numpy as jnp
from jax import lax
from jax.experimental import pallas as pl
from jax.experimental.pallas import tpu as pltpu

F32 = jnp.float32
BF16 = jnp.bfloat16

EPS = 1e-6
LOG2_E = 1.4426950408889634
GRID_W = 64
CONV_W = 5
N_MOD = 9
ML_HEADS = 4
ML_BLOCK = 4
SSD_GROUPS = 8
SSD_HPG = 4
SSD_HEAD_DIM = 64
SSD_STATE = 128

LANES = 128
SUBLANES = 8
MXU_DIM = 256
TOKEN_TILE = 256
MLSTM_CHUNK = 256
SSD_CHUNK = 128
FFN_TILES = 2
FF_CHUNK = MXU_DIM
BD_CHUNK = MXU_DIM
MOD_ROWS = 16
VMEM_LIMIT_BYTES = 56 * 1024 * 1024


def _rms(x, g):
    return x * lax.rsqrt(jnp.mean(x * x, axis=-1, keepdims=True) + EPS) * g


def _silu(x):
    return x * jax.nn.sigmoid(x)


def _softplus(x):
    return jnp.maximum(x, 0.0) + jnp.log1p(jnp.exp(-jnp.abs(x)))


def _log_sigmoid(x):
    return -_softplus(-x)


def _dot(a, b):
    return jnp.dot(a, b, preferred_element_type=F32)


def _dot_nt(a, b):
    return lax.dot_general(a, b, (((1,), (1,)), ((), ())), preferred_element_type=F32)


def _dot_tn(a, b):
    return lax.dot_general(a, b, (((0,), (0,)), ((), ())), preferred_element_type=F32)


def _resident(arr):
    nd = arr.ndim
    return pl.BlockSpec(arr.shape, lambda *_: (0,) * nd, pipeline_mode=pl.Buffered(1))


def _params(sem):
    return pltpu.CompilerParams(dimension_semantics=sem, vmem_limit_bytes=VMEM_LIMIT_BYTES)


def _tile_specs(batch, tm, tile_off, ctx_tiles):
    def tok(width):
        return pl.BlockSpec((1, tm, width), lambda b, j: (b, j + tile_off, 0))

    def out(width):
        return pl.BlockSpec((1, tm, width), lambda b, j: (b, j, 0))

    def mod_spec(d_model):
        return pl.BlockSpec((1, N_MOD, d_model),
                            lambda b, j: (jnp.where(j + tile_off < ctx_tiles, batch, b), 0, 0))

    return tok, out, mod_spec


def _mod_kernel(c_ref, w_ref, b_ref, o_ref):
    sc = _silu(c_ref[...])
    o_ref[...] = jnp.dot(sc, w_ref[...], preferred_element_type=F32,
                         precision=lax.Precision.HIGHEST) + b_ref[...]


def _modulation(c_all, w, b):
    d_model, n_out = w.shape
    return pl.pallas_call(
        _mod_kernel,
        grid=(n_out // d_model,),
        in_specs=[pl.BlockSpec(c_all.shape, lambda n: (0, 0)),
                  pl.BlockSpec((d_model, d_model), lambda n: (0, n)),
                  pl.BlockSpec((1, d_model), lambda n: (0, n))],
        out_specs=pl.BlockSpec((c_all.shape[0], d_model), lambda n: (0, n)),
        out_shape=jax.ShapeDtypeStruct((c_all.shape[0], n_out), F32),
        compiler_params=_params(("arbitrary",)),
        name="modulation",
    )(c_all, w, b.reshape(1, n_out))


def _ffn_kernel(*refs, j0, tiles_per_row, ctx_tiles, split_ctx):
    n_x = FFN_TILES * (2 if split_ctx else 1)
    x_refs, mod_refs = refs[:n_x], refs[n_x:n_x + FFN_TILES]
    gpre_ref, gpost_ref, wg_ref, wu_ref, wd_ref, o_ref, h_ref, acc_ref = refs[n_x + FFN_TILES:]
    tm = x_refs[0].shape[0]

    def tile_input(t):
        if not split_ctx:
            return x_refs[t][...]
        j = (pl.program_id(0) * FFN_TILES + t) % tiles_per_row
        return jnp.where(j < ctx_tiles, x_refs[FFN_TILES + t][...], x_refs[t][...])

    for t, mod_ref in enumerate(mod_refs):
        rows = slice(t * tm, (t + 1) * tm)
        shift, scale = mod_ref[0, j0:j0 + 1, :], mod_ref[0, j0 + 1:j0 + 2, :]
        h_ref[rows, :] = (_rms(tile_input(t), gpre_ref[...]) * (1.0 + scale) + shift).astype(BF16)
    h = h_ref[...]
    for c in range(wg_ref.shape[1] // FF_CHUNK):
        sl = slice(c * FF_CHUNK, (c + 1) * FF_CHUNK)
        a = (_silu(_dot(h, wg_ref[:, sl])) * _dot(h, wu_ref[:, sl])).astype(BF16)
        contrib = _dot(a, wd_ref[sl, :])
        if c == 0:
            acc_ref[...] = contrib
        else:
            acc_ref[...] += contrib
    for t, mod_ref in enumerate(mod_refs):
        rows = slice(t * tm, (t + 1) * tm)
        gate = mod_ref[0, j0 + 2:j0 + 3, :]
        o_ref[rows, :] = tile_input(t) + 0.5 * gate * _rms(acc_ref[rows, :], gpost_ref[...])


def _ffn(xs, mod, g_pre, g_post, weights, which, *, j0, tm, ctx_tiles, ctx_src=None):
    batch, s_len, d_model = xs.shape
    wg, wu, wd = weights
    assert wg.shape[-1] % FF_CHUNK == 0

    def stacked(w):
        return pl.BlockSpec((None, None) + w.shape[2:], lambda i: which + (0, 0), pipeline_mode=pl.Buffered(1))

    gpre, gpost = g_pre.reshape(1, d_model), g_post.reshape(1, d_model)
    split_ctx = ctx_src is not None
    lat_per_row = s_len // tm
    tiles_per_row = lat_per_row + (ctx_tiles if split_ctx else 0)
    n_steps = batch * tiles_per_row // FFN_TILES
    assert n_steps * FFN_TILES == batch * tiles_per_row
    rows = FFN_TILES * tm

    def split(i, t):
        tile = i * FFN_TILES + t
        return tile // tiles_per_row, tile % tiles_per_row

    def mod_spec(t):
        def index(i):
            b, j = split(i, t)
            return jnp.where(j < ctx_tiles, batch, b), 0, 0
        return pl.BlockSpec((1, N_MOD, d_model), index)

    def x_spec(t):
        if not split_ctx:
            return pl.BlockSpec((tm, d_model), lambda i: (i * FFN_TILES + t, 0))

        def index(i):
            b, j = split(i, t)
            return b * lat_per_row + jnp.maximum(j - ctx_tiles, 0), 0
        return pl.BlockSpec((tm, d_model), index)

    def ctx_spec(t):
        def index(i):
            b, j = split(i, t)
            return b * ctx_tiles + jnp.minimum(j, ctx_tiles - 1), 0
        return pl.BlockSpec((tm, d_model), index)

    x_specs = [x_spec(t) for t in range(FFN_TILES)]
    x_args = [xs.reshape(batch * s_len, d_model)] * FFN_TILES
    if split_ctx:
        x_specs += [ctx_spec(t) for t in range(FFN_TILES)]
        x_args += [ctx_src.reshape(-1, d_model)] * FFN_TILES
    out = pl.pallas_call(
        functools.partial(_ffn_kernel, j0=j0, tiles_per_row=tiles_per_row, ctx_tiles=ctx_tiles,
                          split_ctx=split_ctx),
        grid=(n_steps,),
        in_specs=x_specs + [mod_spec(t) for t in range(FFN_TILES)]
                 + [_resident(gpre), _resident(gpost), stacked(wg), stacked(wu), stacked(wd)],
        out_specs=pl.BlockSpec((rows, d_model), lambda i: (i, 0)),
        out_shape=jax.ShapeDtypeStruct((batch * tiles_per_row * tm, d_model), F32),
        scratch_shapes=[pltpu.VMEM((rows, d_model), BF16), pltpu.VMEM((rows, d_model), F32)],
        compiler_params=_params(("parallel",)),
        name="ffn",
    )(*x_args, *([mod] * FFN_TILES), gpre, gpost, wg, wu, wd)
    return out.reshape(batch, tiles_per_row * tm, d_model)


def _short_conv(u, cw_ref, cb_ref, sl, is_ctx):
    tm, width = u.shape
    n_grp, per_row = tm // SUBLANES, GRID_W // SUBLANES
    pad = CONV_W // 2
    u3 = u.reshape(n_grp, SUBLANES, width)
    sub = lax.broadcasted_iota(jnp.int32, (1, SUBLANES, 1), 1)
    zero = jnp.zeros((1, SUBLANES, width), F32)
    out = cb_ref[:, sl] + u3 * cw_ref[pad:pad + 1, sl]
    for j in range(CONV_W):
        d = j - pad
        if d == 0:
            continue
        rot = pltpu.roll(u3, shift=(-d) % SUBLANES, axis=1)
        step = 1 if d > 0 else -1
        parts, run = [], []
        for g in range(n_grp):
            src = g + step
            edge = (src % per_row == 0) if d > 0 else (g % per_row == 0)
            if not edge:
                run.append(src)
                continue
            if run:
                parts.append(rot[run[0]:run[-1] + 1])
                run = []
            inside = 0 <= src < n_grp
            parts.append(jnp.where(is_ctx, rot[src:src + 1], zero) if inside else zero)
        if run:
            parts.append(rot[run[0]:run[-1] + 1])
        neighbour = jnp.concatenate(parts, axis=0)
        from_neighbour = (sub >= SUBLANES - d) if d > 0 else (sub < -d)
        out = out + jnp.where(from_neighbour, neighbour, rot) * cw_ref[j:j + 1, sl]
    return out.reshape(tm, width)


def _block_diag(w):
    per = BD_CHUNK // ML_BLOCK
    n_tiles = w.shape[0] // per
    w = w.reshape(n_tiles, per, ML_BLOCK, ML_BLOCK)
    eye = jnp.eye(per, dtype=w.dtype)
    return jnp.einsum("cnij,nm->cnimj", w, eye).reshape(n_tiles, BD_CHUNK, BD_CHUNK).astype(BF16)


def _mlstm_in_kernel(x_ref, mod_ref, g_ref, win_ref, cw_ref, cb_ref, bdq_ref, bdk_ref, bdv_ref,
                     wgq_ref, wgk_ref, wgv_ref, bg_ref,
                     q_ref, k_ref, v_ref, xc_ref, z_ref, gates_ref, *, tile_off, ctx_tiles, kscale):
    tm = x_ref.shape[1]
    inner = q_ref.shape[2]
    is_ctx = pl.program_id(1) + tile_off < ctx_tiles
    shift, scale = mod_ref[0, 3:4, :], mod_ref[0, 4:5, :]
    h = (_rms(x_ref[0], g_ref[...]) * (1.0 + scale) + shift).astype(BF16)
    gates = bg_ref[...]
    n_chunks = inner // BD_CHUNK
    cols = lambda c, base=0: slice(base + c * BD_CHUNK, base + (c + 1) * BD_CHUNK)
    def after_conv(c, xcb, vb, gates):
        sl = cols(c)
        q = _dot(xcb, bdq_ref[c])
        k = _dot(xcb, bdk_ref[c])
        qb, kb = q.astype(BF16), k.astype(BF16)
        q_ref[0, :, sl] = qb
        k_ref[0, :, sl] = (k * kscale).astype(BF16)
        return gates + _dot(qb, wgq_ref[sl, :]) + _dot(kb, wgk_ref[sl, :]) + _dot(vb, wgv_ref[sl, :])

    xm_next = _dot(h, win_ref[:, cols(0)])
    pending = None
    for c in range(n_chunks):
        sl = cols(c)
        xm = xm_next
        z_ref[0, :, sl] = _silu(_dot(h, win_ref[:, cols(c, inner)])).astype(BF16)
        vb = _dot(xm.astype(BF16), bdv_ref[c]).astype(BF16)
        v_ref[0, :, sl] = vb
        if c + 1 < n_chunks:
            xm_next = _dot(h, win_ref[:, cols(c + 1)])
        if pending is not None:
            gates = after_conv(*pending, gates)
        xcb = _silu(_short_conv(xm, cw_ref, cb_ref, sl, is_ctx)).astype(BF16)
        xc_ref[0, :, sl] = xcb
        pending = (c, xcb, vb)
    gates_ref[0] = after_conv(*pending, gates)


def _mlstm_in(xs, mod, g, w_in, conv_w, conv_b, w_q, w_k, w_v, w_gates, b_gates, *, tm, ctx_tiles):
    batch, s_len, d_model = xs.shape
    inner = w_in.shape[1] // 2
    n_gates = w_gates.shape[1]
    win = w_in.astype(BF16)
    cb = conv_b.reshape(1, inner)
    bdq, bdk, bdv = _block_diag(w_q), _block_diag(w_k), _block_diag(w_v)
    wg = jnp.pad(w_gates, ((0, 0), (0, LANES - n_gates))).astype(BF16)
    wgq, wgk, wgv = wg[:inner], wg[inner:2 * inner], wg[2 * inner:]
    bg = jnp.pad(b_gates, (0, LANES - n_gates)).reshape(1, LANES)
    gg = g.reshape(1, d_model)
    tok, out, mod_spec = _tile_specs(batch, tm, 0, ctx_tiles)
    act = jax.ShapeDtypeStruct((batch, s_len, inner), BF16)
    kernel = functools.partial(_mlstm_in_kernel, tile_off=0, ctx_tiles=ctx_tiles,
                               kscale=(inner // ML_HEADS) ** -0.5)
    return pl.pallas_call(
        kernel,
        grid=(batch, s_len // tm),
        in_specs=[tok(d_model), mod_spec(d_model), _resident(gg), _resident(win), _resident(conv_w),
                  _resident(cb), _resident(bdq), _resident(bdk), _resident(bdv),
                  _resident(wgq), _resident(wgk), _resident(wgv), _resident(bg)],
        out_specs=[out(inner)] * 5 + [out(LANES)],
        out_shape=[act] * 5 + [jax.ShapeDtypeStruct((batch, s_len, LANES), F32)],
        compiler_params=_params(("parallel", "parallel")),
        name="mlstm_in",
    )(xs, mod, gg, win, conv_w, cb, bdq, bdk, bdv, wgq, wgk, wgv, bg)


def _chunk_masks(chunk, reverse):
    t_idx = lax.broadcasted_iota(jnp.int32, (chunk, chunk), 0)
    s_idx = lax.broadcasted_iota(jnp.int32, (chunk, chunk), 1)
    if reverse:
        return s_idx >= t_idx, t_idx >= s_idx
    return s_idx <= t_idx, t_idx <= s_idx


def _lane_pick(tile, idx):
    lane = lax.broadcasted_iota(jnp.int32, (1, tile.shape[1]), 1)
    return jnp.sum(jnp.where(lane == idx, tile, 0.0), axis=1, keepdims=True)


def _sublane_pick(tile, idx):
    sub = lax.broadcasted_iota(jnp.int32, (tile.shape[0], 1), 0)
    return jnp.sum(jnp.where(sub == idx, tile, 0.0), axis=0, keepdims=True)


def _mlstm_scan_kernel(qf_ref, kf_ref, vf_ref, gf_ref, qb_ref, kb_ref, vb_ref, gb_ref,
                       hf_ref, hb_ref, c_ref, n_ref, m_ref):
    chunk = qf_ref.shape[1]
    dh = c_ref.shape[2]

    @pl.when(pl.program_id(1) == 0)
    def _():
        c_ref[...] = jnp.zeros_like(c_ref)
        n_ref[...] = jnp.zeros_like(n_ref)
        m_ref[...] = jnp.zeros_like(m_ref)

    dirs = ((qf_ref, kf_ref, vf_ref, gf_ref, hf_ref), (qb_ref, kb_ref, vb_ref, gb_ref, hb_ref))
    masks = (_chunk_masks(chunk, reverse=False), _chunk_masks(chunk, reverse=True))
    gates = (gf_ref[0], gb_ref[0])
    gates_t = (gates[0].T, gates[1].T)

    ones = jnp.ones((chunk, LANES), BF16)

    def front(hd, d):
        hs = slice(hd * dh, (hd + 1) * dh)
        q = dirs[d][0][0, :, hs]
        return dict(hd=hd, d=d, hs=hs, qk=_dot_nt(q, dirs[d][1][0, :, hs]),
                    qn=_dot(q, n_ref[d, hd].astype(BF16))[:, 0:1])

    def middle(st):
        hd, d, hs = st["hd"], st["d"], st["hs"]
        k_ref = dirs[d][1]
        tri, tri_t = masks[d]
        i_lane = d * 2 * ML_HEADS + hd
        f_lane = i_lane + ML_HEADS
        ig_col, ig_row = _lane_pick(gates[d], i_lane), _sublane_pick(gates_t[d], i_lane)
        lf_col = _log_sigmoid(_lane_pick(gates[d], f_lane))
        lf_row = _log_sigmoid(_sublane_pick(gates_t[d], f_lane))
        b_col = jnp.sum(jnp.where(tri, lf_row, 0.0), axis=1, keepdims=True)
        b_row = jnp.sum(jnp.where(tri_t, lf_col, 0.0), axis=0, keepdims=True)
        b_end = jnp.sum(lf_col, axis=0, keepdims=True)
        m_prev = m_ref[d, hd, 0:1, 0:1]
        dm = jnp.where(tri, b_col - b_row + ig_row, -jnp.inf)
        m_inter = b_col + m_prev
        m_t = jnp.maximum(m_inter, jnp.max(dm, axis=1, keepdims=True))
        s = st["qk"] * jnp.exp(dm - m_t)
        sc = jnp.exp(m_inter - m_t)
        den = jnp.sum(s, axis=1, keepdims=True) + sc * st["qn"]
        gl = b_end - b_col + ig_col
        m_new = jnp.maximum(b_end + m_prev, jnp.max(gl, axis=0, keepdims=True))
        kw = k_ref[0, :, hs].astype(F32) * jnp.exp(gl - m_new)
        dec = jnp.exp(b_end + m_prev - m_new)
        m_ref[d, hd] = jnp.broadcast_to(m_new, m_ref.shape[2:])
        st.update(s=s.astype(BF16), sc=sc, kw=kw.astype(BF16), dec=dec,
                  inv=1.0 / jnp.maximum(jnp.abs(den), jnp.exp(-m_t)))

    def back(st):
        hd, d, hs = st["hd"], st["d"], st["hs"]
        q_ref, _, v_ref, _, o_ref = dirs[d]
        v, cmat = v_ref[0, :, hs], c_ref[d, hd]
        num = _dot(st["s"], v) + st["sc"] * _dot(q_ref[0, :, hs], cmat.astype(BF16))
        o_ref[0, :, hs] = (num * st["inv"]).astype(o_ref.dtype)
        c_ref[d, hd] = st["dec"] * cmat + _dot_tn(st["kw"], v)
        n_ref[d, hd] = st["dec"] * n_ref[d, hd] + _dot_tn(st["kw"], ones)

    prev = None
    for hd in range(ML_HEADS):
        for d in range(2):
            cur = front(hd, d)
            if prev is not None:
                back(prev)
            middle(cur)
            prev = cur
    back(prev)


def _scan_chunk_maps(n_chunks, ctx_chunks):
    def fwd(i):
        return i

    def bwd(i):
        return jnp.where(i < ctx_chunks, ctx_chunks - 1 - i, n_chunks - 1 + ctx_chunks - i)

    return fwd, bwd


def _mlstm_scan(q, k, v, gates, *, chunk, ctx_len):
    batch, s_len, inner = q.shape
    dh = inner // ML_HEADS
    n_chunks = s_len // chunk
    fwd, bwd = _scan_chunk_maps(n_chunks, ctx_len // chunk)

    def specs(order):
        head_spec = pl.BlockSpec((1, chunk, inner), lambda b, i: (b, order(i), 0))
        gate_spec = pl.BlockSpec((1, chunk, LANES), lambda b, i: (b, order(i), 0))
        return head_spec, gate_spec

    hf_spec, gf_spec = specs(fwd)
    hb_spec, gb_spec = specs(bwd)
    out = jax.ShapeDtypeStruct((batch, s_len, inner), BF16)
    return pl.pallas_call(
        _mlstm_scan_kernel,
        grid=(batch, n_chunks),
        in_specs=[hf_spec, hf_spec, hf_spec, gf_spec, hb_spec, hb_spec, hb_spec, gb_spec],
        out_specs=[hf_spec, hb_spec],
        out_shape=[out, out],
        scratch_shapes=[pltpu.VMEM((2, ML_HEADS, dh, dh), F32), pltpu.VMEM((2, ML_HEADS, dh, LANES), F32),
                        pltpu.VMEM((2, ML_HEADS, 8, LANES), F32)],
        compiler_params=_params(("parallel", "arbitrary")),
        name="mlstm_scan",
    )(q, k, v, gates, q, k, v, gates)


def _mlstm_out_kernel(x_ref, mod_ref, hf_ref, hb_ref, xc_ref, z_ref, ng_ref, sk_ref, wout_ref, g3_ref, o_ref):
    inner = hf_ref.shape[2]
    dh = inner // ML_HEADS
    y = None
    for hd in range(ML_HEADS):
        sl = slice(hd * dh, (hd + 1) * dh)
        hs = hf_ref[0, :, sl].astype(F32) + hb_ref[0, :, sl].astype(F32)
        cen = hs - jnp.mean(hs, axis=-1, keepdims=True)
        hn = cen * lax.rsqrt(jnp.mean(cen * cen, axis=-1, keepdims=True) + EPS)
        u = hn * ng_ref[:, sl] + sk_ref[:, sl] * xc_ref[0, :, sl].astype(F32)
        u = u * z_ref[0, :, sl].astype(F32)
        part = _dot(u.astype(BF16), wout_ref[sl, :])
        y = part if y is None else y + part
    o_ref[0] = x_ref[0] + mod_ref[0, 5:6, :] * _rms(y, g3_ref[...])


def _mlstm_out(xs, mod, hf, hb, xc, z, norm_g, skip, w_out, g3, *, tm, tile_off, n_tiles, ctx_tiles):
    batch, _, d_model = xs.shape
    inner = hf.shape[2]
    ng, sk, g3r = norm_g.reshape(1, inner), skip.reshape(1, inner), g3.reshape(1, d_model)
    wout = w_out.astype(BF16)
    tok, out, mod_spec = _tile_specs(batch, tm, tile_off, ctx_tiles)
    return pl.pallas_call(
        _mlstm_out_kernel,
        grid=(batch, n_tiles),
        in_specs=[tok(d_model), mod_spec(d_model), tok(inner), tok(inner), tok(inner), tok(inner),
                  _resident(ng), _resident(sk), _resident(wout), _resident(g3r)],
        out_specs=out(d_model),
        out_shape=jax.ShapeDtypeStruct((batch, n_tiles * tm, d_model), F32),
        compiler_params=_params(("parallel", "parallel")),
        name="mlstm_out",
    )(xs, mod, hf, hb, xc, z, ng, sk, wout, g3r)


def _split3(x):
    hi = x.astype(BF16)
    r1 = x - hi.astype(F32)
    mid = r1.astype(BF16)
    lo = (r1 - mid.astype(F32)).astype(BF16)
    return hi, mid, lo


def _dot3(a, pieces):
    return _dot(a, pieces[0]) + _dot(a, pieces[1]) + _dot(a, pieces[2])


def _ssd_lane_source():
    src = []
    for d in range(2):
        for g in range(SSD_GROUPS):
            for _ in range(2):
                for r in range(SSD_HPG):
                    src.append(d * SSD_GROUPS * SSD_HPG + g * SSD_HPG + r)
    return jnp.array(src, jnp.int32)


def _ssd_in_kernel(x_ref, mod_ref, g_ref, wz_ref, wx_ref, wdt_ref, cw_ref, cb_ref, dtb_ref, alog_ref,
                   tril_ref, triu_ref,
                   z_ref, xs_ref, xst_ref, bm_ref, cm_ref, dcol_ref, drow_ref, *, tile_off, ctx_tiles):
    tm = x_ref.shape[1]
    groups = xs_ref.shape[1]
    is_ctx = pl.program_id(1) + tile_off < ctx_tiles
    shift, scale = mod_ref[0, 3:4, :], mod_ref[0, 4:5, :]
    h = (_rms(x_ref[0], g_ref[...]) * (1.0 + scale) + shift).astype(BF16)
    half = BD_CHUNK // 2
    n_z, n_x = wz_ref.shape[1] // BD_CHUNK, wx_ref.shape[1] // BD_CHUNK
    cols = lambda c: slice(c * BD_CHUNK, (c + 1) * BD_CHUNK)
    u_next = _dot(h, wx_ref[:, cols(0)])
    for c in range(n_x):
        sl = cols(c)
        u = u_next
        for cz in range(c * n_z // n_x, (c + 1) * n_z // n_x):
            z_ref[0, :, cols(cz)] = _dot(h, wz_ref[:, cols(cz)]).astype(BF16)
        if c + 1 < n_x:
            u_next = _dot(h, wx_ref[:, cols(c + 1)])
        xbc = _silu(_short_conv(u, cw_ref, cb_ref, sl, is_ctx))
        if c < groups:
            xs_ref[0, c] = xbc.astype(BF16)
            xst_ref[0, c] = xbc.T.astype(BF16)
        else:
            gi = 2 * (c - groups)
            ref, gi = (bm_ref, gi) if gi < groups else (cm_ref, gi - groups)
            ref[0, gi] = xbc[:, :half].astype(BF16)
            ref[0, gi + 1] = xbc[:, half:].astype(BF16)
    dt = _softplus(_dot(h, wdt_ref[...]) + dtb_ref[...])
    la = _split3(dt * (-LOG2_E * jnp.exp(alog_ref[...])))
    lane = lax.broadcasted_iota(jnp.int32, (1, LANES), 1)
    cs = jnp.where(lane < LANES // 2, _dot3(tril_ref[...], la), _dot3(triu_ref[...], la))
    tile = jnp.where(jnp.bitwise_and(lane, SSD_HPG) == 0, dt, cs)
    dcol_ref[0] = tile
    drow_ref[0] = tile.T


def _ssd_in(xs, mod, g, w_in, conv_w, conv_b, dt_bias, a_log, *, tm, chunk, ctx_tiles):
    batch, s_len, d_model = xs.shape
    groups, state = SSD_GROUPS, SSD_STATE
    gw = SSD_HPG * SSD_HEAD_DIM
    gn = groups * state
    conv_dim = conv_w.shape[1]
    inner = conv_dim - 2 * gn
    assert gw == BD_CHUNK and 2 * state == BD_CHUNK and inner == groups * gw
    assert 4 * dt_bias.size == 2 * LANES
    wz = w_in[:, :inner].astype(BF16)
    wx = w_in[:, inner:inner + conv_dim].astype(BF16)
    src = _ssd_lane_source()
    wdt = w_in[:, inner + conv_dim:][:, src].astype(BF16)
    dtb = dt_bias.reshape(-1)[src].reshape(1, LANES)
    alog = a_log.reshape(-1)[src].reshape(1, LANES)
    t_idx = lax.broadcasted_iota(jnp.int32, (tm, tm), 0)
    u_idx = lax.broadcasted_iota(jnp.int32, (tm, tm), 1)
    same = (t_idx // chunk) == (u_idx // chunk)
    tril = jnp.logical_and(same, u_idx <= t_idx).astype(BF16)
    triu = jnp.logical_and(same, u_idx >= t_idx).astype(BF16)
    cb = conv_b.reshape(1, conv_dim)
    gg = g.reshape(1, d_model)
    tok, out, mod_spec = _tile_specs(batch, tm, 0, ctx_tiles)
    kernel = functools.partial(_ssd_in_kernel, tile_off=0, ctx_tiles=ctx_tiles)
    grp = lambda w: pl.BlockSpec((1, groups, tm, w), lambda b, j: (b, 0, j, 0))
    return pl.pallas_call(
        kernel,
        grid=(batch, s_len // tm),
        in_specs=[tok(d_model), mod_spec(d_model), _resident(gg), _resident(wz), _resident(wx), _resident(wdt),
                  _resident(conv_w), _resident(cb), _resident(dtb), _resident(alog), _resident(tril), _resident(triu)],
        out_specs=[out(inner), grp(gw), pl.BlockSpec((1, groups, gw, tm), lambda b, j: (b, 0, 0, j)),
                   grp(state), grp(state), out(LANES), pl.BlockSpec((1, LANES, tm), lambda b, j: (b, 0, j))],
        out_shape=[jax.ShapeDtypeStruct((batch, s_len, inner), BF16),
                   jax.ShapeDtypeStruct((batch, groups, s_len, gw), BF16),
                   jax.ShapeDtypeStruct((batch, groups, gw, s_len), BF16),
                   jax.ShapeDtypeStruct((batch, groups, s_len, state), BF16),
                   jax.ShapeDtypeStruct((batch, groups, s_len, state), BF16),
                   jax.ShapeDtypeStruct((batch, s_len, LANES), F32),
                   jax.ShapeDtypeStruct((batch, LANES, s_len), F32)],
        compiler_params=_params(("parallel", "parallel")),
        name="ssd_in",
    )(xs, mod, gg, wz, wx, wdt, conv_w, cb, dtb, alog, tril, triu)


def _ssd_scan_kernel(xf_ref, xtf_ref, bf_ref, cf_ref, dcf_ref, drf_ref,
                     xb_ref, xtb_ref, bb_ref, cb_ref, dcb_ref, drb_ref,
                     yf_ref, yb_ref, h_ref):
    groups, chunk = xf_ref.shape[1], xf_ref.shape[2]
    p = SSD_HEAD_DIM

    @pl.when(pl.program_id(1) == 0)
    def _():
        h_ref[...] = jnp.zeros_like(h_ref)

    dirs = ((xf_ref, xtf_ref, bf_ref, cf_ref, dcf_ref, drf_ref, yf_ref),
            (xb_ref, xtb_ref, bb_ref, cb_ref, dcb_ref, drb_ref, yb_ref))
    masks = (_chunk_masks(chunk, reverse=False)[0], _chunk_masks(chunk, reverse=True)[0])

    low_half = lax.broadcasted_iota(jnp.int32, (1, 2 * p), 1) < p
    pairs = [slice(k * 2 * p, (k + 1) * 2 * p) for k in range(SSD_HPG // 2)]

    def front(g, d):
        bm, cm = dirs[d][2][0, g], dirs[d][3][0, g]
        return dict(g=g, d=d, bm=bm, cm=cm, cbm=_dot_nt(cm, bm))

    def middle(st):
        g, d = st["g"], st["d"]
        _, xt_ref, _, _, dc_ref, dr_ref, _ = dirs[d]
        base = d * (LANES // 2) + g * 2 * SSD_HPG
        dcol, xt = dc_ref[0], xt_ref[0, g]
        end = 0 if d == 1 else chunk - 1
        mm, ecol, xw, dec = [], [], [], []
        for r in range(SSD_HPG):
            dt_row = dr_ref[0, base + r:base + r + 1, :]
            cs_row = dr_ref[0, base + SSD_HPG + r:base + SSD_HPG + r + 1, :]
            cs_col = _lane_pick(dcol, base + SSD_HPG + r)
            mmat = st["cbm"] * jnp.exp2(jnp.where(masks[d], cs_col - cs_row, -jnp.inf)) * dt_row
            mm.append(mmat.astype(BF16))
            ecol.append(jnp.exp2(cs_col))
            cs_end = cs_row[:, end:end + 1]
            xw.append((xt[r * p:(r + 1) * p, :].astype(F32) * (jnp.exp2(cs_end - cs_row) * dt_row)).astype(BF16))
            dec.append(jnp.exp2(cs_end))
        st.update(mm=mm, ecol=ecol, xw=jnp.concatenate(xw, axis=0), dec=dec)

    def back(st):
        g, d = st["g"], st["d"]
        x_ref, y_ref = dirs[d][0], dirs[d][6]
        for k, pair in enumerate(pairs):
            ch = _dot_nt(st["cm"], h_ref[d, g, pair, :].astype(BF16))
            xp = x_ref[0, g, :, pair]
            y0 = _dot(st["mm"][2 * k], xp) + st["ecol"][2 * k] * ch
            y1 = _dot(st["mm"][2 * k + 1], xp) + st["ecol"][2 * k + 1] * ch
            y_ref[0, g, :, pair] = jnp.where(low_half, y0, y1).astype(y_ref.dtype)
        upd = _dot(st["xw"], st["bm"])
        for r in range(SSD_HPG):
            hs = slice(r * p, (r + 1) * p)
            h_ref[d, g, hs, :] = st["dec"][r] * h_ref[d, g, hs, :] + upd[hs, :]

    prev = None
    for g in range(groups):
        for d in range(2):
            cur = front(g, d)
            if prev is not None:
                back(prev)
            middle(cur)
            prev = cur
    back(prev)


def _ssd_scan(xs_in, xs_t, bm, cm, dcol, drow, *, chunk, ctx_len):
    batch, groups, s_len, gw = xs_in.shape
    state = bm.shape[3]
    n_chunks = s_len // chunk
    fwd, bwd = _scan_chunk_maps(n_chunks, ctx_len // chunk)

    def specs(order):
        return [pl.BlockSpec((1, groups, chunk, gw), lambda b, i: (b, 0, order(i), 0)),
                pl.BlockSpec((1, groups, gw, chunk), lambda b, i: (b, 0, 0, order(i))),
                pl.BlockSpec((1, groups, chunk, state), lambda b, i: (b, 0, order(i), 0)),
                pl.BlockSpec((1, groups, chunk, state), lambda b, i: (b, 0, order(i), 0)),
                pl.BlockSpec((1, chunk, LANES), lambda b, i: (b, order(i), 0)),
                pl.BlockSpec((1, LANES, chunk), lambda b, i: (b, 0, order(i)))]

    sf, sb = specs(fwd), specs(bwd)
    out = jax.ShapeDtypeStruct((batch, groups, s_len, gw), BF16)
    args = (xs_in, xs_t, bm, cm, dcol, drow)
    return pl.pallas_call(
        _ssd_scan_kernel,
        grid=(batch, n_chunks),
        in_specs=sf + sb,
        out_specs=[sf[0], sb[0]],
        out_shape=[out, out],
        scratch_shapes=[pltpu.VMEM((2, groups, gw, state), F32)],
        compiler_params=_params(("parallel", "arbitrary")),
        name="ssd_scan",
    )(*args, *args)


def _ssd_out_kernel(x_ref, mod_ref, yf_ref, yb_ref, xs_ref, z_ref, dsk_ref, ng_ref, wout_ref, g3_ref, o_ref, u_ref):
    groups, gw = yf_ref.shape[1], yf_ref.shape[3]
    ssq = None
    for g in range(groups):
        sl = slice(g * gw, (g + 1) * gw)
        y = yf_ref[0, g].astype(F32) + yb_ref[0, g].astype(F32) + dsk_ref[:, sl] * xs_ref[0, g].astype(F32)
        u = y * _silu(z_ref[0, :, sl].astype(F32))
        u_ref[:, sl] = u
        part = jnp.sum(u * u, axis=-1, keepdims=True)
        ssq = part if ssq is None else ssq + part
    inv = lax.rsqrt(ssq / (groups * gw) + EPS)
    out = None
    for g in range(groups):
        sl = slice(g * gw, (g + 1) * gw)
        part = _dot((u_ref[:, sl] * inv * ng_ref[:, sl]).astype(BF16), wout_ref[sl, :])
        out = part if out is None else out + part
    o_ref[0] = x_ref[0] + mod_ref[0, 5:6, :] * _rms(out, g3_ref[...])


def _ssd_out(xs, mod, yf, yb, xs_in, z, d_skip, norm_g, w_out, g3, *, tm, tile_off, n_tiles, ctx_tiles):
    batch, _, d_model = xs.shape
    groups, gw = yf.shape[1], yf.shape[3]
    inner = groups * gw
    dsk = jnp.repeat(d_skip, SSD_HEAD_DIM).reshape(1, inner)
    ng, g3r = norm_g.reshape(1, inner), g3.reshape(1, d_model)
    wout = w_out.astype(BF16)
    tok, out, mod_spec = _tile_specs(batch, tm, tile_off, ctx_tiles)
    grp = pl.BlockSpec((1, groups, tm, gw), lambda b, j: (b, 0, j + tile_off, 0))
    return pl.pallas_call(
        _ssd_out_kernel,
        grid=(batch, n_tiles),
        in_specs=[tok(d_model), mod_spec(d_model), grp, grp, grp, tok(inner),
                  _resident(dsk), _resident(ng), _resident(wout), _resident(g3r)],
        out_specs=out(d_model),
        out_shape=jax.ShapeDtypeStruct((batch, n_tiles * tm, d_model), F32),
        scratch_shapes=[pltpu.VMEM((tm, inner), F32)],
        compiler_params=_params(("parallel", "parallel")),
        name="ssd_out",
    )(xs, mod, yf, yb, xs_in, z, dsk, ng, wout, g3r)


def _forward(x, c, ctx, c_ctx, ada_w, ada_b, norm_g, ffn_w_gate, ffn_w_up, ffn_w_down,
             mlstm_w_in, mlstm_conv_w, mlstm_conv_b, mlstm_w_q, mlstm_w_k, mlstm_w_v,
             mlstm_w_gates, mlstm_b_gates, mlstm_norm_g, mlstm_skip, mlstm_w_out,
             ssd_w_in, ssd_conv_w, ssd_conv_b, ssd_dt_bias, ssd_a_log, ssd_d,
             ssd_norm_g, ssd_w_out, *, tm, ml_chunk, ssd_chunk):
    batch, seq, d_model = x.shape
    ctx_len = ctx.shape[1]
    depth = ada_w.shape[0]
    assert ctx_len == tm and seq % tm == 0 and tm % GRID_W == 0 and tm % ml_chunk == 0 and tm % ssd_chunk == 0
    assert batch < MOD_ROWS
    xs = x
    ctx_tiles = ctx_len // tm
    c_all = jnp.zeros((MOD_ROWS, d_model), F32).at[:batch].set(c).at[batch].set(c_ctx)
    ffn_weights = ffn_w_gate.astype(BF16), ffn_w_up.astype(BF16), ffn_w_down.astype(BF16)
    for i in range(depth):
        last = i == depth - 1
        mod = _modulation(c_all, ada_w[i], ada_b[i]).reshape(MOD_ROWS, N_MOD, d_model)
        g = norm_g[i]
        xs = _ffn(xs, mod, g[0], g[1], ffn_weights, (i, 0),
                  j0=0, tm=tm, ctx_tiles=ctx_tiles, ctx_src=ctx if i == 0 else None)
        n_tiles = xs.shape[1] // tm
        out_off = ctx_tiles if last else 0
        out_kw = dict(tm=tm, tile_off=out_off, n_tiles=n_tiles - out_off, ctx_tiles=ctx_tiles)
        j = i // 2
        if i % 2 == 0:
            q, k, v, xc, z, gates = _mlstm_in(xs, mod, g[2], mlstm_w_in[j], mlstm_conv_w[j], mlstm_conv_b[j],
                                              mlstm_w_q[j], mlstm_w_k[j], mlstm_w_v[j], mlstm_w_gates[j],
                                              mlstm_b_gates[j], tm=tm, ctx_tiles=ctx_tiles)
            hf, hb = _mlstm_scan(q, k, v, gates, chunk=ml_chunk, ctx_len=ctx_len)
            xs = _mlstm_out(xs, mod, hf, hb, xc, z, mlstm_norm_g[j], mlstm_skip[j], mlstm_w_out[j], g[3], **out_kw)
        else:
            z, xs_in, xs_t, bm, cm, dcol, drow = _ssd_in(xs, mod, g[2], ssd_w_in[j], ssd_conv_w[j], ssd_conv_b[j],
                                                         ssd_dt_bias[j], ssd_a_log[j], tm=tm, chunk=ssd_chunk,
                                                         ctx_tiles=ctx_tiles)
            yf, yb = _ssd_scan(xs_in, xs_t, bm, cm, dcol, drow, chunk=ssd_chunk, ctx_len=ctx_len)
            xs = _ssd_out(xs, mod, yf, yb, xs_in, z, ssd_d[j], ssd_norm_g[j], ssd_w_out[j], g[3], **out_kw)
        if last:
            ctx_tiles = 0
        xs = _ffn(xs, mod, g[4], g[5], ffn_weights, (i, 1), j0=6, tm=tm, ctx_tiles=ctx_tiles)
    return xs


def kernel(x, c, ctx, c_ctx, ada_w, ada_b, norm_g, ffn_w_gate, ffn_w_up, ffn_w_down, mlstm_w_in, mlstm_conv_w, mlstm_conv_b, mlstm_w_q, mlstm_w_k, mlstm_w_v, mlstm_w_gates, mlstm_b_gates, mlstm_norm_g, mlstm_skip, mlstm_w_out, ssd_w_in, ssd_conv_w, ssd_conv_b, ssd_dt_bias, ssd_a_log, ssd_d, ssd_norm_g, ssd_w_out):
    return _forward(x, c, ctx, c_ctx, ada_w, ada_b, norm_g, ffn_w_gate, ffn_w_up, ffn_w_down,
                    mlstm_w_in, mlstm_conv_w, mlstm_conv_b, mlstm_w_q, mlstm_w_k, mlstm_w_v,
                    mlstm_w_gates, mlstm_b_gates, mlstm_norm_g, mlstm_skip, mlstm_w_out,
                    ssd_w_in, ssd_conv_w, ssd_conv_b, ssd_dt_bias, ssd_a_log, ssd_d,
                    ssd_norm_g, ssd_w_out, tm=TOKEN_TILE, ml_chunk=MLSTM_CHUNK, ssd_chunk=SSD_CHUNK)
```

```python
import functools

import jax
import jax.numpy as jnp
from jax import lax
from jax.experimental import pallas as pl
from jax.experimental.pallas import tpu as pltpu

F32 = jnp.float32
BF16 = jnp.bfloat16

EPS = 1e-6
LOG2_E = 1.4426950408889634
GRID_W = 64
CONV_W = 5
N_MOD = 9
ML_HEADS = 4
ML_BLOCK = 4
SSD_GROUPS = 8
SSD_HPG = 4
SSD_HEAD_DIM = 64
SSD_STATE = 128

LANES = 128
SUBLANES = 8
MXU_DIM = 256
TOKEN_TILE = 256
MLSTM_CHUNK = 256
SSD_CHUNK = 128
FFN_TILES = 2
FF_CHUNK = MXU_DIM
BD_CHUNK = MXU_DIM
MOD_ROWS = 16
VMEM_LIMIT_BYTES = 56 * 1024 * 1024


def _rms(x, g):
    return x * lax.rsqrt(jnp.mean(x * x, axis=-1, keepdims=True) + EPS) * g


def _silu(x):
    return x * jax.nn.sigmoid(x)


def _softplus(x):
    return jnp.maximum(x, 0.0) + jnp.log1p(jnp.exp(-jnp.abs(x)))


def _log_sigmoid(x):
    return -_softplus(-x)


def _dot(a, b):
    return jnp.dot(a, b, preferred_element_type=F32)


def _dot_nt(a, b):
    return lax.dot_general(a, b, (((1,), (1,)), ((), ())), preferred_element_type=F32)


def _dot_tn(a, b):
    return lax.dot_general(a, b, (((0,), (0,)), ((), ())), preferred_element_type=F32)


def _resident(arr):
    nd = arr.ndim
    return pl.BlockSpec(arr.shape, lambda *_: (0,) * nd, pipeline_mode=pl.Buffered(1))


def _params(sem):
    return pltpu.CompilerParams(dimension_semantics=sem, vmem_limit_bytes=VMEM_LIMIT_BYTES)


def _tile_specs(batch, tm, tile_off, ctx_tiles):
    def tok(width):
        return pl.BlockSpec((1, tm, width), lambda b, j: (b, j + tile_off, 0))

    def out(width):
        return pl.BlockSpec((1, tm, width), lambda b, j: (b, j, 0))

    def mod_spec(d_model):
        return pl.BlockSpec((1, N_MOD, d_model),
                            lambda b, j: (jnp.where(j + tile_off < ctx_tiles, batch, b), 0, 0))

    return tok, out, mod_spec


def _mod_kernel(c_ref, w_ref, b_ref, o_ref):
    sc = _silu(c_ref[...])
    o_ref[...] = jnp.dot(sc, w_ref[...], preferred_element_type=F32,
                         precision=lax.Precision.HIGHEST) + b_ref[...]


def _modulation(c_all, w, b):
    depth, d_model, n_out = w.shape
    rows = c_all.shape[0]
    return pl.pallas_call(
        _mod_kernel,
        grid=(depth, n_out // d_model),
        in_specs=[pl.BlockSpec(c_all.shape, lambda l, n: (0, 0)),
                  pl.BlockSpec((None, d_model, d_model), lambda l, n: (l, 0, n)),
                  pl.BlockSpec((None, 1, d_model), lambda l, n: (l, 0, n))],
        out_specs=pl.BlockSpec((None, rows, d_model), lambda l, n: (l, 0, n)),
        out_shape=jax.ShapeDtypeStruct((depth, rows, n_out), F32),
        compiler_params=_params(("arbitrary", "arbitrary")),
        name="modulation",
    )(c_all, w, b.reshape(depth, 1, n_out))


def _cast_kernel(w_ref, o_ref):
    o_ref[...] = w_ref[...].astype(o_ref.dtype)


def _stack_bf16(w):
    w = w.reshape((-1,) + w.shape[2:])
    n, rows, cols = w.shape
    parts = 4
    assert rows % (parts * 16) == 0
    spec = pl.BlockSpec((None, rows // parts, cols), lambda i, r: (i, r, 0))
    return pl.pallas_call(
        _cast_kernel,
        grid=(n, parts),
        in_specs=[spec],
        out_specs=spec,
        out_shape=jax.ShapeDtypeStruct(w.shape, BF16),
        compiler_params=_params(("parallel", "parallel")),
        name="cast_bf16",
    )(w)


def _ffn_kernel(*refs, j0, tiles_per_row, ctx_tiles, split_ctx):
    n_x = FFN_TILES * (2 if split_ctx else 1)
    x_refs, mod_refs = refs[:n_x], refs[n_x:n_x + FFN_TILES]
    gpre_ref, gpost_ref, wg_ref, wu_ref, wd_ref, o_ref, h_ref, acc_ref = refs[n_x + FFN_TILES:]
    tm = x_refs[0].shape[0]

    def tile_input(t):
        if not split_ctx:
            return x_refs[t][...]
        j = (pl.program_id(0) * FFN_TILES + t) % tiles_per_row
        return jnp.where(j < ctx_tiles, x_refs[FFN_TILES + t][...], x_refs[t][...])

    for t, mod_ref in enumerate(mod_refs):
        rows = slice(t * tm, (t + 1) * tm)
        shift, scale = mod_ref[0, j0:j0 + 1, :], mod_ref[0, j0 + 1:j0 + 2, :]
        h_ref[rows, :] = (_rms(tile_input(t), gpre_ref[...]) * (1.0 + scale) + shift).astype(BF16)
    h = h_ref[...]
    for c in range(wg_ref.shape[1] // FF_CHUNK):
        sl = slice(c * FF_CHUNK, (c + 1) * FF_CHUNK)
        a = (_silu(_dot(h, wg_ref[:, sl])) * _dot(h, wu_ref[:, sl])).astype(BF16)
        contrib = _dot(a, wd_ref[sl, :])
        if c == 0:
            acc_ref[...] = contrib
        else:
            acc_ref[...] += contrib
    for t, mod_ref in enumerate(mod_refs):
        rows = slice(t * tm, (t + 1) * tm)
        gate = mod_ref[0, j0 + 2:j0 + 3, :]
        o_ref[rows, :] = tile_input(t) + 0.5 * gate * _rms(acc_ref[rows, :], gpost_ref[...])


def _ffn(xs, mod, g_pre, g_post, weights, which, *, j0, tm, ctx_tiles, ctx_src=None):
    batch, s_len, d_model = xs.shape
    wg, wu, wd = weights
    assert wg.shape[-1] % FF_CHUNK == 0

    def stacked(w):
        return pl.BlockSpec((None,) + w.shape[1:], lambda i: (which, 0, 0), pipeline_mode=pl.Buffered(1))

    gpre, gpost = g_pre.reshape(1, d_model), g_post.reshape(1, d_model)
    split_ctx = ctx_src is not None
    lat_per_row = s_len // tm
    tiles_per_row = lat_per_row + (ctx_tiles if split_ctx else 0)
    n_steps = batch * tiles_per_row // FFN_TILES
    assert n_steps * FFN_TILES == batch * tiles_per_row
    rows = FFN_TILES * tm

    def split(i, t):
        tile = i * FFN_TILES + t
        return tile // tiles_per_row, tile % tiles_per_row

    def mod_spec(t):
        def index(i):
            b, j = split(i, t)
            return jnp.where(j < ctx_tiles, batch, b), 0, 0
        return pl.BlockSpec((1, N_MOD, d_model), index)

    def x_spec(t):
        if not split_ctx:
            return pl.BlockSpec((tm, d_model), lambda i: (i * FFN_TILES + t, 0))

        def index(i):
            b, j = split(i, t)
            return b * lat_per_row + jnp.maximum(j - ctx_tiles, 0), 0
        return pl.BlockSpec((tm, d_model), index)

    def ctx_spec(t):
        def index(i):
            b, j = split(i, t)
            return b * ctx_tiles + jnp.minimum(j, ctx_tiles - 1), 0
        return pl.BlockSpec((tm, d_model), index)

    x_specs = [x_spec(t) for t in range(FFN_TILES)]
    x_args = [xs.reshape(batch * s_len, d_model)] * FFN_TILES
    if split_ctx:
        x_specs += [ctx_spec(t) for t in range(FFN_TILES)]
        x_args += [ctx_src.reshape(-1, d_model)] * FFN_TILES
    out = pl.pallas_call(
        functools.partial(_ffn_kernel, j0=j0, tiles_per_row=tiles_per_row, ctx_tiles=ctx_tiles,
                          split_ctx=split_ctx),
        grid=(n_steps,),
        in_specs=x_specs + [mod_spec(t) for t in range(FFN_TILES)]
                 + [_resident(gpre), _resident(gpost), stacked(wg), stacked(wu), stacked(wd)],
        out_specs=pl.BlockSpec((rows, d_model), lambda i: (i, 0)),
        out_shape=jax.ShapeDtypeStruct((batch * tiles_per_row * tm, d_model), F32),
        scratch_shapes=[pltpu.VMEM((rows, d_model), BF16), pltpu.VMEM((rows, d_model), F32)],
        compiler_params=_params(("parallel",)),
        name="ffn",
    )(*x_args, *([mod] * FFN_TILES), gpre, gpost, wg, wu, wd)
    return out.reshape(batch, tiles_per_row * tm, d_model)


def _short_conv(u, cw_ref, cb_ref, sl, is_ctx):
    tm, width = u.shape
    n_grp, per_row = tm // SUBLANES, GRID_W // SUBLANES
    pad = CONV_W // 2
    u3 = u.reshape(n_grp, SUBLANES, width)
    sub = lax.broadcasted_iota(jnp.int32, (1, SUBLANES, 1), 1)
    zero = jnp.zeros((1, SUBLANES, width), F32)
    out = cb_ref[:, sl] + u3 * cw_ref[pad:pad + 1, sl]
    for j in range(CONV_W):
        d = j - pad
        if d == 0:
            continue
        rot = pltpu.roll(u3, shift=(-d) % SUBLANES, axis=1)
        step = 1 if d > 0 else -1
        parts, run = [], []
        for g in range(n_grp):
            src = g + step
            edge = (src % per_row == 0) if d > 0 else (g % per_row == 0)
            if not edge:
                run.append(src)
                continue
            if run:
                parts.append(rot[run[0]:run[-1] + 1])
                run = []
            inside = 0 <= src < n_grp
            parts.append(jnp.where(is_ctx, rot[src:src + 1], zero) if inside else zero)
        if run:
            parts.append(rot[run[0]:run[-1] + 1])
        neighbour = jnp.concatenate(parts, axis=0)
        from_neighbour = (sub >= SUBLANES - d) if d > 0 else (sub < -d)
        out = out + jnp.where(from_neighbour, neighbour, rot) * cw_ref[j:j + 1, sl]
    return out.reshape(tm, width)


def _block_diag(w):
    per = BD_CHUNK // ML_BLOCK
    n_tiles = w.shape[0] // per
    w = w.reshape(n_tiles, per, ML_BLOCK, ML_BLOCK)
    eye = jnp.eye(per, dtype=w.dtype)
    return jnp.einsum("cnij,nm->cnimj", w, eye).reshape(n_tiles, BD_CHUNK, BD_CHUNK).astype(BF16)


def _mlstm_in_kernel(x_ref, mod_ref, g_ref, win_ref, cw_ref, cb_ref, bdq_ref, bdk_ref, bdv_ref,
                     wgq_ref, wgk_ref, wgv_ref, bg_ref,
                     q_ref, k_ref, v_ref, xc_ref, z_ref, gates_ref, *, tile_off, ctx_tiles, kscale):
    tm = x_ref.shape[1]
    inner = q_ref.shape[2]
    is_ctx = pl.program_id(1) + tile_off < ctx_tiles
    shift, scale = mod_ref[0, 3:4, :], mod_ref[0, 4:5, :]
    h = (_rms(x_ref[0], g_ref[...]) * (1.0 + scale) + shift).astype(BF16)
    gates = bg_ref[...]
    n_chunks = inner // BD_CHUNK
    cols = lambda c, base=0: slice(base + c * BD_CHUNK, base + (c + 1) * BD_CHUNK)
    def after_conv(c, xcb, vb, gates):
        sl = cols(c)
        q = _dot(xcb, bdq_ref[c])
        k = _dot(xcb, bdk_ref[c])
        qb, kb = q.astype(BF16), k.astype(BF16)
        q_ref[0, :, sl] = qb
        k_ref[0, :, sl] = (k * kscale).astype(BF16)
        return gates + _dot(qb, wgq_ref[sl, :]) + _dot(kb, wgk_ref[sl, :]) + _dot(vb, wgv_ref[sl, :])

    xm_next = _dot(h, win_ref[:, cols(0)])
    pending = None
    for c in range(n_chunks):
        sl = cols(c)
        xm = xm_next
        z_ref[0, :, sl] = _silu(_dot(h, win_ref[:, cols(c, inner)])).astype(BF16)
        vb = _dot(xm.astype(BF16), bdv_ref[c]).astype(BF16)
        v_ref[0, :, sl] = vb
        if c + 1 < n_chunks:
            xm_next = _dot(h, win_ref[:, cols(c + 1)])
        if pending is not None:
            gates = after_conv(*pending, gates)
        xcb = _silu(_short_conv(xm, cw_ref, cb_ref, sl, is_ctx)).astype(BF16)
        xc_ref[0, :, sl] = xcb
        pending = (c, xcb, vb)
    gates_ref[0] = after_conv(*pending, gates)


def _mlstm_in(xs, mod, g, w_in, conv_w, conv_b, w_q, w_k, w_v, w_gates, b_gates, *, tm, ctx_tiles):
    batch, s_len, d_model = xs.shape
    inner = w_in.shape[1] // 2
    n_gates = w_gates.shape[1]
    win = w_in.astype(BF16)
    cb = conv_b.reshape(1, inner)
    bdq, bdk, bdv = _block_diag(w_q), _block_diag(w_k), _block_diag(w_v)
    wg = jnp.pad(w_gates, ((0, 0), (0, LANES - n_gates))).astype(BF16)
    wgq, wgk, wgv = wg[:inner], wg[inner:2 * inner], wg[2 * inner:]
    bg = jnp.pad(b_gates, (0, LANES - n_gates)).reshape(1, LANES)
    gg = g.reshape(1, d_model)
    tok, out, mod_spec = _tile_specs(batch, tm, 0, ctx_tiles)
    act = jax.ShapeDtypeStruct((batch, s_len, inner), BF16)
    kernel = functools.partial(_mlstm_in_kernel, tile_off=0, ctx_tiles=ctx_tiles,
                               kscale=(inner // ML_HEADS) ** -0.5)
    return pl.pallas_call(
        kernel,
        grid=(batch, s_len // tm),
        in_specs=[tok(d_model), mod_spec(d_model), _resident(gg), _resident(win), _resident(conv_w),
                  _resident(cb), _resident(bdq), _resident(bdk), _resident(bdv),
                  _resident(wgq), _resident(wgk), _resident(wgv), _resident(bg)],
        out_specs=[out(inner)] * 5 + [out(LANES)],
        out_shape=[act] * 5 + [jax.ShapeDtypeStruct((batch, s_len, LANES), F32)],
        compiler_params=_params(("parallel", "parallel")),
        name="mlstm_in",
    )(xs, mod, gg, win, conv_w, cb, bdq, bdk, bdv, wgq, wgk, wgv, bg)


def _chunk_masks(chunk, reverse):
    t_idx = lax.broadcasted_iota(jnp.int32, (chunk, chunk), 0)
    s_idx = lax.broadcasted_iota(jnp.int32, (chunk, chunk), 1)
    if reverse:
        return s_idx >= t_idx, t_idx >= s_idx
    return s_idx <= t_idx, t_idx <= s_idx


def _lane_pick(tile, idx):
    lane = lax.broadcasted_iota(jnp.int32, (1, tile.shape[1]), 1)
    return jnp.sum(jnp.where(lane == idx, tile, 0.0), axis=1, keepdims=True)


def _sublane_pick(tile, idx):
    sub = lax.broadcasted_iota(jnp.int32, (tile.shape[0], 1), 0)
    return jnp.sum(jnp.where(sub == idx, tile, 0.0), axis=0, keepdims=True)


def _mlstm_scan_kernel(qf_ref, kf_ref, vf_ref, gf_ref, qb_ref, kb_ref, vb_ref, gb_ref,
                       hf_ref, hb_ref, c_ref, n_ref, m_ref):
    chunk = qf_ref.shape[1]
    dh = c_ref.shape[2]

    @pl.when(pl.program_id(1) == 0)
    def _():
        c_ref[...] = jnp.zeros_like(c_ref)
        n_ref[...] = jnp.zeros_like(n_ref)
        m_ref[...] = jnp.zeros_like(m_ref)

    dirs = ((qf_ref, kf_ref, vf_ref, gf_ref, hf_ref), (qb_ref, kb_ref, vb_ref, gb_ref, hb_ref))
    masks = (_chunk_masks(chunk, reverse=False), _chunk_masks(chunk, reverse=True))
    gates = (gf_ref[0], gb_ref[0])
    gates_t = (gates[0].T, gates[1].T)

    ones = jnp.ones((chunk, LANES), BF16)

    def front(hd, d):
        hs = slice(hd * dh, (hd + 1) * dh)
        q = dirs[d][0][0, :, hs]
        return dict(hd=hd, d=d, hs=hs, qk=_dot_nt(q, dirs[d][1][0, :, hs]),
                    qn=_dot(q, n_ref[d, hd].astype(BF16))[:, 0:1])

    def middle(st):
        hd, d, hs = st["hd"], st["d"], st["hs"]
        k_ref = dirs[d][1]
        tri, tri_t = masks[d]
        i_lane = d * 2 * ML_HEADS + hd
        f_lane = i_lane + ML_HEADS
        ig_col, ig_row = _lane_pick(gates[d], i_lane), _sublane_pick(gates_t[d], i_lane)
        lf_col = _log_sigmoid(_lane_pick(gates[d], f_lane))
        lf_row = _log_sigmoid(_sublane_pick(gates_t[d], f_lane))
        b_col = jnp.sum(jnp.where(tri, lf_row, 0.0), axis=1, keepdims=True)
        b_row = jnp.sum(jnp.where(tri_t, lf_col, 0.0), axis=0, keepdims=True)
        b_end = jnp.sum(lf_col, axis=0, keepdims=True)
        m_prev = m_ref[d, hd, 0:1, 0:1]
        dm = jnp.where(tri, b_col - b_row + ig_row, -jnp.inf)
        m_inter = b_col + m_prev
        m_t = jnp.maximum(m_inter, jnp.max(dm, axis=1, keepdims=True))
        s = st["qk"] * jnp.exp(dm - m_t)
        sc = jnp.exp(m_inter - m_t)
        den = jnp.sum(s, axis=1, keepdims=True) + sc * st["qn"]
        gl = b_end - b_col + ig_col
        m_new = jnp.maximum(b_end + m_prev, jnp.max(gl, axis=0, keepdims=True))
        kw = k_ref[0, :, hs].astype(F32) * jnp.exp(gl - m_new)
        dec = jnp.exp(b_end + m_prev - m_new)
        m_ref[d, hd] = jnp.broadcast_to(m_new, m_ref.shape[2:])
        st.update(s=s.astype(BF16), sc=sc, kw=kw.astype(BF16), dec=dec,
                  inv=1.0 / jnp.maximum(jnp.abs(den), jnp.exp(-m_t)))

    def back(st):
        hd, d, hs = st["hd"], st["d"], st["hs"]
        q_ref, _, v_ref, _, o_ref = dirs[d]
        v, cmat = v_ref[0, :, hs], c_ref[d, hd]
        num = _dot(st["s"], v) + st["sc"] * _dot(q_ref[0, :, hs], cmat.astype(BF16))
        o_ref[0, :, hs] = (num * st["inv"]).astype(o_ref.dtype)
        c_ref[d, hd] = st["dec"] * cmat + _dot_tn(st["kw"], v)
        n_ref[d, hd] = st["dec"] * n_ref[d, hd] + _dot_tn(st["kw"], ones)

    prev = None
    for hd in range(ML_HEADS):
        for d in range(2):
            cur = front(hd, d)
            if prev is not None:
                back(prev)
            middle(cur)
            prev = cur
    back(prev)


def _scan_chunk_maps(n_chunks, ctx_chunks):
    def fwd(i):
        return i

    def bwd(i):
        return jnp.where(i < ctx_chunks, ctx_chunks - 1 - i, n_chunks - 1 + ctx_chunks - i)

    return fwd, bwd


def _mlstm_scan(q, k, v, gates, *, chunk, ctx_len):
    batch, s_len, inner = q.shape
    dh = inner // ML_HEADS
    n_chunks = s_len // chunk
    fwd, bwd = _scan_chunk_maps(n_chunks, ctx_len // chunk)

    def specs(order):
        head_spec = pl.BlockSpec((1, chunk, inner), lambda b, i: (b, order(i), 0))
        gate_spec = pl.BlockSpec((1, chunk, LANES), lambda b, i: (b, order(i), 0))
        return head_spec, gate_spec

    hf_spec, gf_spec = specs(fwd)
    hb_spec, gb_spec = specs(bwd)
    out = jax.ShapeDtypeStruct((batch, s_len, inner), BF16)
    return pl.pallas_call(
        _mlstm_scan_kernel,
        grid=(batch, n_chunks),
        in_specs=[hf_spec, hf_spec, hf_spec, gf_spec, hb_spec, hb_spec, hb_spec, gb_spec],
        out_specs=[hf_spec, hb_spec],
        out_shape=[out, out],
        scratch_shapes=[pltpu.VMEM((2, ML_HEADS, dh, dh), F32), pltpu.VMEM((2, ML_HEADS, dh, LANES), F32),
                        pltpu.VMEM((2, ML_HEADS, 8, LANES), F32)],
        compiler_params=_params(("parallel", "arbitrary")),
        name="mlstm_scan",
    )(q, k, v, gates, q, k, v, gates)


def _mlstm_out_kernel(x_ref, mod_ref, hf_ref, hb_ref, xc_ref, z_ref, ng_ref, sk_ref, wout_ref, g3_ref, o_ref):
    inner = hf_ref.shape[2]
    dh = inner // ML_HEADS
    y = None
    for hd in range(ML_HEADS):
        sl = slice(hd * dh, (hd + 1) * dh)
        hs = hf_ref[0, :, sl].astype(F32) + hb_ref[0, :, sl].astype(F32)
        cen = hs - jnp.mean(hs, axis=-1, keepdims=True)
        hn = cen * lax.rsqrt(jnp.mean(cen * cen, axis=-1, keepdims=True) + EPS)
        u = hn * ng_ref[:, sl] + sk_ref[:, sl] * xc_ref[0, :, sl].astype(F32)
        u = u * z_ref[0, :, sl].astype(F32)
        part = _dot(u.astype(BF16), wout_ref[sl, :])
        y = part if y is None else y + part
    o_ref[0] = x_ref[0] + mod_ref[0, 5:6, :] * _rms(y, g3_ref[...])


def _mlstm_out(xs, mod, hf, hb, xc, z, norm_g, skip, w_out, g3, *, tm, tile_off, n_tiles, ctx_tiles):
    batch, _, d_model = xs.shape
    inner = hf.shape[2]
    ng, sk, g3r = norm_g.reshape(1, inner), skip.reshape(1, inner), g3.reshape(1, d_model)
    wout = w_out.astype(BF16)
    tok, out, mod_spec = _tile_specs(batch, tm, tile_off, ctx_tiles)
    return pl.pallas_call(
        _mlstm_out_kernel,
        grid=(batch, n_tiles),
        in_specs=[tok(d_model), mod_spec(d_model), tok(inner), tok(inner), tok(inner), tok(inner),
                  _resident(ng), _resident(sk), _resident(wout), _resident(g3r)],
        out_specs=out(d_model),
        out_shape=jax.ShapeDtypeStruct((batch, n_tiles * tm, d_model), F32),
        compiler_params=_params(("parallel", "parallel")),
        name="mlstm_out",
    )(xs, mod, hf, hb, xc, z, ng, sk, wout, g3r)


def _split3(x):
    hi = x.astype(BF16)
    r1 = x - hi.astype(F32)
    mid = r1.astype(BF16)
    lo = (r1 - mid.astype(F32)).astype(BF16)
    return hi, mid, lo


def _dot3(a, pieces):
    return _dot(a, pieces[0]) + _dot(a, pieces[1]) + _dot(a, pieces[2])


def _ssd_lane_source():
    src = []
    for d in range(2):
        for g in range(SSD_GROUPS):
            for _ in range(2):
                for r in range(SSD_HPG):
                    src.append(d * SSD_GROUPS * SSD_HPG + g * SSD_HPG + r)
    return jnp.array(src, jnp.int32)


def _ssd_in_kernel(x_ref, mod_ref, g_ref, wz_ref, wx_ref, wdt_ref, cw_ref, cb_ref, dtb_ref, alog_ref,
                   tril_ref, triu_ref,
                   z_ref, xs_ref, xst_ref, bm_ref, cm_ref, dcol_ref, drow_ref, ua_ref, ub_ref, *, tile_off, ctx_tiles):
    tm = x_ref.shape[1]
    groups = xs_ref.shape[1]
    is_ctx = pl.program_id(1) + tile_off < ctx_tiles
    shift, scale = mod_ref[0, 3:4, :], mod_ref[0, 4:5, :]
    h = (_rms(x_ref[0], g_ref[...]) * (1.0 + scale) + shift).astype(BF16)
    half = BD_CHUNK // 2
    n_z, n_x = wz_ref.shape[1] // BD_CHUNK, wx_ref.shape[1] // BD_CHUNK
    cols = lambda c: slice(c * BD_CHUNK, (c + 1) * BD_CHUNK)
    dyn0 = jnp.minimum(pl.program_id(1), 0)
    stage = (ua_ref, ub_ref)
    stage[0][dyn0] = _dot(h, wx_ref[:, cols(0)])
    for c in range(n_x):
        sl = cols(c)
        for cz in range(c * n_z // n_x, (c + 1) * n_z // n_x):
            z_ref[0, :, cols(cz)] = _dot(h, wz_ref[:, cols(cz)]).astype(BF16)
        if c + 1 < n_x:
            stage[(c + 1) % 2][dyn0] = _dot(h, wx_ref[:, cols(c + 1)])
        xbc = _silu(_short_conv(stage[c % 2][dyn0], cw_ref, cb_ref, sl, is_ctx))
        if c < groups:
            xs_ref[0, c] = xbc.astype(BF16)
            xst_ref[0, c] = xbc.T.astype(BF16)
        else:
            gi = 2 * (c - groups)
            ref, gi = (bm_ref, gi) if gi < groups else (cm_ref, gi - groups)
            ref[0, gi] = xbc[:, :half].astype(BF16)
            ref[0, gi + 1] = xbc[:, half:].astype(BF16)
    dt = _softplus(_dot(h, wdt_ref[...]) + dtb_ref[...])
    la = _split3(dt * (-LOG2_E * jnp.exp(alog_ref[...])))
    lane = lax.broadcasted_iota(jnp.int32, (1, LANES), 1)
    cs = jnp.where(lane < LANES // 2, _dot3(tril_ref[...], la), _dot3(triu_ref[...], la))
    tile = jnp.where(jnp.bitwise_and(lane, SSD_HPG) == 0, dt, cs)
    dcol_ref[0] = tile
    drow_ref[0] = tile.T


def _ssd_in(xs, mod, g, w_in, conv_w, conv_b, dt_bias, a_log, *, tm, chunk, ctx_tiles):
    batch, s_len, d_model = xs.shape
    groups, state = SSD_GROUPS, SSD_STATE
    gw = SSD_HPG * SSD_HEAD_DIM
    gn = groups * state
    conv_dim = conv_w.shape[1]
    inner = conv_dim - 2 * gn
    assert gw == BD_CHUNK and 2 * state == BD_CHUNK and inner == groups * gw
    assert 4 * dt_bias.size == 2 * LANES
    wz = w_in[:, :inner].astype(BF16)
    wx = w_in[:, inner:inner + conv_dim].astype(BF16)
    src = _ssd_lane_source()
    wdt = w_in[:, inner + conv_dim:][:, src].astype(BF16)
    dtb = dt_bias.reshape(-1)[src].reshape(1, LANES)
    alog = a_log.reshape(-1)[src].reshape(1, LANES)
    t_idx = lax.broadcasted_iota(jnp.int32, (tm, tm), 0)
    u_idx = lax.broadcasted_iota(jnp.int32, (tm, tm), 1)
    same = (t_idx // chunk) == (u_idx // chunk)
    tril = jnp.logical_and(same, u_idx <= t_idx).astype(BF16)
    triu = jnp.logical_and(same, u_idx >= t_idx).astype(BF16)
    cb = conv_b.reshape(1, conv_dim)
    gg = g.reshape(1, d_model)
    tok, out, mod_spec = _tile_specs(batch, tm, 0, ctx_tiles)
    kernel = functools.partial(_ssd_in_kernel, tile_off=0, ctx_tiles=ctx_tiles)
    grp = lambda w: pl.BlockSpec((1, groups, tm, w), lambda b, j: (b, 0, j, 0))
    return pl.pallas_call(
        kernel,
        grid=(batch, s_len // tm),
        in_specs=[tok(d_model), mod_spec(d_model), _resident(gg), _resident(wz), _resident(wx), _resident(wdt),
                  _resident(conv_w), _resident(cb), _resident(dtb), _resident(alog), _resident(tril), _resident(triu)],
        out_specs=[out(inner), grp(gw), pl.BlockSpec((1, groups, gw, tm), lambda b, j: (b, 0, 0, j)),
                   grp(state), grp(state), out(LANES), pl.BlockSpec((1, LANES, tm), lambda b, j: (b, 0, j))],
        out_shape=[jax.ShapeDtypeStruct((batch, s_len, inner), BF16),
                   jax.ShapeDtypeStruct((batch, groups, s_len, gw), BF16),
                   jax.ShapeDtypeStruct((batch, groups, gw, s_len), BF16),
                   jax.ShapeDtypeStruct((batch, groups, s_len, state), BF16),
                   jax.ShapeDtypeStruct((batch, groups, s_len, state), BF16),
                   jax.ShapeDtypeStruct((batch, s_len, LANES), F32),
                   jax.ShapeDtypeStruct((batch, LANES, s_len), F32)],
        scratch_shapes=[pltpu.VMEM((1, tm, BD_CHUNK), F32), pltpu.VMEM((1, tm, BD_CHUNK), F32)],
        compiler_params=_params(("parallel", "parallel")),
        name="ssd_in",
    )(xs, mod, gg, wz, wx, wdt, conv_w, cb, dtb, alog, tril, triu)


def _ssd_scan_kernel(xf_ref, xtf_ref, bf_ref, cf_ref, dcf_ref, drf_ref,
                     xb_ref, xtb_ref, bb_ref, cb_ref, dcb_ref, drb_ref,
                     yf_ref, yb_ref, h_ref):
    groups, chunk = xf_ref.shape[1], xf_ref.shape[2]
    p = SSD_HEAD_DIM

    @pl.when(pl.program_id(1) == 0)
    def _():
        h_ref[...] = jnp.zeros_like(h_ref)

    dirs = ((xf_ref, xtf_ref, bf_ref, cf_ref, dcf_ref, drf_ref, yf_ref),
            (xb_ref, xtb_ref, bb_ref, cb_ref, dcb_ref, drb_ref, yb_ref))
    masks = (_chunk_masks(chunk, reverse=False)[0], _chunk_masks(chunk, reverse=True)[0])

    low_half = lax.broadcasted_iota(jnp.int32, (1, 2 * p), 1) < p
    pairs = [slice(k * 2 * p, (k + 1) * 2 * p) for k in range(SSD_HPG // 2)]

    def front(g, d):
        bm, cm = dirs[d][2][0, g], dirs[d][3][0, g]
        return dict(g=g, d=d, bm=bm, cm=cm, cbm=_dot_nt(cm, bm))

    def middle(st):
        g, d = st["g"], st["d"]
        _, xt_ref, _, _, dc_ref, dr_ref, _ = dirs[d]
        base = d * (LANES // 2) + g * 2 * SSD_HPG
        dcol, xt = dc_ref[0], xt_ref[0, g]
        end = 0 if d == 1 else chunk - 1
        mm, ecol, xw, dec = [], [], [], []
        for r in range(SSD_HPG):
            dt_row = dr_ref[0, base + r:base + r + 1, :]
            cs_row = dr_ref[0, base + SSD_HPG + r:base + SSD_HPG + r + 1, :]
            cs_col = _lane_pick(dcol, base + SSD_HPG + r)
            mmat = st["cbm"] * jnp.exp2(jnp.where(masks[d], cs_col - cs_row, -jnp.inf)) * dt_row
            mm.append(mmat.astype(BF16))
            ecol.append(jnp.exp2(cs_col))
            cs_end = cs_row[:, end:end + 1]
            xw.append((xt[r * p:(r + 1) * p, :].astype(F32) * (jnp.exp2(cs_end - cs_row) * dt_row)).astype(BF16))
            dec.append(jnp.exp2(cs_end))
        st.update(mm=mm, ecol=ecol, xw=jnp.concatenate(xw, axis=0), dec=dec)

    def back(st):
        g, d = st["g"], st["d"]
        x_ref, y_ref = dirs[d][0], dirs[d][6]
        for k, pair in enumerate(pairs):
            ch = _dot_nt(st["cm"], h_ref[d, g, pair, :].astype(BF16))
            xp = x_ref[0, g, :, pair]
            y0 = _dot(st["mm"][2 * k], xp) + st["ecol"][2 * k] * ch
            y1 = _dot(st["mm"][2 * k + 1], xp) + st["ecol"][2 * k + 1] * ch
            y_ref[0, g, :, pair] = jnp.where(low_half, y0, y1).astype(y_ref.dtype)
        upd = _dot(st["xw"], st["bm"])
        for r in range(SSD_HPG):
            hs = slice(r * p, (r + 1) * p)
            h_ref[d, g, hs, :] = st["dec"][r] * h_ref[d, g, hs, :] + upd[hs, :]

    prev = None
    for g in range(groups):
        for d in range(2):
            cur = front(g, d)
            if prev is not None:
                back(prev)
            middle(cur)
            prev = cur
    back(prev)


def _ssd_scan(xs_in, xs_t, bm, cm, dcol, drow, *, chunk, ctx_len):
    batch, groups, s_len, gw = xs_in.shape
    state = bm.shape[3]
    n_chunks = s_len // chunk
    fwd, bwd = _scan_chunk_maps(n_chunks, ctx_len // chunk)

    def specs(order):
        return [pl.BlockSpec((1, groups, chunk, gw), lambda b, i: (b, 0, order(i), 0)),
                pl.BlockSpec((1, groups, gw, chunk), lambda b, i: (b, 0, 0, order(i))),
                pl.BlockSpec((1, groups, chunk, state), lambda b, i: (b, 0, order(i), 0)),
                pl.BlockSpec((1, groups, chunk, state), lambda b, i: (b, 0, order(i), 0)),
                pl.BlockSpec((1, chunk, LANES), lambda b, i: (b, order(i), 0)),
                pl.BlockSpec((1, LANES, chunk), lambda b, i: (b, 0, order(i)))]

    sf, sb = specs(fwd), specs(bwd)
    out = jax.ShapeDtypeStruct((batch, groups, s_len, gw), BF16)
    args = (xs_in, xs_t, bm, cm, dcol, drow)
    return pl.pallas_call(
        _ssd_scan_kernel,
        grid=(batch, n_chunks),
        in_specs=sf + sb,
        out_specs=[sf[0], sb[0]],
        out_shape=[out, out],
        scratch_shapes=[pltpu.VMEM((2, groups, gw, state), F32)],
        compiler_params=_params(("parallel", "arbitrary")),
        name="ssd_scan",
    )(*args, *args)


def _ssd_out_kernel(x_ref, mod_ref, yf_ref, yb_ref, xs_ref, z_ref, dsk_ref, ng_ref, wout_ref, g3_ref, o_ref, u_ref):
    groups, gw = yf_ref.shape[1], yf_ref.shape[3]
    ssq = None
    for g in range(groups):
        sl = slice(g * gw, (g + 1) * gw)
        y = yf_ref[0, g].astype(F32) + yb_ref[0, g].astype(F32) + dsk_ref[:, sl] * xs_ref[0, g].astype(F32)
        u = y * _silu(z_ref[0, :, sl].astype(F32))
        u_ref[:, sl] = u
        part = jnp.sum(u * u, axis=-1, keepdims=True)
        ssq = part if ssq is None else ssq + part
    inv = lax.rsqrt(ssq / (groups * gw) + EPS)
    out = None
    for g in range(groups):
        sl = slice(g * gw, (g + 1) * gw)
        part = _dot((u_ref[:, sl] * inv * ng_ref[:, sl]).astype(BF16), wout_ref[sl, :])
        out = part if out is None else out + part
    o_ref[0] = x_ref[0] + mod_ref[0, 5:6, :] * _rms(out, g3_ref[...])


def _ssd_out(xs, mod, yf, yb, xs_in, z, d_skip, norm_g, w_out, g3, *, tm, tile_off, n_tiles, ctx_tiles):
    batch, _, d_model = xs.shape
    groups, gw = yf.shape[1], yf.shape[3]
    inner = groups * gw
    dsk = jnp.repeat(d_skip, SSD_HEAD_DIM).reshape(1, inner)
    ng, g3r = norm_g.reshape(1, inner), g3.reshape(1, d_model)
    wout = w_out.astype(BF16)
    tok, out, mod_spec = _tile_specs(batch, tm, tile_off, ctx_tiles)
    grp = pl.BlockSpec((1, groups, tm, gw), lambda b, j: (b, 0, j + tile_off, 0))
    return pl.pallas_call(
        _ssd_out_kernel,
        grid=(batch, n_tiles),
        in_specs=[tok(d_model), mod_spec(d_model), grp, grp, grp, tok(inner),
                  _resident(dsk), _resident(ng), _resident(wout), _resident(g3r)],
        out_specs=out(d_model),
        out_shape=jax.ShapeDtypeStruct((batch, n_tiles * tm, d_model), F32),
        scratch_shapes=[pltpu.VMEM((tm, inner), F32)],
        compiler_params=_params(("parallel", "parallel")),
        name="ssd_out",
    )(xs, mod, yf, yb, xs_in, z, dsk, ng, wout, g3r)


def _forward(x, c, ctx, c_ctx, ada_w, ada_b, norm_g, ffn_w_gate, ffn_w_up, ffn_w_down,
             mlstm_w_in, mlstm_conv_w, mlstm_conv_b, mlstm_w_q, mlstm_w_k, mlstm_w_v,
             mlstm_w_gates, mlstm_b_gates, mlstm_norm_g, mlstm_skip, mlstm_w_out,
             ssd_w_in, ssd_conv_w, ssd_conv_b, ssd_dt_bias, ssd_a_log, ssd_d,
             ssd_norm_g, ssd_w_out, *, tm, ml_chunk, ssd_chunk):
    batch, seq, d_model = x.shape
    ctx_len = ctx.shape[1]
    depth = ada_w.shape[0]
    assert ctx_len == tm and seq % tm == 0 and tm % GRID_W == 0 and tm % ml_chunk == 0 and tm % ssd_chunk == 0
    assert batch < MOD_ROWS
    xs = x
    ctx_tiles = ctx_len // tm
    c_all = jnp.zeros((MOD_ROWS, d_model), F32).at[:batch].set(c).at[batch].set(c_ctx)
    ffn_weights = _stack_bf16(ffn_w_gate), _stack_bf16(ffn_w_up), _stack_bf16(ffn_w_down)
    mods = _modulation(c_all, ada_w, ada_b).reshape(depth, MOD_ROWS, N_MOD, d_model)
    for i in range(depth):
        last = i == depth - 1
        mod = mods[i]
        g = norm_g[i]
        xs = _ffn(xs, mod, g[0], g[1], ffn_weights, 2 * i,
                  j0=0, tm=tm, ctx_tiles=ctx_tiles, ctx_src=ctx if i == 0 else None)
        n_tiles = xs.shape[1] // tm
        out_off = ctx_tiles if last else 0
        out_kw = dict(tm=tm, tile_off=out_off, n_tiles=n_tiles - out_off, ctx_tiles=ctx_tiles)
        j = i // 2
        if i % 2 == 0:
            q, k, v, xc, z, gates = _mlstm_in(xs, mod, g[2], mlstm_w_in[j], mlstm_conv_w[j], mlstm_conv_b[j],
                                              mlstm_w_q[j], mlstm_w_k[j], mlstm_w_v[j], mlstm_w_gates[j],
                                              mlstm_b_gates[j], tm=tm, ctx_tiles=ctx_tiles)
            hf, hb = _mlstm_scan(q, k, v, gates, chunk=ml_chunk, ctx_len=ctx_len)
            xs = _mlstm_out(xs, mod, hf, hb, xc, z, mlstm_norm_g[j], mlstm_skip[j], mlstm_w_out[j], g[3], **out_kw)
        else:
            z, xs_in, xs_t, bm, cm, dcol, drow = _ssd_in(xs, mod, g[2], ssd_w_in[j], ssd_conv_w[j], ssd_conv_b[j],
                                                         ssd_dt_bias[j], ssd_a_log[j], tm=tm, chunk=ssd_chunk,
                                                         ctx_tiles=ctx_tiles)
            yf, yb = _ssd_scan(xs_in, xs_t, bm, cm, dcol, drow, chunk=ssd_chunk, ctx_len=ctx_len)
            xs = _ssd_out(xs, mod, yf, yb, xs_in, z, ssd_d[j], ssd_norm_g[j], ssd_w_out[j], g[3], **out_kw)
        if last:
            ctx_tiles = 0
        xs = _ffn(xs, mod, g[4], g[5], ffn_weights, 2 * i + 1, j0=6, tm=tm, ctx_tiles=ctx_tiles)
    return xs


def kernel(x, c, ctx, c_ctx, ada_w, ada_b, norm_g, ffn_w_gate, ffn_w_up, ffn_w_down, mlstm_w_in, mlstm_conv_w, mlstm_conv_b, mlstm_w_q, mlstm_w_k, mlstm_w_v, mlstm_w_gates, mlstm_b_gates, mlstm_norm_g, mlstm_skip, mlstm_w_out, ssd_w_in, ssd_conv_w, ssd_conv_b, ssd_dt_bias, ssd_a_log, ssd_d, ssd_norm_g, ssd_w_out):
    return _forward(x, c, ctx, c_ctx, ada_w, ada_b, norm_g, ffn_w_gate, ffn_w_up, ffn_w_down,
                    mlstm_w_in, mlstm_conv_w, mlstm_conv_b, mlstm_w_q, mlstm_w_k, mlstm_w_v,
                    mlstm_w_gates, mlstm_b_gates, mlstm_norm_g, mlstm_skip, mlstm_w_out,
                    ssd_w_in, ssd_conv_w, ssd_conv_b, ssd_dt_bias, ssd_a_log, ssd_d,
                    ssd_norm_g, ssd_w_out, tm=TOKEN_TILE, ml_chunk=MLSTM_CHUNK, ssd_chunk=SSD_CHUNK)
```

```python
import functools

import jax
import jax.numpy as jnp
from jax import lax
from jax.experimental import pallas as pl
from jax.experimental.pallas import tpu as pltpu

F32 = jnp.float32
BF16 = jnp.bfloat16

EPS = 1e-6
LOG2_E = 1.4426950408889634
GRID_W = 64
CONV_W = 5
N_MOD = 9
ML_HEADS = 4
ML_BLOCK = 4
SSD_GROUPS = 8
SSD_HPG = 4
SSD_HEAD_DIM = 64
SSD_STATE = 128

LANES = 128
SUBLANES = 8
MXU_DIM = 256
TOKEN_TILE = 256
MLSTM_CHUNK = 256
SSD_CHUNK = 128
FFN_TILES = 2
FF_CHUNK = MXU_DIM
BD_CHUNK = MXU_DIM
MOD_ROWS = 16
VMEM_LIMIT_BYTES = 56 * 1024 * 1024


def _rms(x, g):
    return x * lax.rsqrt(jnp.mean(x * x, axis=-1, keepdims=True) + EPS) * g


def _silu(x):
    return x * jax.nn.sigmoid(x)


def _softplus(x):
    return jnp.maximum(x, 0.0) + jnp.log1p(jnp.exp(-jnp.abs(x)))


def _log_sigmoid(x):
    return -_softplus(-x)


def _dot(a, b):
    return jnp.dot(a, b, preferred_element_type=F32)


def _dot_nt(a, b):
    return lax.dot_general(a, b, (((1,), (1,)), ((), ())), preferred_element_type=F32)


def _dot_tn(a, b):
    return lax.dot_general(a, b, (((0,), (0,)), ((), ())), preferred_element_type=F32)


def _resident(arr):
    nd = arr.ndim
    return pl.BlockSpec(arr.shape, lambda *_: (0,) * nd, pipeline_mode=pl.Buffered(1))


def _params(sem):
    return pltpu.CompilerParams(dimension_semantics=sem, vmem_limit_bytes=VMEM_LIMIT_BYTES)


def _tile_specs(batch, tm, tile_off, ctx_tiles):
    def tok(width):
        return pl.BlockSpec((1, tm, width), lambda b, j: (b, j + tile_off, 0))

    def out(width):
        return pl.BlockSpec((1, tm, width), lambda b, j: (b, j, 0))

    def mod_spec(d_model):
        return pl.BlockSpec((1, N_MOD, d_model),
                            lambda b, j: (jnp.where(j + tile_off < ctx_tiles, batch, b), 0, 0))

    return tok, out, mod_spec


def _mod_kernel(c_ref, w_ref, b_ref, o_ref):
    sc = _silu(c_ref[...])
    o_ref[...] = jnp.dot(sc, w_ref[...], preferred_element_type=F32,
                         precision=lax.Precision.HIGHEST) + b_ref[...]


def _modulation(c_all, w, b):
    depth, d_model, n_out = w.shape
    rows = c_all.shape[0]
    return pl.pallas_call(
        _mod_kernel,
        grid=(depth, n_out // d_model),
        in_specs=[pl.BlockSpec(c_all.shape, lambda l, n: (0, 0)),
                  pl.BlockSpec((None, d_model, d_model), lambda l, n: (l, 0, n)),
                  pl.BlockSpec((None, 1, d_model), lambda l, n: (l, 0, n))],
        out_specs=pl.BlockSpec((None, rows, d_model), lambda l, n: (l, 0, n)),
        out_shape=jax.ShapeDtypeStruct((depth, rows, n_out), F32),
        compiler_params=_params(("arbitrary", "arbitrary")),
        name="modulation",
    )(c_all, w, b.reshape(depth, 1, n_out))


def _cast_kernel(w_ref, o_ref):
    o_ref[...] = w_ref[...].astype(o_ref.dtype)


def _stack_bf16(w):
    w = w.reshape((-1,) + w.shape[2:])
    n, rows, cols = w.shape
    parts = 4
    assert rows % (parts * 2 * SUBLANES) == 0
    spec = pl.BlockSpec((None, rows // parts, cols), lambda i, r: (i, r, 0))
    return pl.pallas_call(
        _cast_kernel,
        grid=(n, parts),
        in_specs=[spec],
        out_specs=spec,
        out_shape=jax.ShapeDtypeStruct(w.shape, BF16),
        compiler_params=_params(("parallel", "parallel")),
        name="cast_bf16",
    )(w)


def _ffn_kernel(*refs, j0, tiles_per_row, ctx_tiles, split_ctx):
    n_x = FFN_TILES * (2 if split_ctx else 1)
    x_refs, mod_refs = refs[:n_x], refs[n_x:n_x + FFN_TILES]
    gpre_ref, gpost_ref, wg_ref, wu_ref, wd_ref, o_ref, h_ref, acc_ref = refs[n_x + FFN_TILES:]
    tm = x_refs[0].shape[0]

    def tile_input(t):
        if not split_ctx:
            return x_refs[t][...]
        j = (pl.program_id(0) * FFN_TILES + t) % tiles_per_row
        return jnp.where(j < ctx_tiles, x_refs[FFN_TILES + t][...], x_refs[t][...])

    for t, mod_ref in enumerate(mod_refs):
        rows = slice(t * tm, (t + 1) * tm)
        shift, scale = mod_ref[0, j0:j0 + 1, :], mod_ref[0, j0 + 1:j0 + 2, :]
        h_ref[rows, :] = (_rms(tile_input(t), gpre_ref[...]) * (1.0 + scale) + shift).astype(BF16)
    h = h_ref[...]
    for c in range(wg_ref.shape[1] // FF_CHUNK):
        sl = slice(c * FF_CHUNK, (c + 1) * FF_CHUNK)
        a = (_silu(_dot(h, wg_ref[:, sl])) * _dot(h, wu_ref[:, sl])).astype(BF16)
        contrib = _dot(a, wd_ref[sl, :])
        if c == 0:
            acc_ref[...] = contrib
        else:
            acc_ref[...] += contrib
    for t, mod_ref in enumerate(mod_refs):
        rows = slice(t * tm, (t + 1) * tm)
        gate = mod_ref[0, j0 + 2:j0 + 3, :]
        o_ref[rows, :] = tile_input(t) + 0.5 * gate * _rms(acc_ref[rows, :], gpost_ref[...])


def _ffn(xs, mod, g_pre, g_post, weights, which, *, j0, tm, ctx_tiles, ctx_src=None):
    batch, s_len, d_model = xs.shape
    wg, wu, wd = weights
    assert wg.shape[-1] % FF_CHUNK == 0

    def stacked(w):
        return pl.BlockSpec((None,) + w.shape[1:], lambda i: (which, 0, 0), pipeline_mode=pl.Buffered(1))

    gpre, gpost = g_pre.reshape(1, d_model), g_post.reshape(1, d_model)
    split_ctx = ctx_src is not None
    lat_per_row = s_len // tm
    tiles_per_row = lat_per_row + (ctx_tiles if split_ctx else 0)
    n_steps = batch * tiles_per_row // FFN_TILES
    assert n_steps * FFN_TILES == batch * tiles_per_row
    rows = FFN_TILES * tm

    def split(i, t):
        tile = i * FFN_TILES + t
        return tile // tiles_per_row, tile % tiles_per_row

    def mod_spec(t):
        def index(i):
            b, j = split(i, t)
            return jnp.where(j < ctx_tiles, batch, b), 0, 0
        return pl.BlockSpec((1, N_MOD, d_model), index)

    def x_spec(t):
        if not split_ctx:
            return pl.BlockSpec((tm, d_model), lambda i: (i * FFN_TILES + t, 0))

        def index(i):
            b, j = split(i, t)
            return b * lat_per_row + jnp.maximum(j - ctx_tiles, 0), 0
        return pl.BlockSpec((tm, d_model), index)

    def ctx_spec(t):
        def index(i):
            b, j = split(i, t)
            return b * ctx_tiles + jnp.minimum(j, ctx_tiles - 1), 0
        return pl.BlockSpec((tm, d_model), index)

    x_specs = [x_spec(t) for t in range(FFN_TILES)]
    x_args = [xs.reshape(batch * s_len, d_model)] * FFN_TILES
    if split_ctx:
        x_specs += [ctx_spec(t) for t in range(FFN_TILES)]
        x_args += [ctx_src.reshape(-1, d_model)] * FFN_TILES
    out = pl.pallas_call(
        functools.partial(_ffn_kernel, j0=j0, tiles_per_row=tiles_per_row, ctx_tiles=ctx_tiles,
                          split_ctx=split_ctx),
        grid=(n_steps,),
        in_specs=x_specs + [mod_spec(t) for t in range(FFN_TILES)]
                 + [_resident(gpre), _resident(gpost), stacked(wg), stacked(wu), stacked(wd)],
        out_specs=pl.BlockSpec((rows, d_model), lambda i: (i, 0)),
        out_shape=jax.ShapeDtypeStruct((batch * tiles_per_row * tm, d_model), F32),
        scratch_shapes=[pltpu.VMEM((rows, d_model), BF16), pltpu.VMEM((rows, d_model), F32)],
        compiler_params=_params(("parallel",)),
        name="ffn",
    )(*x_args, *([mod] * FFN_TILES), gpre, gpost, wg, wu, wd)
    return out.reshape(batch, tiles_per_row * tm, d_model)


def _short_conv(u, cw_ref, cb_ref, sl, is_ctx):
    tm, width = u.shape
    n_grp, per_row = tm // SUBLANES, GRID_W // SUBLANES
    pad = CONV_W // 2
    u3 = u.reshape(n_grp, SUBLANES, width)
    sub = lax.broadcasted_iota(jnp.int32, (1, SUBLANES, 1), 1)
    zero = jnp.zeros((1, SUBLANES, width), F32)
    out = cb_ref[:, sl] + u3 * cw_ref[pad:pad + 1, sl]
    for j in range(CONV_W):
        d = j - pad
        if d == 0:
            continue
        rot = pltpu.roll(u3, shift=(-d) % SUBLANES, axis=1)
        step = 1 if d > 0 else -1
        parts, run = [], []
        for g in range(n_grp):
            src = g + step
            edge = (src % per_row == 0) if d > 0 else (g % per_row == 0)
            if not edge:
                run.append(src)
                continue
            if run:
                parts.append(rot[run[0]:run[-1] + 1])
                run = []
            inside = 0 <= src < n_grp
            parts.append(jnp.where(is_ctx, rot[src:src + 1], zero) if inside else zero)
        if run:
            parts.append(rot[run[0]:run[-1] + 1])
        neighbour = jnp.concatenate(parts, axis=0)
        from_neighbour = (sub >= SUBLANES - d) if d > 0 else (sub < -d)
        out = out + jnp.where(from_neighbour, neighbour, rot) * cw_ref[j:j + 1, sl]
    return out.reshape(tm, width)


def _block_diag(w):
    per = BD_CHUNK // ML_BLOCK
    n_tiles = w.shape[0] // per
    w = w.reshape(n_tiles, per, ML_BLOCK, ML_BLOCK)
    eye = jnp.eye(per, dtype=w.dtype)
    return jnp.einsum("cnij,nm->cnimj", w, eye).reshape(n_tiles, BD_CHUNK, BD_CHUNK).astype(BF16)


def _mlstm_in_kernel(x_ref, mod_ref, g_ref, win_ref, cw_ref, cb_ref, bdq_ref, bdk_ref, bdv_ref,
                     wgq_ref, wgk_ref, wgv_ref, bg_ref,
                     q_ref, k_ref, v_ref, xc_ref, z_ref, gates_ref, *, tile_off, ctx_tiles, kscale):
    tm = x_ref.shape[1]
    inner = q_ref.shape[2]
    is_ctx = pl.program_id(1) + tile_off < ctx_tiles
    shift, scale = mod_ref[0, 3:4, :], mod_ref[0, 4:5, :]
    h = (_rms(x_ref[0], g_ref[...]) * (1.0 + scale) + shift).astype(BF16)
    gates = bg_ref[...]
    n_chunks = inner // BD_CHUNK
    cols = lambda c, base=0: slice(base + c * BD_CHUNK, base + (c + 1) * BD_CHUNK)
    def after_conv(c, xcb, vb, gates):
        sl = cols(c)
        q = _dot(xcb, bdq_ref[c])
        k = _dot(xcb, bdk_ref[c])
        qb, kb = q.astype(BF16), k.astype(BF16)
        q_ref[0, :, sl] = qb
        k_ref[0, :, sl] = (k * kscale).astype(BF16)
        return gates + _dot(qb, wgq_ref[sl, :]) + _dot(kb, wgk_ref[sl, :]) + _dot(vb, wgv_ref[sl, :])

    xm_next = _dot(h, win_ref[:, cols(0)])
    pending = None
    for c in range(n_chunks):
        sl = cols(c)
        xm = xm_next
        z_ref[0, :, sl] = _silu(_dot(h, win_ref[:, cols(c, inner)])).astype(BF16)
        vb = _dot(xm.astype(BF16), bdv_ref[c]).astype(BF16)
        v_ref[0, :, sl] = vb
        if c + 1 < n_chunks:
            xm_next = _dot(h, win_ref[:, cols(c + 1)])
        if pending is not None:
            gates = after_conv(*pending, gates)
        xcb = _silu(_short_conv(xm, cw_ref, cb_ref, sl, is_ctx)).astype(BF16)
        xc_ref[0, :, sl] = xcb
        pending = (c, xcb, vb)
    gates_ref[0] = after_conv(*pending, gates)


def _chunk_sum_matrices(tm, chunk):
    t_idx = lax.broadcasted_iota(jnp.int32, (tm, tm), 0)
    u_idx = lax.broadcasted_iota(jnp.int32, (tm, tm), 1)
    same = (t_idx // chunk) == (u_idx // chunk)
    return (jnp.logical_and(same, u_idx <= t_idx).astype(BF16), jnp.logical_and(same, u_idx >= t_idx).astype(BF16))


def _mlstm_in(xs, mod, g, w_in, conv_w, conv_b, w_q, w_k, w_v, w_gates, b_gates, *, tm, ctx_tiles):
    batch, s_len, d_model = xs.shape
    inner = w_in.shape[1] // 2
    n_gates = w_gates.shape[1]
    win = w_in.astype(BF16)
    cb = conv_b.reshape(1, inner)
    bdq, bdk, bdv = _block_diag(w_q), _block_diag(w_k), _block_diag(w_v)
    wg = jnp.pad(w_gates, ((0, 0), (0, LANES - n_gates))).astype(BF16)
    wgq, wgk, wgv = wg[:inner], wg[inner:2 * inner], wg[2 * inner:]
    bg = jnp.pad(b_gates, (0, LANES - n_gates)).reshape(1, LANES)
    gg = g.reshape(1, d_model)
    tok, out, mod_spec = _tile_specs(batch, tm, 0, ctx_tiles)
    act = jax.ShapeDtypeStruct((batch, s_len, inner), BF16)
    kernel = functools.partial(_mlstm_in_kernel, tile_off=0, ctx_tiles=ctx_tiles,
                               kscale=(inner // ML_HEADS) ** -0.5)
    return pl.pallas_call(
        kernel,
        grid=(batch, s_len // tm),
        in_specs=[tok(d_model), mod_spec(d_model), _resident(gg), _resident(win), _resident(conv_w),
                  _resident(cb), _resident(bdq), _resident(bdk), _resident(bdv),
                  _resident(wgq), _resident(wgk), _resident(wgv), _resident(bg)],
        out_specs=[out(inner)] * 5 + [out(LANES)],
        out_shape=[act] * 5 + [jax.ShapeDtypeStruct((batch, s_len, LANES), F32)],
        compiler_params=_params(("parallel", "parallel")),
        name="mlstm_in",
    )(xs, mod, gg, win, conv_w, cb, bdq, bdk, bdv, wgq, wgk, wgv, bg)


def _chunk_mask(chunk, reverse):
    t_idx = lax.broadcasted_iota(jnp.int32, (chunk, chunk), 0)
    s_idx = lax.broadcasted_iota(jnp.int32, (chunk, chunk), 1)
    return s_idx >= t_idx if reverse else s_idx <= t_idx


def _lane_pick(tile, idx):
    lane = lax.broadcasted_iota(jnp.int32, (1, tile.shape[1]), 1)
    return jnp.sum(jnp.where(lane == idx, tile, 0.0), axis=1, keepdims=True)


def _mlstm_scan_kernel(qf_ref, kf_ref, vf_ref, gf_ref, qb_ref, kb_ref, vb_ref, gb_ref, tril_ref, triu_ref,
                       hf_ref, hb_ref, c_ref, n_ref, m_ref, gcol_ref, grow_ref):
    chunk = qf_ref.shape[1]
    dh = c_ref.shape[2]

    @pl.when(pl.program_id(1) == 0)
    def _():
        c_ref[...] = jnp.zeros_like(c_ref)
        n_ref[...] = jnp.zeros_like(n_ref)
        m_ref[...] = jnp.zeros_like(m_ref)

    dirs = ((qf_ref, kf_ref, vf_ref, gf_ref, hf_ref), (qb_ref, kb_ref, vb_ref, gb_ref, hb_ref))
    masks = (_chunk_mask(chunk, reverse=False), _chunk_mask(chunk, reverse=True))
    lane = lax.broadcasted_iota(jnp.int32, (1, LANES), 1)
    for d, (g_ref, sum_ref) in enumerate(((gf_ref, tril_ref), (gb_ref, triu_ref))):
        gates = g_ref[0]
        cum = _dot3(sum_ref[...], _split3(_log_sigmoid(gates)))
        tile = jnp.where(jnp.bitwise_and(lane, ML_HEADS) == 0, gates, cum)
        gcol_ref[d] = tile
        grow_ref[d] = tile.T

    ones = jnp.ones((chunk, LANES), BF16)

    def front(hd, d):
        hs = slice(hd * dh, (hd + 1) * dh)
        q = dirs[d][0][0, :, hs]
        return dict(hd=hd, d=d, hs=hs, qk=_dot_nt(q, dirs[d][1][0, :, hs]),
                    qn=_dot(q, n_ref[d, hd].astype(BF16))[:, 0:1])

    def middle(st):
        hd, d, hs = st["hd"], st["d"], st["hs"]
        k_ref = dirs[d][1]
        tri = masks[d]
        i_lane = d * 2 * ML_HEADS + hd
        f_lane = i_lane + ML_HEADS
        ig_col, b_col = _lane_pick(gcol_ref[d], i_lane), _lane_pick(gcol_ref[d], f_lane)
        ig_row, b_row = grow_ref[d, i_lane:i_lane + 1, :], grow_ref[d, f_lane:f_lane + 1, :]
        end = 0 if d == 1 else chunk - 1
        b_end = b_row[:, end:end + 1]
        m_prev = m_ref[d, hd, 0:1, 0:1]
        dm = jnp.where(tri, b_col - b_row + ig_row, -jnp.inf)
        m_inter = b_col + m_prev
        m_t = jnp.maximum(m_inter, jnp.max(dm, axis=1, keepdims=True))
        s = st["qk"] * jnp.exp(dm - m_t)
        sc = jnp.exp(m_inter - m_t)
        den = jnp.sum(s, axis=1, keepdims=True) + sc * st["qn"]
        gl = b_end - b_col + ig_col
        m_new = jnp.maximum(b_end + m_prev, jnp.max(gl, axis=0, keepdims=True))
        kw = k_ref[0, :, hs].astype(F32) * jnp.exp(gl - m_new)
        dec = jnp.exp(b_end + m_prev - m_new)
        m_ref[d, hd] = jnp.broadcast_to(m_new, m_ref.shape[2:])
        st.update(s=s.astype(BF16), sc=sc, kw=kw.astype(BF16), dec=dec,
                  inv=1.0 / jnp.maximum(jnp.abs(den), jnp.exp(-m_t)))

    def back(st):
        hd, d, hs = st["hd"], st["d"], st["hs"]
        q_ref, _, v_ref, _, o_ref = dirs[d]
        v, cmat = v_ref[0, :, hs], c_ref[d, hd]
        num = _dot(st["s"], v) + st["sc"] * _dot(q_ref[0, :, hs], cmat.astype(BF16))
        o_ref[0, :, hs] = (num * st["inv"]).astype(o_ref.dtype)
        c_ref[d, hd] = st["dec"] * cmat + _dot_tn(st["kw"], v)
        n_ref[d, hd] = st["dec"] * n_ref[d, hd] + _dot_tn(st["kw"], ones)

    prev = None
    for hd in range(ML_HEADS):
        for d in range(2):
            cur = front(hd, d)
            if prev is not None:
                back(prev)
            middle(cur)
            prev = cur
    back(prev)


def _scan_chunk_maps(n_chunks, ctx_chunks):
    def fwd(i):
        return i

    def bwd(i):
        return jnp.where(i < ctx_chunks, ctx_chunks - 1 - i, n_chunks - 1 + ctx_chunks - i)

    return fwd, bwd


def _mlstm_scan(q, k, v, gates, *, chunk, ctx_len):
    batch, s_len, inner = q.shape
    dh = inner // ML_HEADS
    n_chunks = s_len // chunk
    fwd, bwd = _scan_chunk_maps(n_chunks, ctx_len // chunk)
    tril, triu = _chunk_sum_matrices(chunk, chunk)

    def specs(order):
        head_spec = pl.BlockSpec((1, chunk, inner), lambda b, i: (b, order(i), 0))
        gate_spec = pl.BlockSpec((1, chunk, LANES), lambda b, i: (b, order(i), 0))
        return head_spec, [head_spec, head_spec, head_spec, gate_spec]

    hf_spec, in_f = specs(fwd)
    hb_spec, in_b = specs(bwd)
    out = jax.ShapeDtypeStruct((batch, s_len, inner), BF16)
    return pl.pallas_call(
        _mlstm_scan_kernel,
        grid=(batch, n_chunks),
        in_specs=in_f + in_b + [_resident(tril), _resident(triu)],
        out_specs=[hf_spec, hb_spec],
        out_shape=[out, out],
        scratch_shapes=[pltpu.VMEM((2, ML_HEADS, dh, dh), F32), pltpu.VMEM((2, ML_HEADS, dh, LANES), F32),
                        pltpu.VMEM((2, ML_HEADS, SUBLANES, LANES), F32),
                        pltpu.VMEM((2, chunk, LANES), F32), pltpu.VMEM((2, LANES, chunk), F32)],
        compiler_params=_params(("parallel", "arbitrary")),
        name="mlstm_scan",
    )(q, k, v, gates, q, k, v, gates, tril, triu)


def _mlstm_out_kernel(x_ref, mod_ref, hf_ref, hb_ref, xc_ref, z_ref, ng_ref, sk_ref, wout_ref, g3_ref, o_ref):
    inner = hf_ref.shape[2]
    dh = inner // ML_HEADS
    y = None
    for hd in range(ML_HEADS):
        sl = slice(hd * dh, (hd + 1) * dh)
        hs = hf_ref[0, :, sl].astype(F32) + hb_ref[0, :, sl].astype(F32)
        cen = hs - jnp.mean(hs, axis=-1, keepdims=True)
        hn = cen * lax.rsqrt(jnp.mean(cen * cen, axis=-1, keepdims=True) + EPS)
        u = hn * ng_ref[:, sl] + sk_ref[:, sl] * xc_ref[0, :, sl].astype(F32)
        u = u * z_ref[0, :, sl].astype(F32)
        part = _dot(u.astype(BF16), wout_ref[sl, :])
        y = part if y is None else y + part
    o_ref[0] = x_ref[0] + mod_ref[0, 5:6, :] * _rms(y, g3_ref[...])


def _mlstm_out(xs, mod, hf, hb, xc, z, norm_g, skip, w_out, g3, *, tm, tile_off, n_tiles, ctx_tiles):
    batch, _, d_model = xs.shape
    inner = hf.shape[2]
    ng, sk, g3r = norm_g.reshape(1, inner), skip.reshape(1, inner), g3.reshape(1, d_model)
    wout = w_out.astype(BF16)
    tok, out, mod_spec = _tile_specs(batch, tm, tile_off, ctx_tiles)
    return pl.pallas_call(
        _mlstm_out_kernel,
        grid=(batch, n_tiles),
        in_specs=[tok(d_model), mod_spec(d_model), tok(inner), tok(inner), tok(inner), tok(inner),
                  _resident(ng), _resident(sk), _resident(wout), _resident(g3r)],
        out_specs=out(d_model),
        out_shape=jax.ShapeDtypeStruct((batch, n_tiles * tm, d_model), F32),
        compiler_params=_params(("parallel", "parallel")),
        name="mlstm_out",
    )(xs, mod, hf, hb, xc, z, ng, sk, wout, g3r)


def _split3(x):
    hi = x.astype(BF16)
    r1 = x - hi.astype(F32)
    mid = r1.astype(BF16)
    lo = (r1 - mid.astype(F32)).astype(BF16)
    return hi, mid, lo


def _dot3(a, pieces):
    return _dot(a, pieces[0]) + _dot(a, pieces[1]) + _dot(a, pieces[2])


def _ssd_lane_source():
    src = []
    for d in range(2):
        for g in range(SSD_GROUPS):
            for _ in range(2):
                for r in range(SSD_HPG):
                    src.append(d * SSD_GROUPS * SSD_HPG + g * SSD_HPG + r)
    return jnp.array(src, jnp.int32)


def _ssd_in_kernel(x_ref, mod_ref, g_ref, wz_ref, wx_ref, wdt_ref, cw_ref, cb_ref, dtb_ref, alog_ref,
                   tril_ref, triu_ref,
                   z_ref, xs_ref, xst_ref, bm_ref, cm_ref, dcol_ref, drow_ref, ua_ref, ub_ref, *, tile_off, ctx_tiles):
    tm = x_ref.shape[1]
    groups = xs_ref.shape[1]
    is_ctx = pl.program_id(1) + tile_off < ctx_tiles
    shift, scale = mod_ref[0, 3:4, :], mod_ref[0, 4:5, :]
    h = (_rms(x_ref[0], g_ref[...]) * (1.0 + scale) + shift).astype(BF16)
    half = BD_CHUNK // 2
    n_z, n_x = wz_ref.shape[1] // BD_CHUNK, wx_ref.shape[1] // BD_CHUNK
    cols = lambda c: slice(c * BD_CHUNK, (c + 1) * BD_CHUNK)
    dyn0 = jnp.minimum(pl.program_id(1), 0)
    stage = (ua_ref, ub_ref)
    stage[0][dyn0] = _dot(h, wx_ref[:, cols(0)])
    for c in range(n_x):
        sl = cols(c)
        for cz in range(c * n_z // n_x, (c + 1) * n_z // n_x):
            z_ref[0, :, cols(cz)] = _dot(h, wz_ref[:, cols(cz)]).astype(BF16)
        if c + 1 < n_x:
            stage[(c + 1) % 2][dyn0] = _dot(h, wx_ref[:, cols(c + 1)])
        xbc = _silu(_short_conv(stage[c % 2][dyn0], cw_ref, cb_ref, sl, is_ctx))
        if c < groups:
            xs_ref[0, c] = xbc.astype(BF16)
            xst_ref[0, c] = xbc.astype(BF16).T
        else:
            gi = 2 * (c - groups)
            ref, gi = (bm_ref, gi) if gi < groups else (cm_ref, gi - groups)
            ref[0, gi] = xbc[:, :half].astype(BF16)
            ref[0, gi + 1] = xbc[:, half:].astype(BF16)
    dt = _softplus(_dot(h, wdt_ref[...]) + dtb_ref[...])
    la = _split3(dt * (-LOG2_E * jnp.exp(alog_ref[...])))
    lane = lax.broadcasted_iota(jnp.int32, (1, LANES), 1)
    cs = jnp.where(lane < LANES // 2, _dot3(tril_ref[...], la), _dot3(triu_ref[...], la))
    tile = jnp.where(jnp.bitwise_and(lane, SSD_HPG) == 0, dt, cs)
    dcol_ref[0] = tile
    drow_ref[0] = tile.T


def _ssd_in(xs, mod, g, w_in, conv_w, conv_b, dt_bias, a_log, *, tm, chunk, ctx_tiles):
    batch, s_len, d_model = xs.shape
    groups, state = SSD_GROUPS, SSD_STATE
    gw = SSD_HPG * SSD_HEAD_DIM
    gn = groups * state
    conv_dim = conv_w.shape[1]
    inner = conv_dim - 2 * gn
    assert gw == BD_CHUNK and 2 * state == BD_CHUNK and inner == groups * gw
    assert 4 * dt_bias.size == 2 * LANES
    wz = w_in[:, :inner].astype(BF16)
    wx = w_in[:, inner:inner + conv_dim].astype(BF16)
    src = _ssd_lane_source()
    wdt = w_in[:, inner + conv_dim:][:, src].astype(BF16)
    dtb = dt_bias.reshape(-1)[src].reshape(1, LANES)
    alog = a_log.reshape(-1)[src].reshape(1, LANES)
    tril, triu = _chunk_sum_matrices(tm, chunk)
    cb = conv_b.reshape(1, conv_dim)
    gg = g.reshape(1, d_model)
    tok, out, mod_spec = _tile_specs(batch, tm, 0, ctx_tiles)
    kernel = functools.partial(_ssd_in_kernel, tile_off=0, ctx_tiles=ctx_tiles)
    grp = lambda w: pl.BlockSpec((1, groups, tm, w), lambda b, j: (b, 0, j, 0))
    return pl.pallas_call(
        kernel,
        grid=(batch, s_len // tm),
        in_specs=[tok(d_model), mod_spec(d_model), _resident(gg), _resident(wz), _resident(wx), _resident(wdt),
                  _resident(conv_w), _resident(cb), _resident(dtb), _resident(alog), _resident(tril), _resident(triu)],
        out_specs=[out(inner), grp(gw), pl.BlockSpec((1, groups, gw, tm), lambda b, j: (b, 0, 0, j)),
                   grp(state), grp(state), out(LANES), pl.BlockSpec((1, LANES, tm), lambda b, j: (b, 0, j))],
        out_shape=[jax.ShapeDtypeStruct((batch, s_len, inner), BF16),
                   jax.ShapeDtypeStruct((batch, groups, s_len, gw), BF16),
                   jax.ShapeDtypeStruct((batch, groups, gw, s_len), BF16),
                   jax.ShapeDtypeStruct((batch, groups, s_len, state), BF16),
                   jax.ShapeDtypeStruct((batch, groups, s_len, state), BF16),
                   jax.ShapeDtypeStruct((batch, s_len, LANES), F32),
                   jax.ShapeDtypeStruct((batch, LANES, s_len), F32)],
        scratch_shapes=[pltpu.VMEM((1, tm, BD_CHUNK), F32), pltpu.VMEM((1, tm, BD_CHUNK), F32)],
        compiler_params=_params(("parallel", "parallel")),
        name="ssd_in",
    )(xs, mod, gg, wz, wx, wdt, conv_w, cb, dtb, alog, tril, triu)


def _ssd_scan_kernel(xf_ref, xtf_ref, bf_ref, cf_ref, dcf_ref, drf_ref,
                     xb_ref, xtb_ref, bb_ref, cb_ref, dcb_ref, drb_ref,
                     yf_ref, yb_ref, h_ref):
    groups, chunk = xf_ref.shape[1], xf_ref.shape[2]
    p = SSD_HEAD_DIM

    @pl.when(pl.program_id(1) == 0)
    def _():
        h_ref[...] = jnp.zeros_like(h_ref)

    dirs = ((xf_ref, xtf_ref, bf_ref, cf_ref, dcf_ref, drf_ref, yf_ref),
            (xb_ref, xtb_ref, bb_ref, cb_ref, dcb_ref, drb_ref, yb_ref))
    masks = (_chunk_mask(chunk, reverse=False), _chunk_mask(chunk, reverse=True))

    low_half = lax.broadcasted_iota(jnp.int32, (1, 2 * p), 1) < p
    pairs = [slice(k * 2 * p, (k + 1) * 2 * p) for k in range(SSD_HPG // 2)]

    def front(g, d):
        bm, cm = dirs[d][2][0, g], dirs[d][3][0, g]
        return dict(g=g, d=d, bm=bm, cm=cm, cbm=_dot_nt(cm, bm))

    def middle(st):
        g, d = st["g"], st["d"]
        _, xt_ref, _, _, dc_ref, dr_ref, _ = dirs[d]
        base = d * (LANES // 2) + g * 2 * SSD_HPG
        dcol, xt = dc_ref[0], xt_ref[0, g]
        end = 0 if d == 1 else chunk - 1
        mm, ecol, xw, dec = [], [], [], []
        for r in range(SSD_HPG):
            dt_row = dr_ref[0, base + r:base + r + 1, :]
            cs_row = dr_ref[0, base + SSD_HPG + r:base + SSD_HPG + r + 1, :]
            cs_col = _lane_pick(dcol, base + SSD_HPG + r)
            mmat = st["cbm"] * jnp.exp2(jnp.where(masks[d], cs_col - cs_row, -jnp.inf)) * dt_row
            mm.append(mmat.astype(BF16))
            ecol.append(jnp.exp2(cs_col))
            cs_end = cs_row[:, end:end + 1]
            xw.append((xt[r * p:(r + 1) * p, :].astype(F32) * (jnp.exp2(cs_end - cs_row) * dt_row)).astype(BF16))
            dec.append(jnp.exp2(cs_end))
        st.update(mm=mm, ecol=ecol, xw=jnp.concatenate(xw, axis=0), dec=dec)

    def back(st):
        g, d = st["g"], st["d"]
        x_ref, y_ref = dirs[d][0], dirs[d][6]
        for k, pair in enumerate(pairs):
            ch = _dot_nt(st["cm"], h_ref[d, g, pair, :].astype(BF16))
            xp = x_ref[0, g, :, pair]
            y0 = _dot(st["mm"][2 * k], xp) + st["ecol"][2 * k] * ch
            y1 = _dot(st["mm"][2 * k + 1], xp) + st["ecol"][2 * k + 1] * ch
            y_ref[0, g, :, pair] = jnp.where(low_half, y0, y1).astype(y_ref.dtype)
        upd = _dot(st["xw"], st["bm"])
        for r in range(SSD_HPG):
            hs = slice(r * p, (r + 1) * p)
            h_ref[d, g, hs, :] = st["dec"][r] * h_ref[d, g, hs, :] + upd[hs, :]

    prev = None
    for g in range(groups):
        for d in range(2):
            cur = front(g, d)
            if prev is not None:
                back(prev)
            middle(cur)
            prev = cur
    back(prev)


def _ssd_scan(xs_in, xs_t, bm, cm, dcol, drow, *, chunk, ctx_len):
    batch, groups, s_len, gw = xs_in.shape
    state = bm.shape[3]
    n_chunks = s_len // chunk
    fwd, bwd = _scan_chunk_maps(n_chunks, ctx_len // chunk)

    def specs(order):
        return [pl.BlockSpec((1, groups, chunk, gw), lambda b, i: (b, 0, order(i), 0)),
                pl.BlockSpec((1, groups, gw, chunk), lambda b, i: (b, 0, 0, order(i))),
                pl.BlockSpec((1, groups, chunk, state), lambda b, i: (b, 0, order(i), 0)),
                pl.BlockSpec((1, groups, chunk, state), lambda b, i: (b, 0, order(i), 0)),
                pl.BlockSpec((1, chunk, LANES), lambda b, i: (b, order(i), 0)),
                pl.BlockSpec((1, LANES, chunk), lambda b, i: (b, 0, order(i)))]

    sf, sb = specs(fwd), specs(bwd)
    out = jax.ShapeDtypeStruct((batch, groups, s_len, gw), BF16)
    args = (xs_in, xs_t, bm, cm, dcol, drow)
    return pl.pallas_call(
        _ssd_scan_kernel,
        grid=(batch, n_chunks),
        in_specs=sf + sb,
        out_specs=[sf[0], sb[0]],
        out_shape=[out, out],
        scratch_shapes=[pltpu.VMEM((2, groups, gw, state), F32)],
        compiler_params=_params(("parallel", "arbitrary")),
        name="ssd_scan",
    )(*args, *args)


def _ssd_out_kernel(x_ref, mod_ref, yf_ref, yb_ref, xs_ref, z_ref, dsk_ref, ng_ref, wout_ref, g3_ref, o_ref, u_ref):
    groups, gw = yf_ref.shape[1], yf_ref.shape[3]
    ssq = None
    for g in range(groups):
        sl = slice(g * gw, (g + 1) * gw)
        y = yf_ref[0, g].astype(F32) + yb_ref[0, g].astype(F32) + dsk_ref[:, sl] * xs_ref[0, g].astype(F32)
        u = y * _silu(z_ref[0, :, sl].astype(F32))
        u_ref[:, sl] = u
        part = jnp.sum(u * u, axis=-1, keepdims=True)
        ssq = part if ssq is None else ssq + part
    inv = lax.rsqrt(ssq / (groups * gw) + EPS)
    out = None
    for g in range(groups):
        sl = slice(g * gw, (g + 1) * gw)
        part = _dot((u_ref[:, sl] * inv * ng_ref[:, sl]).astype(BF16), wout_ref[sl, :])
        out = part if out is None else out + part
    o_ref[0] = x_ref[0] + mod_ref[0, 5:6, :] * _rms(out, g3_ref[...])


def _ssd_out(xs, mod, yf, yb, xs_in, z, d_skip, norm_g, w_out, g3, *, tm, tile_off, n_tiles, ctx_tiles):
    batch, _, d_model = xs.shape
    groups, gw = yf.shape[1], yf.shape[3]
    inner = groups * gw
    dsk = jnp.repeat(d_skip, SSD_HEAD_DIM).reshape(1, inner)
    ng, g3r = norm_g.reshape(1, inner), g3.reshape(1, d_model)
    wout = w_out.astype(BF16)
    tok, out, mod_spec = _tile_specs(batch, tm, tile_off, ctx_tiles)
    grp = pl.BlockSpec((1, groups, tm, gw), lambda b, j: (b, 0, j + tile_off, 0))
    return pl.pallas_call(
        _ssd_out_kernel,
        grid=(batch, n_tiles),
        in_specs=[tok(d_model), mod_spec(d_model), grp, grp, grp, tok(inner),
                  _resident(dsk), _resident(ng), _resident(wout), _resident(g3r)],
        out_specs=out(d_model),
        out_shape=jax.ShapeDtypeStruct((batch, n_tiles * tm, d_model), F32),
        scratch_shapes=[pltpu.VMEM((tm, inner), F32)],
        compiler_params=_params(("parallel", "parallel")),
        name="ssd_out",
    )(xs, mod, yf, yb, xs_in, z, dsk, ng, wout, g3r)


def _forward(x, c, ctx, c_ctx, ada_w, ada_b, norm_g, ffn_w_gate, ffn_w_up, ffn_w_down,
             mlstm_w_in, mlstm_conv_w, mlstm_conv_b, mlstm_w_q, mlstm_w_k, mlstm_w_v,
             mlstm_w_gates, mlstm_b_gates, mlstm_norm_g, mlstm_skip, mlstm_w_out,
             ssd_w_in, ssd_conv_w, ssd_conv_b, ssd_dt_bias, ssd_a_log, ssd_d,
             ssd_norm_g, ssd_w_out, *, tm, ml_chunk, ssd_chunk):
    batch, seq, d_model = x.shape
    ctx_len = ctx.shape[1]
    depth = ada_w.shape[0]
    assert ctx_len == tm and seq % tm == 0 and tm % GRID_W == 0 and tm % ml_chunk == 0 and tm % ssd_chunk == 0
    assert batch < MOD_ROWS
    xs = x
    ctx_tiles = ctx_len // tm
    c_all = jnp.zeros((MOD_ROWS, d_model), F32).at[:batch].set(c).at[batch].set(c_ctx)
    ffn_weights = _stack_bf16(ffn_w_gate), _stack_bf16(ffn_w_up), _stack_bf16(ffn_w_down)
    mods = _modulation(c_all, ada_w, ada_b).reshape(depth, MOD_ROWS, N_MOD, d_model)
    for i in range(depth):
        last = i == depth - 1
        mod = mods[i]
        g = norm_g[i]
        xs = _ffn(xs, mod, g[0], g[1], ffn_weights, 2 * i,
                  j0=0, tm=tm, ctx_tiles=ctx_tiles, ctx_src=ctx if i == 0 else None)
        n_tiles = xs.shape[1] // tm
        out_off = ctx_tiles if last else 0
        out_kw = dict(tm=tm, tile_off=out_off, n_tiles=n_tiles - out_off, ctx_tiles=ctx_tiles)
        j = i // 2
        if i % 2 == 0:
            q, k, v, xc, z, gates = _mlstm_in(xs, mod, g[2], mlstm_w_in[j], mlstm_conv_w[j], mlstm_conv_b[j],
                                              mlstm_w_q[j], mlstm_w_k[j], mlstm_w_v[j], mlstm_w_gates[j],
                                              mlstm_b_gates[j], tm=tm, ctx_tiles=ctx_tiles)
            hf, hb = _mlstm_scan(q, k, v, gates, chunk=ml_chunk, ctx_len=ctx_len)
            xs = _mlstm_out(xs, mod, hf, hb, xc, z, mlstm_norm_g[j], mlstm_skip[j], mlstm_w_out[j], g[3], **out_kw)
        else:
            z, xs_in, xs_t, bm, cm, dcol, drow = _ssd_in(xs, mod, g[2], ssd_w_in[j], ssd_conv_w[j], ssd_conv_b[j],
                                                         ssd_dt_bias[j], ssd_a_log[j], tm=tm, chunk=ssd_chunk,
                                                         ctx_tiles=ctx_tiles)
            yf, yb = _ssd_scan(xs_in, xs_t, bm, cm, dcol, drow, chunk=ssd_chunk, ctx_len=ctx_len)
            xs = _ssd_out(xs, mod, yf, yb, xs_in, z, ssd_d[j], ssd_norm_g[j], ssd_w_out[j], g[3], **out_kw)
        if last:
            ctx_tiles = 0
        xs = _ffn(xs, mod, g[4], g[5], ffn_weights, 2 * i + 1, j0=6, tm=tm, ctx_tiles=ctx_tiles)
    return xs


def kernel(x, c, ctx, c_ctx, ada_w, ada_b, norm_g, ffn_w_gate, ffn_w_up, ffn_w_down, mlstm_w_in, mlstm_conv_w, mlstm_conv_b, mlstm_w_q, mlstm_w_k, mlstm_w_v, mlstm_w_gates, mlstm_b_gates, mlstm_norm_g, mlstm_skip, mlstm_w_out, ssd_w_in, ssd_conv_w, ssd_conv_b, ssd_dt_bias, ssd_a_log, ssd_d, ssd_norm_g, ssd_w_out):
    return _forward(x, c, ctx, c_ctx, ada_w, ada_b, norm_g, ffn_w_gate, ffn_w_up, ffn_w_down,
                    mlstm_w_in, mlstm_conv_w, mlstm_conv_b, mlstm_w_q, mlstm_w_k, mlstm_w_v,
                    mlstm_w_gates, mlstm_b_gates, mlstm_norm_g, mlstm_skip, mlstm_w_out,
                    ssd_w_in, ssd_conv_w, ssd_conv_b, ssd_dt_bias, ssd_a_log, ssd_d,
                    ssd_norm_g, ssd_w_out, tm=TOKEN_TILE, ml_chunk=MLSTM_CHUNK, ssd_chunk=SSD_CHUNK)
```

```python
import functools

import jax
import jax.numpy as jnp
from jax import lax
from jax.experimental import pallas as pl
from jax.experimental.pallas import tpu as pltpu

F32 = jnp.float32
BF16 = jnp.bfloat16

EPS = 1e-6
LOG2_E = 1.4426950408889634
GRID_W = 64
CONV_W = 5
N_MOD = 9
ML_HEADS = 4
ML_BLOCK = 4
SSD_GROUPS = 8
SSD_HPG = 4
SSD_HEAD_DIM = 64
SSD_STATE = 128

LANES = 128
SUBLANES = 8
MXU_DIM = 256
TOKEN_TILE = 256
MLSTM_CHUNK = 256
SSD_CHUNK = 128
FFN_TILES = 2
OUT_TILES = 2
FF_CHUNK = MXU_DIM
BD_CHUNK = MXU_DIM
MOD_ROWS = 16
VMEM_LIMIT_BYTES = 56 * 1024 * 1024


def _rms(x, g):
    return x * lax.rsqrt(jnp.mean(x * x, axis=-1, keepdims=True) + EPS) * g


def _silu(x):
    return x * jax.nn.sigmoid(x)


def _softplus(x):
    return jnp.maximum(x, 0.0) + jnp.log1p(jnp.exp(-jnp.abs(x)))


def _log_sigmoid(x):
    return -_softplus(-x)


def _dot(a, b):
    return jnp.dot(a, b, preferred_element_type=F32)


def _dot_nt(a, b):
    return lax.dot_general(a, b, (((1,), (1,)), ((), ())), preferred_element_type=F32)


def _dot_tn(a, b):
    return lax.dot_general(a, b, (((0,), (0,)), ((), ())), preferred_element_type=F32)


def _resident(arr):
    nd = arr.ndim
    return pl.BlockSpec(arr.shape, lambda *_: (0,) * nd, pipeline_mode=pl.Buffered(1))


def _params(sem):
    return pltpu.CompilerParams(dimension_semantics=sem, vmem_limit_bytes=VMEM_LIMIT_BYTES)


def _tile_specs(batch, tm, tile_off, ctx_tiles):
    def tok(width):
        return pl.BlockSpec((1, tm, width), lambda b, j: (b, j + tile_off, 0))

    def out(width):
        return pl.BlockSpec((1, tm, width), lambda b, j: (b, j, 0))

    def mod_spec(d_model):
        return pl.BlockSpec((1, N_MOD, d_model),
                            lambda b, j: (jnp.where(j + tile_off < ctx_tiles, batch, b), 0, 0))

    return tok, out, mod_spec


def _mod_kernel(c_ref, w_ref, b_ref, o_ref):
    sc = _silu(c_ref[...])
    o_ref[...] = jnp.dot(sc, w_ref[...], preferred_element_type=F32,
                         precision=lax.Precision.HIGHEST) + b_ref[...]


def _modulation(c_all, w, b):
    depth, d_model, n_out = w.shape
    rows = c_all.shape[0]
    return pl.pallas_call(
        _mod_kernel,
        grid=(depth, n_out // d_model),
        in_specs=[pl.BlockSpec(c_all.shape, lambda l, n: (0, 0)),
                  pl.BlockSpec((None, d_model, d_model), lambda l, n: (l, 0, n)),
                  pl.BlockSpec((None, 1, d_model), lambda l, n: (l, 0, n))],
        out_specs=pl.BlockSpec((None, rows, d_model), lambda l, n: (l, 0, n)),
        out_shape=jax.ShapeDtypeStruct((depth, rows, n_out), F32),
        compiler_params=_params(("arbitrary", "arbitrary")),
        name="modulation",
    )(c_all, w, b.reshape(depth, 1, n_out))


def _cast_kernel(w_ref, o_ref):
    o_ref[...] = w_ref[...].astype(o_ref.dtype)


def _stack_bf16(w):
    w = w.reshape((-1,) + w.shape[2:])
    n, rows, cols = w.shape
    parts = 4
    assert rows % (parts * 2 * SUBLANES) == 0
    spec = pl.BlockSpec((None, rows // parts, cols), lambda i, r: (i, r, 0))
    return pl.pallas_call(
        _cast_kernel,
        grid=(n, parts),
        in_specs=[spec],
        out_specs=spec,
        out_shape=jax.ShapeDtypeStruct(w.shape, BF16),
        compiler_params=_params(("parallel", "parallel")),
        name="cast_bf16",
    )(w)


def _ffn_kernel(*refs, j0, tiles_per_row, ctx_tiles, split_ctx):
    n_x = FFN_TILES * (2 if split_ctx else 1)
    x_refs, mod_refs = refs[:n_x], refs[n_x:n_x + FFN_TILES]
    gpre_ref, gpost_ref, wg_ref, wu_ref, wd_ref, o_ref, h_ref, acc_ref = refs[n_x + FFN_TILES:]
    tm = x_refs[0].shape[0]

    def tile_input(t):
        if not split_ctx:
            return x_refs[t][...]
        j = (pl.program_id(0) * FFN_TILES + t) % tiles_per_row
        return jnp.where(j < ctx_tiles, x_refs[FFN_TILES + t][...], x_refs[t][...])

    for t, mod_ref in enumerate(mod_refs):
        rows = slice(t * tm, (t + 1) * tm)
        shift, scale = mod_ref[0, j0:j0 + 1, :], mod_ref[0, j0 + 1:j0 + 2, :]
        h_ref[rows, :] = (_rms(tile_input(t), gpre_ref[...]) * (1.0 + scale) + shift).astype(BF16)
    h = h_ref[...]
    for c in range(wg_ref.shape[1] // FF_CHUNK):
        sl = slice(c * FF_CHUNK, (c + 1) * FF_CHUNK)
        a = (_silu(_dot(h, wg_ref[:, sl])) * _dot(h, wu_ref[:, sl])).astype(BF16)
        contrib = _dot(a, wd_ref[sl, :])
        if c == 0:
            acc_ref[...] = contrib
        else:
            acc_ref[...] += contrib
    for t, mod_ref in enumerate(mod_refs):
        rows = slice(t * tm, (t + 1) * tm)
        gate = mod_ref[0, j0 + 2:j0 + 3, :]
        o_ref[rows, :] = tile_input(t) + 0.5 * gate * _rms(acc_ref[rows, :], gpost_ref[...])


def _ffn(xs, mod, g_pre, g_post, weights, which, *, j0, tm, ctx_tiles, ctx_src=None):
    batch, s_len, d_model = xs.shape
    wg, wu, wd = weights
    assert wg.shape[-1] % FF_CHUNK == 0

    def stacked(w):
        return pl.BlockSpec((None,) + w.shape[1:], lambda i: (which, 0, 0), pipeline_mode=pl.Buffered(1))

    gpre, gpost = g_pre.reshape(1, d_model), g_post.reshape(1, d_model)
    split_ctx = ctx_src is not None
    lat_per_row = s_len // tm
    tiles_per_row = lat_per_row + (ctx_tiles if split_ctx else 0)
    n_steps = batch * tiles_per_row // FFN_TILES
    assert n_steps * FFN_TILES == batch * tiles_per_row
    rows = FFN_TILES * tm

    def split(i, t):
        tile = i * FFN_TILES + t
        return tile // tiles_per_row, tile % tiles_per_row

    def mod_spec(t):
        def index(i):
            b, j = split(i, t)
            return jnp.where(j < ctx_tiles, batch, b), 0, 0
        return pl.BlockSpec((1, N_MOD, d_model), index)

    def x_spec(t):
        if not split_ctx:
            return pl.BlockSpec((tm, d_model), lambda i: (i * FFN_TILES + t, 0))

        def index(i):
            b, j = split(i, t)
            return b * lat_per_row + jnp.maximum(j - ctx_tiles, 0), 0
        return pl.BlockSpec((tm, d_model), index)

    def ctx_spec(t):
        def index(i):
            b, j = split(i, t)
            return b * ctx_tiles + jnp.minimum(j, ctx_tiles - 1), 0
        return pl.BlockSpec((tm, d_model), index)

    x_specs = [x_spec(t) for t in range(FFN_TILES)]
    x_args = [xs.reshape(batch * s_len, d_model)] * FFN_TILES
    if split_ctx:
        x_specs += [ctx_spec(t) for t in range(FFN_TILES)]
        x_args += [ctx_src.reshape(-1, d_model)] * FFN_TILES
    out = pl.pallas_call(
        functools.partial(_ffn_kernel, j0=j0, tiles_per_row=tiles_per_row, ctx_tiles=ctx_tiles,
                          split_ctx=split_ctx),
        grid=(n_steps,),
        in_specs=x_specs + [mod_spec(t) for t in range(FFN_TILES)]
                 + [_resident(gpre), _resident(gpost), stacked(wg), stacked(wu), stacked(wd)],
        out_specs=pl.BlockSpec((rows, d_model), lambda i: (i, 0)),
        out_shape=jax.ShapeDtypeStruct((batch * tiles_per_row * tm, d_model), F32),
        scratch_shapes=[pltpu.VMEM((rows, d_model), BF16), pltpu.VMEM((rows, d_model), F32)],
        compiler_params=_params(("parallel",)),
        name="ffn",
    )(*x_args, *([mod] * FFN_TILES), gpre, gpost, wg, wu, wd)
    return out.reshape(batch, tiles_per_row * tm, d_model)


def _short_conv(u, cw_ref, cb_ref, sl, is_ctx):
    tm, width = u.shape
    n_grp, per_row = tm // SUBLANES, GRID_W // SUBLANES
    pad = CONV_W // 2
    u3 = u.reshape(n_grp, SUBLANES, width)
    sub = lax.broadcasted_iota(jnp.int32, (1, SUBLANES, 1), 1)
    zero = jnp.zeros((1, SUBLANES, width), F32)
    out = cb_ref[:, sl] + u3 * cw_ref[pad:pad + 1, sl]
    for j in range(CONV_W):
        d = j - pad
        if d == 0:
            continue
        rot = pltpu.roll(u3, shift=(-d) % SUBLANES, axis=1)
        step = 1 if d > 0 else -1
        parts, run = [], []
        for g in range(n_grp):
            src = g + step
            edge = (src % per_row == 0) if d > 0 else (g % per_row == 0)
            if not edge:
                run.append(src)
                continue
            if run:
                parts.append(rot[run[0]:run[-1] + 1])
                run = []
            inside = 0 <= src < n_grp
            parts.append(jnp.where(is_ctx, rot[src:src + 1], zero) if inside else zero)
        if run:
            parts.append(rot[run[0]:run[-1] + 1])
        neighbour = jnp.concatenate(parts, axis=0)
        from_neighbour = (sub >= SUBLANES - d) if d > 0 else (sub < -d)
        out = out + jnp.where(from_neighbour, neighbour, rot) * cw_ref[j:j + 1, sl]
    return out.reshape(tm, width)


def _block_diag(w):
    per = BD_CHUNK // ML_BLOCK
    n_tiles = w.shape[0] // per
    w = w.reshape(n_tiles, per, ML_BLOCK, ML_BLOCK)
    eye = jnp.eye(per, dtype=w.dtype)
    return jnp.einsum("cnij,nm->cnimj", w, eye).reshape(n_tiles, BD_CHUNK, BD_CHUNK).astype(BF16)


def _mlstm_in_kernel(x_ref, mod_ref, g_ref, win_ref, cw_ref, cb_ref, bdq_ref, bdk_ref, bdv_ref,
                     wgq_ref, wgk_ref, wgv_ref, bg_ref,
                     q_ref, k_ref, v_ref, xc_ref, z_ref, gates_ref, *, ctx_tiles, kscale):
    tm = x_ref.shape[1]
    inner = q_ref.shape[2]
    is_ctx = pl.program_id(1) < ctx_tiles
    shift, scale = mod_ref[0, 3:4, :], mod_ref[0, 4:5, :]
    h = (_rms(x_ref[0], g_ref[...]) * (1.0 + scale) + shift).astype(BF16)
    gates = bg_ref[...]
    n_chunks = inner // BD_CHUNK
    cols = lambda c, base=0: slice(base + c * BD_CHUNK, base + (c + 1) * BD_CHUNK)

    def after_conv(c, xcb, vb, gates):
        sl = cols(c)
        q = _dot(xcb, bdq_ref[c])
        k = _dot(xcb, bdk_ref[c])
        qb, kb = q.astype(BF16), k.astype(BF16)
        q_ref[0, :, sl] = qb
        k_ref[0, :, sl] = (k * kscale).astype(BF16)
        return gates + _dot(qb, wgq_ref[sl, :]) + _dot(kb, wgk_ref[sl, :]) + _dot(vb, wgv_ref[sl, :])

    xm_next = _dot(h, win_ref[:, cols(0)])
    pending = None
    for c in range(n_chunks):
        sl = cols(c)
        xm = xm_next
        z_ref[0, :, sl] = _silu(_dot(h, win_ref[:, cols(c, inner)])).astype(BF16)
        vb = _dot(xm.astype(BF16), bdv_ref[c]).astype(BF16)
        v_ref[0, :, sl] = vb
        if c + 1 < n_chunks:
            xm_next = _dot(h, win_ref[:, cols(c + 1)])
        if pending is not None:
            gates = after_conv(*pending, gates)
        xcb = _silu(_short_conv(xm, cw_ref, cb_ref, sl, is_ctx)).astype(BF16)
        xc_ref[0, :, sl] = xcb
        pending = (c, xcb, vb)
    gates_ref[0] = after_conv(*pending, gates)


def _chunk_sum_matrices(tm, chunk):
    t_idx = lax.broadcasted_iota(jnp.int32, (tm, tm), 0)
    u_idx = lax.broadcasted_iota(jnp.int32, (tm, tm), 1)
    same = (t_idx // chunk) == (u_idx // chunk)
    return (jnp.logical_and(same, u_idx <= t_idx).astype(BF16), jnp.logical_and(same, u_idx >= t_idx).astype(BF16))


def _mlstm_in(xs, mod, g, w_in, conv_w, conv_b, w_q, w_k, w_v, w_gates, b_gates, *, tm, ctx_tiles):
    batch, s_len, d_model = xs.shape
    inner = w_in.shape[1] // 2
    n_gates = w_gates.shape[1]
    win = w_in.astype(BF16)
    cb = conv_b.reshape(1, inner)
    bdq, bdk, bdv = _block_diag(w_q), _block_diag(w_k), _block_diag(w_v)
    wg = jnp.pad(w_gates, ((0, 0), (0, LANES - n_gates))).astype(BF16)
    wgq, wgk, wgv = wg[:inner], wg[inner:2 * inner], wg[2 * inner:]
    bg = jnp.pad(b_gates, (0, LANES - n_gates)).reshape(1, LANES)
    gg = g.reshape(1, d_model)
    tok, out, mod_spec = _tile_specs(batch, tm, 0, ctx_tiles)
    act = jax.ShapeDtypeStruct((batch, s_len, inner), BF16)
    kernel = functools.partial(_mlstm_in_kernel, ctx_tiles=ctx_tiles,
                               kscale=(inner // ML_HEADS) ** -0.5)
    return pl.pallas_call(
        kernel,
        grid=(batch, s_len // tm),
        in_specs=[tok(d_model), mod_spec(d_model), _resident(gg), _resident(win), _resident(conv_w),
                  _resident(cb), _resident(bdq), _resident(bdk), _resident(bdv),
                  _resident(wgq), _resident(wgk), _resident(wgv), _resident(bg)],
        out_specs=[out(inner)] * 5 + [out(LANES)],
        out_shape=[act] * 5 + [jax.ShapeDtypeStruct((batch, s_len, LANES), F32)],
        compiler_params=_params(("parallel", "parallel")),
        name="mlstm_in",
    )(xs, mod, gg, win, conv_w, cb, bdq, bdk, bdv, wgq, wgk, wgv, bg)


def _chunk_mask(chunk, reverse):
    t_idx = lax.broadcasted_iota(jnp.int32, (chunk, chunk), 0)
    s_idx = lax.broadcasted_iota(jnp.int32, (chunk, chunk), 1)
    return s_idx >= t_idx if reverse else s_idx <= t_idx


def _lane_pick(tile, idx):
    lane = lax.broadcasted_iota(jnp.int32, (1, tile.shape[1]), 1)
    return jnp.sum(jnp.where(lane == idx, tile, 0.0), axis=1, keepdims=True)


def _mlstm_scan_kernel(qf_ref, kf_ref, vf_ref, gf_ref, qb_ref, kb_ref, vb_ref, gb_ref, tril_ref, triu_ref,
                       hf_ref, hb_ref, c_ref, n_ref, m_ref, gcol_ref, grow_ref):
    chunk = qf_ref.shape[1]
    dh = c_ref.shape[2]

    @pl.when(pl.program_id(1) == 0)
    def _():
        c_ref[...] = jnp.zeros_like(c_ref)
        n_ref[...] = jnp.zeros_like(n_ref)
        m_ref[...] = jnp.zeros_like(m_ref)

    dirs = ((qf_ref, kf_ref, vf_ref, gf_ref, hf_ref), (qb_ref, kb_ref, vb_ref, gb_ref, hb_ref))
    masks = (_chunk_mask(chunk, reverse=False), _chunk_mask(chunk, reverse=True))
    lane = lax.broadcasted_iota(jnp.int32, (1, LANES), 1)
    for d, (g_ref, sum_ref) in enumerate(((gf_ref, tril_ref), (gb_ref, triu_ref))):
        gates = g_ref[0]
        cum = _dot3(sum_ref[...], _split3(_log_sigmoid(gates)))
        tile = jnp.where(jnp.bitwise_and(lane, ML_HEADS) == 0, gates, cum)
        gcol_ref[d] = tile
        grow_ref[d] = tile.T

    ones = jnp.ones((chunk, LANES), BF16)

    def front(hd, d):
        hs = slice(hd * dh, (hd + 1) * dh)
        q = dirs[d][0][0, :, hs]
        return dict(hd=hd, d=d, hs=hs, qk=_dot_nt(q, dirs[d][1][0, :, hs]),
                    qn=_dot(q, n_ref[d, hd].astype(BF16))[:, 0:1])

    def middle(st):
        hd, d, hs = st["hd"], st["d"], st["hs"]
        k_ref = dirs[d][1]
        tri = masks[d]
        i_lane = d * 2 * ML_HEADS + hd
        f_lane = i_lane + ML_HEADS
        ig_col, b_col = _lane_pick(gcol_ref[d], i_lane), _lane_pick(gcol_ref[d], f_lane)
        ig_row, b_row = grow_ref[d, i_lane:i_lane + 1, :], grow_ref[d, f_lane:f_lane + 1, :]
        end = 0 if d == 1 else chunk - 1
        b_end = b_row[:, end:end + 1]
        m_prev = m_ref[d, hd, 0:1, 0:1]
        dm = jnp.where(tri, b_col - b_row + ig_row, -jnp.inf)
        m_inter = b_col + m_prev
        m_t = jnp.maximum(m_inter, jnp.max(dm, axis=1, keepdims=True))
        s = st["qk"] * jnp.exp(dm - m_t)
        sc = jnp.exp(m_inter - m_t)
        den = jnp.sum(s, axis=1, keepdims=True) + sc * st["qn"]
        gl = b_end - b_col + ig_col
        m_new = jnp.maximum(b_end + m_prev, jnp.max(gl, axis=0, keepdims=True))
        kw = k_ref[0, :, hs].astype(F32) * jnp.exp(gl - m_new)
        dec = jnp.exp(b_end + m_prev - m_new)
        m_ref[d, hd] = jnp.broadcast_to(m_new, m_ref.shape[2:])
        st.update(s=s.astype(BF16), sc=sc, kw=kw.astype(BF16), dec=dec,
                  inv=1.0 / jnp.maximum(jnp.abs(den), jnp.exp(-m_t)))

    def back(st):
        hd, d, hs = st["hd"], st["d"], st["hs"]
        q_ref, _, v_ref, _, o_ref = dirs[d]
        v, cmat = v_ref[0, :, hs], c_ref[d, hd]
        num = _dot(st["s"], v) + st["sc"] * _dot(q_ref[0, :, hs], cmat.astype(BF16))
        o_ref[0, :, hs] = (num * st["inv"]).astype(o_ref.dtype)
        c_ref[d, hd] = st["dec"] * cmat + _dot_tn(st["kw"], v)
        n_ref[d, hd] = st["dec"] * n_ref[d, hd] + _dot_tn(st["kw"], ones)

    prev = None
    for hd in range(ML_HEADS):
        for d in range(2):
            cur = front(hd, d)
            if prev is not None:
                back(prev)
            middle(cur)
            prev = cur
    back(prev)


def _scan_chunk_maps(n_chunks, ctx_chunks):
    def fwd(i):
        return i

    def bwd(i):
        return jnp.where(i < ctx_chunks, ctx_chunks - 1 - i, n_chunks - 1 + ctx_chunks - i)

    return fwd, bwd


def _mlstm_scan(q, k, v, gates, *, chunk, ctx_len):
    batch, s_len, inner = q.shape
    dh = inner // ML_HEADS
    n_chunks = s_len // chunk
    fwd, bwd = _scan_chunk_maps(n_chunks, ctx_len // chunk)
    tril, triu = _chunk_sum_matrices(chunk, chunk)

    def specs(order):
        head_spec = pl.BlockSpec((1, chunk, inner), lambda b, i: (b, order(i), 0))
        gate_spec = pl.BlockSpec((1, chunk, LANES), lambda b, i: (b, order(i), 0))
        return head_spec, [head_spec, head_spec, head_spec, gate_spec]

    hf_spec, in_f = specs(fwd)
    hb_spec, in_b = specs(bwd)
    out = jax.ShapeDtypeStruct((batch, s_len, inner), BF16)
    return pl.pallas_call(
        _mlstm_scan_kernel,
        grid=(batch, n_chunks),
        in_specs=in_f + in_b + [_resident(tril), _resident(triu)],
        out_specs=[hf_spec, hb_spec],
        out_shape=[out, out],
        scratch_shapes=[pltpu.VMEM((2, ML_HEADS, dh, dh), F32), pltpu.VMEM((2, ML_HEADS, dh, LANES), F32),
                        pltpu.VMEM((2, ML_HEADS, SUBLANES, LANES), F32),
                        pltpu.VMEM((2, chunk, LANES), F32), pltpu.VMEM((2, LANES, chunk), F32)],
        compiler_params=_params(("parallel", "arbitrary")),
        name="mlstm_scan",
    )(q, k, v, gates, q, k, v, gates, tril, triu)


def _mlstm_out_kernel(*refs):
    nt = OUT_TILES
    x_refs, mod_refs, hf_refs, hb_refs, xc_refs, z_refs = (refs[k * nt:(k + 1) * nt] for k in range(6))
    ng_ref, sk_ref, wout_ref, g3_ref, o_ref, u_ref = refs[6 * nt:]
    tm, inner = hf_refs[0].shape[1], hf_refs[0].shape[2]
    dh = inner // ML_HEADS
    y = None
    for hd in range(ML_HEADS):
        sl = slice(hd * dh, (hd + 1) * dh)
        for t in range(nt):
            hs = hf_refs[t][0, :, sl].astype(F32) + hb_refs[t][0, :, sl].astype(F32)
            cen = hs - jnp.mean(hs, axis=-1, keepdims=True)
            hn = cen * lax.rsqrt(jnp.mean(cen * cen, axis=-1, keepdims=True) + EPS)
            u = hn * ng_ref[:, sl] + sk_ref[:, sl] * xc_refs[t][0, :, sl].astype(F32)
            u_ref[t * tm:(t + 1) * tm, sl] = (u * z_refs[t][0, :, sl].astype(F32)).astype(BF16)
        part = _dot(u_ref[:, sl], wout_ref[sl, :])
        y = part if y is None else y + part
    for t in range(nt):
        rows = slice(t * tm, (t + 1) * tm)
        o_ref[rows, :] = x_refs[t][0] + mod_refs[t][0, 5:6, :] * _rms(y[rows, :], g3_ref[...])


def _readout_specs(batch, tm, tile_off, n_tiles, ctx_tiles):
    assert (batch * n_tiles) % OUT_TILES == 0

    def where(i, t):
        tile = i * OUT_TILES + t
        return tile // n_tiles, tile % n_tiles + tile_off

    def per_tile(make):
        return [make(t) for t in range(OUT_TILES)]

    def tok(width):
        return per_tile(lambda t: pl.BlockSpec((1, tm, width), lambda i: where(i, t) + (0,)))

    def grouped(groups, width):
        return per_tile(lambda t: pl.BlockSpec((1, groups, tm, width),
                                               lambda i: (where(i, t)[0], 0, where(i, t)[1], 0)))

    def mod(d_model):
        return per_tile(lambda t: pl.BlockSpec(
            (1, N_MOD, d_model), lambda i: (jnp.where(where(i, t)[1] < ctx_tiles, batch, where(i, t)[0]), 0, 0)))

    return tok, grouped, mod, batch * n_tiles // OUT_TILES


def _mlstm_out(xs, mod, hf, hb, xc, z, norm_g, skip, w_out, g3, *, tm, tile_off, n_tiles, ctx_tiles):
    batch, _, d_model = xs.shape
    inner = hf.shape[2]
    ng, sk, g3r = norm_g.reshape(1, inner), skip.reshape(1, inner), g3.reshape(1, d_model)
    wout = w_out.astype(BF16)
    tok, _, mod_specs, n_steps = _readout_specs(batch, tm, tile_off, n_tiles, ctx_tiles)
    rows = OUT_TILES * tm
    rep = lambda a: [a] * OUT_TILES
    out = pl.pallas_call(
        _mlstm_out_kernel,
        grid=(n_steps,),
        in_specs=tok(d_model) + mod_specs(d_model) + tok(inner) + tok(inner) + tok(inner) + tok(inner)
                 + [_resident(ng), _resident(sk), _resident(wout), _resident(g3r)],
        out_specs=pl.BlockSpec((rows, d_model), lambda i: (i, 0)),
        out_shape=jax.ShapeDtypeStruct((batch * n_tiles * tm, d_model), F32),
        scratch_shapes=[pltpu.VMEM((rows, inner), BF16)],
        compiler_params=_params(("parallel",)),
        name="mlstm_out",
    )(*rep(xs), *rep(mod), *rep(hf), *rep(hb), *rep(xc), *rep(z), ng, sk, wout, g3r)
    return out.reshape(batch, n_tiles * tm, d_model)


def _split3(x):
    hi = x.astype(BF16)
    r1 = x - hi.astype(F32)
    mid = r1.astype(BF16)
    lo = (r1 - mid.astype(F32)).astype(BF16)
    return hi, mid, lo


def _dot3(a, pieces):
    return _dot(a, pieces[0]) + _dot(a, pieces[1]) + _dot(a, pieces[2])


def _ssd_lane_source():
    src = []
    for d in range(2):
        for g in range(SSD_GROUPS):
            for _ in range(2):
                for r in range(SSD_HPG):
                    src.append(d * SSD_GROUPS * SSD_HPG + g * SSD_HPG + r)
    return jnp.array(src, jnp.int32)


def _ssd_in_kernel(x_ref, mod_ref, g_ref, wz_ref, wx_ref, wdt_ref, cw_ref, cb_ref, dtb_ref, alog_ref,
                   tril_ref, triu_ref,
                   z_ref, xs_ref, xst_ref, bm_ref, cm_ref, dcol_ref, drow_ref, ua_ref, ub_ref, *, ctx_tiles):
    tm = x_ref.shape[1]
    groups = xs_ref.shape[1]
    is_ctx = pl.program_id(1) < ctx_tiles
    shift, scale = mod_ref[0, 3:4, :], mod_ref[0, 4:5, :]
    h = (_rms(x_ref[0], g_ref[...]) * (1.0 + scale) + shift).astype(BF16)
    half = BD_CHUNK // 2
    n_z, n_x = wz_ref.shape[1] // BD_CHUNK, wx_ref.shape[1] // BD_CHUNK
    cols = lambda c: slice(c * BD_CHUNK, (c + 1) * BD_CHUNK)
    dyn0 = jnp.minimum(pl.program_id(1), 0)
    stage = (ua_ref, ub_ref)
    stage[0][dyn0] = _dot(h, wx_ref[:, cols(0)])
    for c in range(n_x):
        sl = cols(c)
        for cz in range(c * n_z // n_x, (c + 1) * n_z // n_x):
            z_ref[0, :, cols(cz)] = _dot(h, wz_ref[:, cols(cz)]).astype(BF16)
        if c + 1 < n_x:
            stage[(c + 1) % 2][dyn0] = _dot(h, wx_ref[:, cols(c + 1)])
        xbc = _silu(_short_conv(stage[c % 2][dyn0], cw_ref, cb_ref, sl, is_ctx))
        if c < groups:
            xs_ref[0, c] = xbc.astype(BF16)
            xst_ref[0, c] = xbc.astype(BF16).T
        else:
            gi = 2 * (c - groups)
            ref, gi = (bm_ref, gi) if gi < groups else (cm_ref, gi - groups)
            ref[0, gi] = xbc[:, :half].astype(BF16)
            ref[0, gi + 1] = xbc[:, half:].astype(BF16)
    dt = _softplus(_dot(h, wdt_ref[...]) + dtb_ref[...])
    la = _split3(dt * (-LOG2_E * jnp.exp(alog_ref[...])))
    lane = lax.broadcasted_iota(jnp.int32, (1, LANES), 1)
    cs = jnp.where(lane < LANES // 2, _dot3(tril_ref[...], la), _dot3(triu_ref[...], la))
    tile = jnp.where(jnp.bitwise_and(lane, SSD_HPG) == 0, dt, cs)
    dcol_ref[0] = tile
    drow_ref[0] = tile.T


def _ssd_in(xs, mod, g, w_in, conv_w, conv_b, dt_bias, a_log, *, tm, chunk, ctx_tiles):
    batch, s_len, d_model = xs.shape
    groups, state = SSD_GROUPS, SSD_STATE
    gw = SSD_HPG * SSD_HEAD_DIM
    gn = groups * state
    conv_dim = conv_w.shape[1]
    inner = conv_dim - 2 * gn
    assert gw == BD_CHUNK and 2 * state == BD_CHUNK and inner == groups * gw
    assert 4 * dt_bias.size == 2 * LANES
    wz = w_in[:, :inner].astype(BF16)
    wx = w_in[:, inner:inner + conv_dim].astype(BF16)
    src = _ssd_lane_source()
    wdt = w_in[:, inner + conv_dim:][:, src].astype(BF16)
    dtb = dt_bias.reshape(-1)[src].reshape(1, LANES)
    alog = a_log.reshape(-1)[src].reshape(1, LANES)
    tril, triu = _chunk_sum_matrices(tm, chunk)
    cb = conv_b.reshape(1, conv_dim)
    gg = g.reshape(1, d_model)
    tok, out, mod_spec = _tile_specs(batch, tm, 0, ctx_tiles)
    kernel = functools.partial(_ssd_in_kernel, ctx_tiles=ctx_tiles)
    grp = lambda w: pl.BlockSpec((1, groups, tm, w), lambda b, j: (b, 0, j, 0))
    return pl.pallas_call(
        kernel,
        grid=(batch, s_len // tm),
        in_specs=[tok(d_model), mod_spec(d_model), _resident(gg), _resident(wz), _resident(wx), _resident(wdt),
                  _resident(conv_w), _resident(cb), _resident(dtb), _resident(alog), _resident(tril), _resident(triu)],
        out_specs=[out(inner), grp(gw), pl.BlockSpec((1, groups, gw, tm), lambda b, j: (b, 0, 0, j)),
                   grp(state), grp(state), out(LANES), pl.BlockSpec((1, LANES, tm), lambda b, j: (b, 0, j))],
        out_shape=[jax.ShapeDtypeStruct((batch, s_len, inner), BF16),
                   jax.ShapeDtypeStruct((batch, groups, s_len, gw), BF16),
                   jax.ShapeDtypeStruct((batch, groups, gw, s_len), BF16),
                   jax.ShapeDtypeStruct((batch, groups, s_len, state), BF16),
                   jax.ShapeDtypeStruct((batch, groups, s_len, state), BF16),
                   jax.ShapeDtypeStruct((batch, s_len, LANES), F32),
                   jax.ShapeDtypeStruct((batch, LANES, s_len), F32)],
        scratch_shapes=[pltpu.VMEM((1, tm, BD_CHUNK), F32), pltpu.VMEM((1, tm, BD_CHUNK), F32)],
        compiler_params=_params(("parallel", "parallel")),
        name="ssd_in",
    )(xs, mod, gg, wz, wx, wdt, conv_w, cb, dtb, alog, tril, triu)


def _ssd_scan_kernel(xf_ref, xtf_ref, bf_ref, cf_ref, dcf_ref, drf_ref,
                     xb_ref, xtb_ref, bb_ref, cb_ref, dcb_ref, drb_ref,
                     yf_ref, yb_ref, h_ref):
    groups, chunk = xf_ref.shape[1], xf_ref.shape[2]
    p = SSD_HEAD_DIM

    @pl.when(pl.program_id(1) == 0)
    def _():
        h_ref[...] = jnp.zeros_like(h_ref)

    dirs = ((xf_ref, xtf_ref, bf_ref, cf_ref, dcf_ref, drf_ref, yf_ref),
            (xb_ref, xtb_ref, bb_ref, cb_ref, dcb_ref, drb_ref, yb_ref))
    masks = (_chunk_mask(chunk, reverse=False), _chunk_mask(chunk, reverse=True))

    low_half = lax.broadcasted_iota(jnp.int32, (1, 2 * p), 1) < p
    pairs = [slice(k * 2 * p, (k + 1) * 2 * p) for k in range(SSD_HPG // 2)]

    def front(g, d):
        bm, cm = dirs[d][2][0, g], dirs[d][3][0, g]
        return dict(g=g, d=d, bm=bm, cm=cm, cbm=_dot_nt(cm, bm))

    def middle(st):
        g, d = st["g"], st["d"]
        _, xt_ref, _, _, dc_ref, dr_ref, _ = dirs[d]
        base = d * (LANES // 2) + g * 2 * SSD_HPG
        dcol, xt = dc_ref[0], xt_ref[0, g]
        end = 0 if d == 1 else chunk - 1
        mm, ecol, xw, dec = [], [], [], []
        for r in range(SSD_HPG):
            dt_row = dr_ref[0, base + r:base + r + 1, :]
            cs_row = dr_ref[0, base + SSD_HPG + r:base + SSD_HPG + r + 1, :]
            cs_col = _lane_pick(dcol, base + SSD_HPG + r)
            mmat = st["cbm"] * jnp.exp2(jnp.where(masks[d], cs_col - cs_row, -jnp.inf)) * dt_row
            mm.append(mmat.astype(BF16))
            ecol.append(jnp.exp2(cs_col))
            cs_end = cs_row[:, end:end + 1]
            xw.append((xt[r * p:(r + 1) * p, :].astype(F32) * (jnp.exp2(cs_end - cs_row) * dt_row)).astype(BF16))
            dec.append(jnp.exp2(cs_end))
        st.update(mm=mm, ecol=ecol, xw=jnp.concatenate(xw, axis=0), dec=dec)

    def back(st):
        g, d = st["g"], st["d"]
        x_ref, y_ref = dirs[d][0], dirs[d][6]
        for k, pair in enumerate(pairs):
            ch = _dot_nt(st["cm"], h_ref[d, g, pair, :].astype(BF16))
            xp = x_ref[0, g, :, pair]
            y0 = _dot(st["mm"][2 * k], xp) + st["ecol"][2 * k] * ch
            y1 = _dot(st["mm"][2 * k + 1], xp) + st["ecol"][2 * k + 1] * ch
            y_ref[0, g, :, pair] = jnp.where(low_half, y0, y1).astype(y_ref.dtype)
        upd = _dot(st["xw"], st["bm"])
        for r in range(SSD_HPG):
            hs = slice(r * p, (r + 1) * p)
            h_ref[d, g, hs, :] = st["dec"][r] * h_ref[d, g, hs, :] + upd[hs, :]

    prev = None
    for g in range(groups):
        for d in range(2):
            cur = front(g, d)
            if prev is not None:
                back(prev)
            middle(cur)
            prev = cur
    back(prev)


def _ssd_scan(xs_in, xs_t, bm, cm, dcol, drow, *, chunk, ctx_len):
    batch, groups, s_len, gw = xs_in.shape
    state = bm.shape[3]
    n_chunks = s_len // chunk
    fwd, bwd = _scan_chunk_maps(n_chunks, ctx_len // chunk)

    def specs(order):
        return [pl.BlockSpec((1, groups, chunk, gw), lambda b, i: (b, 0, order(i), 0)),
                pl.BlockSpec((1, groups, gw, chunk), lambda b, i: (b, 0, 0, order(i))),
                pl.BlockSpec((1, groups, chunk, state), lambda b, i: (b, 0, order(i), 0)),
                pl.BlockSpec((1, groups, chunk, state), lambda b, i: (b, 0, order(i), 0)),
                pl.BlockSpec((1, chunk, LANES), lambda b, i: (b, order(i), 0)),
                pl.BlockSpec((1, LANES, chunk), lambda b, i: (b, 0, order(i)))]

    sf, sb = specs(fwd), specs(bwd)
    out = jax.ShapeDtypeStruct((batch, groups, s_len, gw), BF16)
    args = (xs_in, xs_t, bm, cm, dcol, drow)
    return pl.pallas_call(
        _ssd_scan_kernel,
        grid=(batch, n_chunks),
        in_specs=sf + sb,
        out_specs=[sf[0], sb[0]],
        out_shape=[out, out],
        scratch_shapes=[pltpu.VMEM((2, groups, gw, state), F32)],
        compiler_params=_params(("parallel", "arbitrary")),
        name="ssd_scan",
    )(*args, *args)


def _ssd_out_kernel(*refs):
    nt = OUT_TILES
    x_refs, mod_refs, yf_refs, yb_refs, xs_refs, z_refs = (refs[k * nt:(k + 1) * nt] for k in range(6))
    dsk_ref, ng_ref, wout_ref, g3_ref, o_ref, u_ref, un_ref = refs[6 * nt:]
    groups, tm, gw = yf_refs[0].shape[1], yf_refs[0].shape[2], yf_refs[0].shape[3]
    for t in range(nt):
        rows = slice(t * tm, (t + 1) * tm)
        ssq = None
        for g in range(groups):
            sl = slice(g * gw, (g + 1) * gw)
            y = (yf_refs[t][0, g].astype(F32) + yb_refs[t][0, g].astype(F32)
                 + dsk_ref[:, sl] * xs_refs[t][0, g].astype(F32))
            u = y * _silu(z_refs[t][0, :, sl].astype(F32))
            u_ref[:, sl] = u
            part = jnp.sum(u * u, axis=-1, keepdims=True)
            ssq = part if ssq is None else ssq + part
        inv = lax.rsqrt(ssq / (groups * gw) + EPS)
        for g in range(groups):
            sl = slice(g * gw, (g + 1) * gw)
            un_ref[rows, sl] = (u_ref[:, sl] * inv * ng_ref[:, sl]).astype(BF16)
    out = None
    for g in range(groups):
        sl = slice(g * gw, (g + 1) * gw)
        part = _dot(un_ref[:, sl], wout_ref[sl, :])
        out = part if out is None else out + part
    for t in range(nt):
        rows = slice(t * tm, (t + 1) * tm)
        o_ref[rows, :] = x_refs[t][0] + mod_refs[t][0, 5:6, :] * _rms(out[rows, :], g3_ref[...])


def _ssd_out(xs, mod, yf, yb, xs_in, z, d_skip, norm_g, w_out, g3, *, tm, tile_off, n_tiles, ctx_tiles):
    batch, _, d_model = xs.shape
    groups, gw = yf.shape[1], yf.shape[3]
    inner = groups * gw
    dsk = jnp.repeat(d_skip, SSD_HEAD_DIM).reshape(1, inner)
    ng, g3r = norm_g.reshape(1, inner), g3.reshape(1, d_model)
    wout = w_out.astype(BF16)
    tok, grouped, mod_specs, n_steps = _readout_specs(batch, tm, tile_off, n_tiles, ctx_tiles)
    rows = OUT_TILES * tm
    rep = lambda a: [a] * OUT_TILES
    grp = lambda: grouped(groups, gw)
    out = pl.pallas_call(
        _ssd_out_kernel,
        grid=(n_steps,),
        in_specs=tok(d_model) + mod_specs(d_model) + grp() + grp() + grp() + tok(inner)
                 + [_resident(dsk), _resident(ng), _resident(wout), _resident(g3r)],
        out_specs=pl.BlockSpec((rows, d_model), lambda i: (i, 0)),
        out_shape=jax.ShapeDtypeStruct((batch * n_tiles * tm, d_model), F32),
        scratch_shapes=[pltpu.VMEM((tm, inner), F32), pltpu.VMEM((rows, inner), BF16)],
        compiler_params=_params(("parallel",)),
        name="ssd_out",
    )(*rep(xs), *rep(mod), *rep(yf), *rep(yb), *rep(xs_in), *rep(z), dsk, ng, wout, g3r)
    return out.reshape(batch, n_tiles * tm, d_model)


def _forward(x, c, ctx, c_ctx, ada_w, ada_b, norm_g, ffn_w_gate, ffn_w_up, ffn_w_down,
             mlstm_w_in, mlstm_conv_w, mlstm_conv_b, mlstm_w_q, mlstm_w_k, mlstm_w_v,
             mlstm_w_gates, mlstm_b_gates, mlstm_norm_g, mlstm_skip, mlstm_w_out,
             ssd_w_in, ssd_conv_w, ssd_conv_b, ssd_dt_bias, ssd_a_log, ssd_d,
             ssd_norm_g, ssd_w_out, *, tm, ml_chunk, ssd_chunk):
    batch, seq, d_model = x.shape
    ctx_len = ctx.shape[1]
    depth = ada_w.shape[0]
    assert ctx_len == tm and seq % tm == 0 and tm % GRID_W == 0 and tm % ml_chunk == 0 and tm % ssd_chunk == 0
    assert batch < MOD_ROWS
    xs = x
    ctx_tiles = ctx_len // tm
    c_all = jnp.zeros((MOD_ROWS, d_model), F32).at[:batch].set(c).at[batch].set(c_ctx)
    ffn_weights = _stack_bf16(ffn_w_gate), _stack_bf16(ffn_w_up), _stack_bf16(ffn_w_down)
    mods = _modulation(c_all, ada_w, ada_b).reshape(depth, MOD_ROWS, N_MOD, d_model)
    for i in range(depth):
        last = i == depth - 1
        mod = mods[i]
        g = norm_g[i]
        xs = _ffn(xs, mod, g[0], g[1], ffn_weights, 2 * i,
                  j0=0, tm=tm, ctx_tiles=ctx_tiles, ctx_src=ctx if i == 0 else None)
        n_tiles = xs.shape[1] // tm
        out_off = ctx_tiles if last else 0
        out_kw = dict(tm=tm, tile_off=out_off, n_tiles=n_tiles - out_off, ctx_tiles=ctx_tiles)
        j = i // 2
        if i % 2 == 0:
            q, k, v, xc, z, gates = _mlstm_in(xs, mod, g[2], mlstm_w_in[j], mlstm_conv_w[j], mlstm_conv_b[j],
                                              mlstm_w_q[j], mlstm_w_k[j], mlstm_w_v[j], mlstm_w_gates[j],
                                              mlstm_b_gates[j], tm=tm, ctx_tiles=ctx_tiles)
            hf, hb = _mlstm_scan(q, k, v, gates, chunk=ml_chunk, ctx_len=ctx_len)
            xs = _mlstm_out(xs, mod, hf, hb, xc, z, mlstm_norm_g[j], mlstm_skip[j], mlstm_w_out[j], g[3], **out_kw)
        else:
            z, xs_in, xs_t, bm, cm, dcol, drow = _ssd_in(xs, mod, g[2], ssd_w_in[j], ssd_conv_w[j], ssd_conv_b[j],
                                                         ssd_dt_bias[j], ssd_a_log[j], tm=tm, chunk=ssd_chunk,
                                                         ctx_tiles=ctx_tiles)
            yf, yb = _ssd_scan(xs_in, xs_t, bm, cm, dcol, drow, chunk=ssd_chunk, ctx_len=ctx_len)
            xs = _ssd_out(xs, mod, yf, yb, xs_in, z, ssd_d[j], ssd_norm_g[j], ssd_w_out[j], g[3], **out_kw)
        if last:
            ctx_tiles = 0
        xs = _ffn(xs, mod, g[4], g[5], ffn_weights, 2 * i + 1, j0=6, tm=tm, ctx_tiles=ctx_tiles)
    return xs


def kernel(x, c, ctx, c_ctx, ada_w, ada_b, norm_g, ffn_w_gate, ffn_w_up, ffn_w_down, mlstm_w_in, mlstm_conv_w, mlstm_conv_b, mlstm_w_q, mlstm_w_k, mlstm_w_v, mlstm_w_gates, mlstm_b_gates, mlstm_norm_g, mlstm_skip, mlstm_w_out, ssd_w_in, ssd_conv_w, ssd_conv_b, ssd_dt_bias, ssd_a_log, ssd_d, ssd_norm_g, ssd_w_out):
    return _forward(x, c, ctx, c_ctx, ada_w, ada_b, norm_g, ffn_w_gate, ffn_w_up, ffn_w_down,
                    mlstm_w_in, mlstm_conv_w, mlstm_conv_b, mlstm_w_q, mlstm_w_k, mlstm_w_v,
                    mlstm_w_gates, mlstm_b_gates, mlstm_norm_g, mlstm_skip, mlstm_w_out,
                    ssd_w_in, ssd_conv_w, ssd_conv_b, ssd_dt_bias, ssd_a_log, ssd_d,
                    ssd_norm_g, ssd_w_out, tm=TOKEN_TILE, ml_chunk=MLSTM_CHUNK, ssd_chunk=SSD_CHUNK)
```

```python
import functools

import jax
import jax.numpy as jnp
from jax import lax
from jax.experimental import pallas as pl
from jax.experimental.pallas import tpu as pltpu

F32 = jnp.float32
BF16 = jnp.bfloat16

EPS = 1e-6
LOG2_E = 1.4426950408889634
GRID_W = 64
CONV_W = 5
N_MOD = 9
ML_HEADS = 4
ML_BLOCK = 4
SSD_GROUPS = 8
SSD_HPG = 4
SSD_HEAD_DIM = 64
SSD_STATE = 128

LANES = 128
SUBLANES = 8
MXU_DIM = 256
TOKEN_TILE = 256
MLSTM_CHUNK = 256
SSD_CHUNK = 128
SSD_STEP_CHUNKS = 2
FFN_TILES = 2
OUT_TILES = 2
FF_CHUNK = MXU_DIM
BD_CHUNK = MXU_DIM
MOD_ROWS = 16
VMEM_LIMIT_BYTES = 56 * 1024 * 1024


def _rms(x, g):
    return x * lax.rsqrt(jnp.mean(x * x, axis=-1, keepdims=True) + EPS) * g


def _silu(x):
    return x * jax.nn.sigmoid(x)


def _softplus(x):
    return jnp.maximum(x, 0.0) + jnp.log1p(jnp.exp(-jnp.abs(x)))


def _log_sigmoid(x):
    return -_softplus(-x)


def _dot(a, b):
    return jnp.dot(a, b, preferred_element_type=F32)


def _dot_nt(a, b):
    return lax.dot_general(a, b, (((1,), (1,)), ((), ())), preferred_element_type=F32)


def _dot_tn(a, b):
    return lax.dot_general(a, b, (((0,), (0,)), ((), ())), preferred_element_type=F32)


def _resident(arr):
    nd = arr.ndim
    return pl.BlockSpec(arr.shape, lambda *_: (0,) * nd, pipeline_mode=pl.Buffered(1))


def _params(sem):
    return pltpu.CompilerParams(dimension_semantics=sem, vmem_limit_bytes=VMEM_LIMIT_BYTES)


def _tile_specs(batch, tm, tile_off, ctx_tiles):
    def tok(width):
        return pl.BlockSpec((1, tm, width), lambda b, j: (b, j + tile_off, 0))

    def out(width):
        return pl.BlockSpec((1, tm, width), lambda b, j: (b, j, 0))

    def mod_spec(d_model):
        return pl.BlockSpec((1, N_MOD, d_model),
                            lambda b, j: (jnp.where(j + tile_off < ctx_tiles, batch, b), 0, 0))

    return tok, out, mod_spec


def _mod_kernel(c_ref, w_ref, b_ref, o_ref):
    sc = _silu(c_ref[...])
    o_ref[...] = jnp.dot(sc, w_ref[...], preferred_element_type=F32,
                         precision=lax.Precision.HIGHEST) + b_ref[...]


def _modulation(c_all, w, b):
    depth, d_model, n_out = w.shape
    rows = c_all.shape[0]
    return pl.pallas_call(
        _mod_kernel,
        grid=(depth, n_out // d_model),
        in_specs=[pl.BlockSpec(c_all.shape, lambda l, n: (0, 0)),
                  pl.BlockSpec((None, d_model, d_model), lambda l, n: (l, 0, n)),
                  pl.BlockSpec((None, 1, d_model), lambda l, n: (l, 0, n))],
        out_specs=pl.BlockSpec((None, rows, d_model), lambda l, n: (l, 0, n)),
        out_shape=jax.ShapeDtypeStruct((depth, rows, n_out), F32),
        compiler_params=_params(("arbitrary", "arbitrary")),
        name="modulation",
    )(c_all, w, b.reshape(depth, 1, n_out))


def _cast_kernel(w_ref, o_ref):
    o_ref[...] = w_ref[...].astype(o_ref.dtype)


def _stack_bf16(w):
    w = w.reshape((-1,) + w.shape[2:])
    n, rows, cols = w.shape
    parts = 4
    assert rows % (parts * 2 * SUBLANES) == 0
    spec = pl.BlockSpec((None, rows // parts, cols), lambda i, r: (i, r, 0))
    return pl.pallas_call(
        _cast_kernel,
        grid=(n, parts),
        in_specs=[spec],
        out_specs=spec,
        out_shape=jax.ShapeDtypeStruct(w.shape, BF16),
        compiler_params=_params(("parallel", "parallel")),
        name="cast_bf16",
    )(w)


def _ffn_kernel(*refs, j0, tiles_per_row, ctx_tiles, split_ctx):
    n_x = FFN_TILES * (2 if split_ctx else 1)
    x_refs, mod_refs = refs[:n_x], refs[n_x:n_x + FFN_TILES]
    gpre_ref, gpost_ref, wg_ref, wu_ref, wd_ref, o_ref, h_ref, acc_ref = refs[n_x + FFN_TILES:]
    tm = x_refs[0].shape[0]

    def tile_input(t):
        if not split_ctx:
            return x_refs[t][...]
        j = (pl.program_id(0) * FFN_TILES + t) % tiles_per_row
        return jnp.where(j < ctx_tiles, x_refs[FFN_TILES + t][...], x_refs[t][...])

    for t, mod_ref in enumerate(mod_refs):
        rows = slice(t * tm, (t + 1) * tm)
        shift, scale = mod_ref[0, j0:j0 + 1, :], mod_ref[0, j0 + 1:j0 + 2, :]
        h_ref[rows, :] = (_rms(tile_input(t), gpre_ref[...]) * (1.0 + scale) + shift).astype(BF16)
    h = h_ref[...]
    for c in range(wg_ref.shape[1] // FF_CHUNK):
        sl = slice(c * FF_CHUNK, (c + 1) * FF_CHUNK)
        a = (_silu(_dot(h, wg_ref[:, sl])) * _dot(h, wu_ref[:, sl])).astype(BF16)
        contrib = _dot(a, wd_ref[sl, :])
        if c == 0:
            acc_ref[...] = contrib
        else:
            acc_ref[...] += contrib
    for t, mod_ref in enumerate(mod_refs):
        rows = slice(t * tm, (t + 1) * tm)
        gate = mod_ref[0, j0 + 2:j0 + 3, :]
        o_ref[rows, :] = tile_input(t) + 0.5 * gate * _rms(acc_ref[rows, :], gpost_ref[...])


def _ffn(xs, mod, g_pre, g_post, weights, which, *, j0, tm, ctx_tiles, ctx_src=None):
    batch, s_len, d_model = xs.shape
    wg, wu, wd = weights
    assert wg.shape[-1] % FF_CHUNK == 0

    def stacked(w):
        return pl.BlockSpec((None,) + w.shape[1:], lambda i: (which, 0, 0), pipeline_mode=pl.Buffered(1))

    gpre, gpost = g_pre.reshape(1, d_model), g_post.reshape(1, d_model)
    split_ctx = ctx_src is not None
    lat_per_row = s_len // tm
    tiles_per_row = lat_per_row + (ctx_tiles if split_ctx else 0)
    n_steps = batch * tiles_per_row // FFN_TILES
    assert n_steps * FFN_TILES == batch * tiles_per_row
    rows = FFN_TILES * tm

    def split(i, t):
        tile = i * FFN_TILES + t
        return tile // tiles_per_row, tile % tiles_per_row

    def mod_spec(t):
        def index(i):
            b, j = split(i, t)
            return jnp.where(j < ctx_tiles, batch, b), 0, 0
        return pl.BlockSpec((1, N_MOD, d_model), index)

    def x_spec(t):
        if not split_ctx:
            return pl.BlockSpec((tm, d_model), lambda i: (i * FFN_TILES + t, 0))

        def index(i):
            b, j = split(i, t)
            return b * lat_per_row + jnp.maximum(j - ctx_tiles, 0), 0
        return pl.BlockSpec((tm, d_model), index)

    def ctx_spec(t):
        def index(i):
            b, j = split(i, t)
            return b * ctx_tiles + jnp.minimum(j, ctx_tiles - 1), 0
        return pl.BlockSpec((tm, d_model), index)

    x_specs = [x_spec(t) for t in range(FFN_TILES)]
    x_args = [xs.reshape(batch * s_len, d_model)] * FFN_TILES
    if split_ctx:
        x_specs += [ctx_spec(t) for t in range(FFN_TILES)]
        x_args += [ctx_src.reshape(-1, d_model)] * FFN_TILES
    out = pl.pallas_call(
        functools.partial(_ffn_kernel, j0=j0, tiles_per_row=tiles_per_row, ctx_tiles=ctx_tiles,
                          split_ctx=split_ctx),
        grid=(n_steps,),
        in_specs=x_specs + [mod_spec(t) for t in range(FFN_TILES)]
                 + [_resident(gpre), _resident(gpost), stacked(wg), stacked(wu), stacked(wd)],
        out_specs=pl.BlockSpec((rows, d_model), lambda i: (i, 0)),
        out_shape=jax.ShapeDtypeStruct((batch * tiles_per_row * tm, d_model), F32),
        scratch_shapes=[pltpu.VMEM((rows, d_model), BF16), pltpu.VMEM((rows, d_model), F32)],
        compiler_params=_params(("parallel",)),
        name="ffn",
    )(*x_args, *([mod] * FFN_TILES), gpre, gpost, wg, wu, wd)
    return out.reshape(batch, tiles_per_row * tm, d_model)


def _short_conv(u, cw_ref, cb_ref, sl, is_ctx):
    tm, width = u.shape
    n_grp, per_row = tm // SUBLANES, GRID_W // SUBLANES
    pad = CONV_W // 2
    u3 = u.reshape(n_grp, SUBLANES, width)
    sub = lax.broadcasted_iota(jnp.int32, (1, SUBLANES, 1), 1)
    zero = jnp.zeros((1, SUBLANES, width), F32)
    out = cb_ref[:, sl] + u3 * cw_ref[pad:pad + 1, sl]
    for j in range(CONV_W):
        d = j - pad
        if d == 0:
            continue
        rot = pltpu.roll(u3, shift=(-d) % SUBLANES, axis=1)
        step = 1 if d > 0 else -1
        parts, run = [], []
        for g in range(n_grp):
            src = g + step
            edge = (src % per_row == 0) if d > 0 else (g % per_row == 0)
            if not edge:
                run.append(src)
                continue
            if run:
                parts.append(rot[run[0]:run[-1] + 1])
                run = []
            inside = 0 <= src < n_grp
            parts.append(jnp.where(is_ctx, rot[src:src + 1], zero) if inside else zero)
        if run:
            parts.append(rot[run[0]:run[-1] + 1])
        neighbour = jnp.concatenate(parts, axis=0)
        from_neighbour = (sub >= SUBLANES - d) if d > 0 else (sub < -d)
        out = out + jnp.where(from_neighbour, neighbour, rot) * cw_ref[j:j + 1, sl]
    return out.reshape(tm, width)


def _block_diag(w):
    per = BD_CHUNK // ML_BLOCK
    n_tiles = w.shape[0] // per
    w = w.reshape(n_tiles, per, ML_BLOCK, ML_BLOCK)
    eye = jnp.eye(per, dtype=w.dtype)
    return jnp.einsum("cnij,nm->cnimj", w, eye).reshape(n_tiles, BD_CHUNK, BD_CHUNK).astype(BF16)


def _mlstm_in_kernel(x_ref, mod_ref, g_ref, win_ref, cw_ref, cb_ref, bdq_ref, bdk_ref, bdv_ref,
                     wgq_ref, wgk_ref, wgv_ref, bg_ref,
                     q_ref, k_ref, v_ref, xc_ref, z_ref, gates_ref, *, ctx_tiles, kscale):
    tm = x_ref.shape[1]
    inner = q_ref.shape[2]
    is_ctx = pl.program_id(1) < ctx_tiles
    shift, scale = mod_ref[0, 3:4, :], mod_ref[0, 4:5, :]
    h = (_rms(x_ref[0], g_ref[...]) * (1.0 + scale) + shift).astype(BF16)
    gates = bg_ref[...]
    n_chunks = inner // BD_CHUNK
    cols = lambda c, base=0: slice(base + c * BD_CHUNK, base + (c + 1) * BD_CHUNK)

    def after_conv(c, xcb, vb, gates):
        sl = cols(c)
        q = _dot(xcb, bdq_ref[c])
        k = _dot(xcb, bdk_ref[c])
        qb, kb = q.astype(BF16), k.astype(BF16)
        q_ref[0, :, sl] = qb
        k_ref[0, :, sl] = (k * kscale).astype(BF16)
        return gates + _dot(qb, wgq_ref[sl, :]) + _dot(kb, wgk_ref[sl, :]) + _dot(vb, wgv_ref[sl, :])

    xm_next = _dot(h, win_ref[:, cols(0)])
    pending = None
    for c in range(n_chunks):
        sl = cols(c)
        xm = xm_next
        z_ref[0, :, sl] = _silu(_dot(h, win_ref[:, cols(c, inner)])).astype(BF16)
        vb = _dot(xm.astype(BF16), bdv_ref[c]).astype(BF16)
        v_ref[0, :, sl] = vb
        if c + 1 < n_chunks:
            xm_next = _dot(h, win_ref[:, cols(c + 1)])
        if pending is not None:
            gates = after_conv(*pending, gates)
        xcb = _silu(_short_conv(xm, cw_ref, cb_ref, sl, is_ctx)).astype(BF16)
        xc_ref[0, :, sl] = xcb
        pending = (c, xcb, vb)
    gates_ref[0] = after_conv(*pending, gates)


def _chunk_sum_matrices(tm, chunk):
    t_idx = lax.broadcasted_iota(jnp.int32, (tm, tm), 0)
    u_idx = lax.broadcasted_iota(jnp.int32, (tm, tm), 1)
    same = (t_idx // chunk) == (u_idx // chunk)
    return (jnp.logical_and(same, u_idx <= t_idx).astype(BF16), jnp.logical_and(same, u_idx >= t_idx).astype(BF16))


def _mlstm_in(xs, mod, g, w_in, conv_w, conv_b, w_q, w_k, w_v, w_gates, b_gates, *, tm, ctx_tiles):
    batch, s_len, d_model = xs.shape
    inner = w_in.shape[1] // 2
    n_gates = w_gates.shape[1]
    win = w_in.astype(BF16)
    cb = conv_b.reshape(1, inner)
    bdq, bdk, bdv = _block_diag(w_q), _block_diag(w_k), _block_diag(w_v)
    wg = jnp.pad(w_gates, ((0, 0), (0, LANES - n_gates))).astype(BF16)
    wgq, wgk, wgv = wg[:inner], wg[inner:2 * inner], wg[2 * inner:]
    bg = jnp.pad(b_gates, (0, LANES - n_gates)).reshape(1, LANES)
    gg = g.reshape(1, d_model)
    tok, out, mod_spec = _tile_specs(batch, tm, 0, ctx_tiles)
    act = jax.ShapeDtypeStruct((batch, s_len, inner), BF16)
    kernel = functools.partial(_mlstm_in_kernel, ctx_tiles=ctx_tiles,
                               kscale=(inner // ML_HEADS) ** -0.5)
    return pl.pallas_call(
        kernel,
        grid=(batch, s_len // tm),
        in_specs=[tok(d_model), mod_spec(d_model), _resident(gg), _resident(win), _resident(conv_w),
                  _resident(cb), _resident(bdq), _resident(bdk), _resident(bdv),
                  _resident(wgq), _resident(wgk), _resident(wgv), _resident(bg)],
        out_specs=[out(inner)] * 5 + [out(LANES)],
        out_shape=[act] * 5 + [jax.ShapeDtypeStruct((batch, s_len, LANES), F32)],
        compiler_params=_params(("parallel", "parallel")),
        name="mlstm_in",
    )(xs, mod, gg, win, conv_w, cb, bdq, bdk, bdv, wgq, wgk, wgv, bg)


def _chunk_mask(chunk, reverse):
    t_idx = lax.broadcasted_iota(jnp.int32, (chunk, chunk), 0)
    s_idx = lax.broadcasted_iota(jnp.int32, (chunk, chunk), 1)
    return s_idx >= t_idx if reverse else s_idx <= t_idx


def _lane_pick(tile, idx):
    lane = lax.broadcasted_iota(jnp.int32, (1, tile.shape[1]), 1)
    return jnp.sum(jnp.where(lane == idx, tile, 0.0), axis=1, keepdims=True)


def _mlstm_scan_kernel(qf_ref, kf_ref, vf_ref, gf_ref, qb_ref, kb_ref, vb_ref, gb_ref, tril_ref, triu_ref,
                       hf_ref, hb_ref, c_ref, n_ref, m_ref, gcol_ref, grow_ref):
    chunk = qf_ref.shape[1]
    dh = c_ref.shape[2]

    @pl.when(pl.program_id(1) == 0)
    def _():
        c_ref[...] = jnp.zeros_like(c_ref)
        n_ref[...] = jnp.zeros_like(n_ref)
        m_ref[...] = jnp.zeros_like(m_ref)

    dirs = ((qf_ref, kf_ref, vf_ref, gf_ref, hf_ref), (qb_ref, kb_ref, vb_ref, gb_ref, hb_ref))
    masks = (_chunk_mask(chunk, reverse=False), _chunk_mask(chunk, reverse=True))
    lane = lax.broadcasted_iota(jnp.int32, (1, LANES), 1)
    for d, (g_ref, sum_ref) in enumerate(((gf_ref, tril_ref), (gb_ref, triu_ref))):
        gates = g_ref[0]
        cum = _dot3(sum_ref[...], _split3(_log_sigmoid(gates)))
        tile = jnp.where(jnp.bitwise_and(lane, ML_HEADS) == 0, gates, cum)
        gcol_ref[d] = tile
        grow_ref[d] = tile.T

    ones = jnp.ones((chunk, LANES), BF16)

    def front(hd, d):
        hs = slice(hd * dh, (hd + 1) * dh)
        q = dirs[d][0][0, :, hs]
        return dict(hd=hd, d=d, hs=hs, qk=_dot_nt(q, dirs[d][1][0, :, hs]),
                    qn=_dot(q, n_ref[d, hd].astype(BF16))[:, 0:1])

    def middle(st):
        hd, d, hs = st["hd"], st["d"], st["hs"]
        k_ref = dirs[d][1]
        tri = masks[d]
        i_lane = d * 2 * ML_HEADS + hd
        f_lane = i_lane + ML_HEADS
        ig_col, b_col = _lane_pick(gcol_ref[d], i_lane), _lane_pick(gcol_ref[d], f_lane)
        ig_row, b_row = grow_ref[d, i_lane:i_lane + 1, :], grow_ref[d, f_lane:f_lane + 1, :]
        end = 0 if d == 1 else chunk - 1
        b_end = b_row[:, end:end + 1]
        m_prev = m_ref[d, hd, 0:1, 0:1]
        dm = jnp.where(tri, b_col - b_row + ig_row, -jnp.inf)
        m_inter = b_col + m_prev
        m_t = jnp.maximum(m_inter, jnp.max(dm, axis=1, keepdims=True))
        s = st["qk"] * jnp.exp(dm - m_t)
        sc = jnp.exp(m_inter - m_t)
        den = jnp.sum(s, axis=1, keepdims=True) + sc * st["qn"]
        gl = b_end - b_col + ig_col
        m_new = jnp.maximum(b_end + m_prev, jnp.max(gl, axis=0, keepdims=True))
        kw = k_ref[0, :, hs].astype(F32) * jnp.exp(gl - m_new)
        dec = jnp.exp(b_end + m_prev - m_new)
        m_ref[d, hd] = jnp.broadcast_to(m_new, m_ref.shape[2:])
        st.update(s=s.astype(BF16), sc=sc, kw=kw.astype(BF16), dec=dec,
                  inv=1.0 / jnp.maximum(jnp.abs(den), jnp.exp(-m_t)))

    def back(st):
        hd, d, hs = st["hd"], st["d"], st["hs"]
        q_ref, _, v_ref, _, o_ref = dirs[d]
        v, cmat = v_ref[0, :, hs], c_ref[d, hd]
        num = _dot(st["s"], v) + st["sc"] * _dot(q_ref[0, :, hs], cmat.astype(BF16))
        o_ref[0, :, hs] = (num * st["inv"]).astype(o_ref.dtype)
        c_ref[d, hd] = st["dec"] * cmat + _dot_tn(st["kw"], v)
        n_ref[d, hd] = st["dec"] * n_ref[d, hd] + _dot_tn(st["kw"], ones)

    prev = None
    for hd in range(ML_HEADS):
        for d in range(2):
            cur = front(hd, d)
            if prev is not None:
                back(prev)
            middle(cur)
            prev = cur
    back(prev)


def _scan_chunk_maps(n_chunks, ctx_chunks):
    def fwd(i):
        return i

    def bwd(i):
        return jnp.where(i < ctx_chunks, ctx_chunks - 1 - i, n_chunks - 1 + ctx_chunks - i)

    return fwd, bwd


def _mlstm_scan(q, k, v, gates, *, chunk, ctx_len):
    batch, s_len, inner = q.shape
    dh = inner // ML_HEADS
    n_chunks = s_len // chunk
    fwd, bwd = _scan_chunk_maps(n_chunks, ctx_len // chunk)
    tril, triu = _chunk_sum_matrices(chunk, chunk)

    def specs(order):
        head_spec = pl.BlockSpec((1, chunk, inner), lambda b, i: (b, order(i), 0))
        gate_spec = pl.BlockSpec((1, chunk, LANES), lambda b, i: (b, order(i), 0))
        return head_spec, [head_spec, head_spec, head_spec, gate_spec]

    hf_spec, in_f = specs(fwd)
    hb_spec, in_b = specs(bwd)
    out = jax.ShapeDtypeStruct((batch, s_len, inner), BF16)
    return pl.pallas_call(
        _mlstm_scan_kernel,
        grid=(batch, n_chunks),
        in_specs=in_f + in_b + [_resident(tril), _resident(triu)],
        out_specs=[hf_spec, hb_spec],
        out_shape=[out, out],
        scratch_shapes=[pltpu.VMEM((2, ML_HEADS, dh, dh), F32), pltpu.VMEM((2, ML_HEADS, dh, LANES), F32),
                        pltpu.VMEM((2, ML_HEADS, SUBLANES, LANES), F32),
                        pltpu.VMEM((2, chunk, LANES), F32), pltpu.VMEM((2, LANES, chunk), F32)],
        compiler_params=_params(("parallel", "arbitrary")),
        name="mlstm_scan",
    )(q, k, v, gates, q, k, v, gates, tril, triu)


def _mlstm_out_kernel(*refs):
    nt = OUT_TILES
    x_refs, mod_refs, hf_refs, hb_refs, xc_refs, z_refs = (refs[k * nt:(k + 1) * nt] for k in range(6))
    ng_ref, sk_ref, wout_ref, g3_ref, o_ref, u_ref = refs[6 * nt:]
    tm, inner = hf_refs[0].shape[1], hf_refs[0].shape[2]
    dh = inner // ML_HEADS
    y = None
    for hd in range(ML_HEADS):
        sl = slice(hd * dh, (hd + 1) * dh)
        for t in range(nt):
            hs = hf_refs[t][0, :, sl].astype(F32) + hb_refs[t][0, :, sl].astype(F32)
            cen = hs - jnp.mean(hs, axis=-1, keepdims=True)
            hn = cen * lax.rsqrt(jnp.mean(cen * cen, axis=-1, keepdims=True) + EPS)
            u = hn * ng_ref[:, sl] + sk_ref[:, sl] * xc_refs[t][0, :, sl].astype(F32)
            u_ref[t * tm:(t + 1) * tm, sl] = (u * z_refs[t][0, :, sl].astype(F32)).astype(BF16)
        part = _dot(u_ref[:, sl], wout_ref[sl, :])
        y = part if y is None else y + part
    for t in range(nt):
        rows = slice(t * tm, (t + 1) * tm)
        o_ref[rows, :] = x_refs[t][0] + mod_refs[t][0, 5:6, :] * _rms(y[rows, :], g3_ref[...])


def _readout_specs(batch, tm, tile_off, n_tiles, ctx_tiles):
    assert (batch * n_tiles) % OUT_TILES == 0

    def where(i, t):
        tile = i * OUT_TILES + t
        return tile // n_tiles, tile % n_tiles + tile_off

    def per_tile(make):
        return [make(t) for t in range(OUT_TILES)]

    def tok(width):
        return per_tile(lambda t: pl.BlockSpec((1, tm, width), lambda i: where(i, t) + (0,)))

    def grouped(groups, width):
        return per_tile(lambda t: pl.BlockSpec((1, groups, tm, width),
                                               lambda i: (where(i, t)[0], 0, where(i, t)[1], 0)))

    def mod(d_model):
        return per_tile(lambda t: pl.BlockSpec(
            (1, N_MOD, d_model), lambda i: (jnp.where(where(i, t)[1] < ctx_tiles, batch, where(i, t)[0]), 0, 0)))

    return tok, grouped, mod, batch * n_tiles // OUT_TILES


def _mlstm_out(xs, mod, hf, hb, xc, z, norm_g, skip, w_out, g3, *, tm, tile_off, n_tiles, ctx_tiles):
    batch, _, d_model = xs.shape
    inner = hf.shape[2]
    ng, sk, g3r = norm_g.reshape(1, inner), skip.reshape(1, inner), g3.reshape(1, d_model)
    wout = w_out.astype(BF16)
    tok, _, mod_specs, n_steps = _readout_specs(batch, tm, tile_off, n_tiles, ctx_tiles)
    rows = OUT_TILES * tm
    rep = lambda a: [a] * OUT_TILES
    out = pl.pallas_call(
        _mlstm_out_kernel,
        grid=(n_steps,),
        in_specs=tok(d_model) + mod_specs(d_model) + tok(inner) + tok(inner) + tok(inner) + tok(inner)
                 + [_resident(ng), _resident(sk), _resident(wout), _resident(g3r)],
        out_specs=pl.BlockSpec((rows, d_model), lambda i: (i, 0)),
        out_shape=jax.ShapeDtypeStruct((batch * n_tiles * tm, d_model), F32),
        scratch_shapes=[pltpu.VMEM((rows, inner), BF16)],
        compiler_params=_params(("parallel",)),
        name="mlstm_out",
    )(*rep(xs), *rep(mod), *rep(hf), *rep(hb), *rep(xc), *rep(z), ng, sk, wout, g3r)
    return out.reshape(batch, n_tiles * tm, d_model)


def _split3(x):
    hi = x.astype(BF16)
    r1 = x - hi.astype(F32)
    mid = r1.astype(BF16)
    lo = (r1 - mid.astype(F32)).astype(BF16)
    return hi, mid, lo


def _dot3(a, pieces):
    return _dot(a, pieces[0]) + _dot(a, pieces[1]) + _dot(a, pieces[2])


def _ssd_lane_source():
    src = []
    for d in range(2):
        for g in range(SSD_GROUPS):
            for _ in range(2):
                for r in range(SSD_HPG):
                    src.append(d * SSD_GROUPS * SSD_HPG + g * SSD_HPG + r)
    return jnp.array(src, jnp.int32)


def _ssd_in_kernel(x_ref, mod_ref, g_ref, wz_ref, wx_ref, wdt_ref, cw_ref, cb_ref, dtb_ref, alog_ref,
                   tril_ref, triu_ref,
                   z_ref, xs_ref, xst_ref, bm_ref, cm_ref, dcol_ref, drow_ref, ua_ref, ub_ref, *, ctx_tiles):
    tm = x_ref.shape[1]
    groups = xs_ref.shape[1]
    is_ctx = pl.program_id(1) < ctx_tiles
    shift, scale = mod_ref[0, 3:4, :], mod_ref[0, 4:5, :]
    h = (_rms(x_ref[0], g_ref[...]) * (1.0 + scale) + shift).astype(BF16)
    half = BD_CHUNK // 2
    n_z, n_x = wz_ref.shape[1] // BD_CHUNK, wx_ref.shape[1] // BD_CHUNK
    cols = lambda c: slice(c * BD_CHUNK, (c + 1) * BD_CHUNK)
    dyn0 = jnp.minimum(pl.program_id(1), 0)
    stage = (ua_ref, ub_ref)
    stage[0][dyn0] = _dot(h, wx_ref[:, cols(0)])
    for c in range(n_x):
        sl = cols(c)
        for cz in range(c * n_z // n_x, (c + 1) * n_z // n_x):
            z_ref[0, :, cols(cz)] = _dot(h, wz_ref[:, cols(cz)]).astype(BF16)
        if c + 1 < n_x:
            stage[(c + 1) % 2][dyn0] = _dot(h, wx_ref[:, cols(c + 1)])
        xbc = _silu(_short_conv(stage[c % 2][dyn0], cw_ref, cb_ref, sl, is_ctx))
        if c < groups:
            xs_ref[0, c] = xbc.astype(BF16)
            xst_ref[0, c] = xbc.astype(BF16).T
        else:
            gi = 2 * (c - groups)
            ref, gi = (bm_ref, gi) if gi < groups else (cm_ref, gi - groups)
            ref[0, gi] = xbc[:, :half].astype(BF16)
            ref[0, gi + 1] = xbc[:, half:].astype(BF16)
    dt = _softplus(_dot(h, wdt_ref[...]) + dtb_ref[...])
    la = _split3(dt * (-LOG2_E * jnp.exp(alog_ref[...])))
    lane = lax.broadcasted_iota(jnp.int32, (1, LANES), 1)
    cs = jnp.where(lane < LANES // 2, _dot3(tril_ref[...], la), _dot3(triu_ref[...], la))
    tile = jnp.where(jnp.bitwise_and(lane, SSD_HPG) == 0, dt, cs)
    dcol_ref[0] = tile
    drow_ref[0] = tile.T


def _ssd_in(xs, mod, g, w_in, conv_w, conv_b, dt_bias, a_log, *, tm, chunk, ctx_tiles):
    batch, s_len, d_model = xs.shape
    groups, state = SSD_GROUPS, SSD_STATE
    gw = SSD_HPG * SSD_HEAD_DIM
    gn = groups * state
    conv_dim = conv_w.shape[1]
    inner = conv_dim - 2 * gn
    assert gw == BD_CHUNK and 2 * state == BD_CHUNK and inner == groups * gw
    assert 4 * dt_bias.size == 2 * LANES
    wz = w_in[:, :inner].astype(BF16)
    wx = w_in[:, inner:inner + conv_dim].astype(BF16)
    src = _ssd_lane_source()
    wdt = w_in[:, inner + conv_dim:][:, src].astype(BF16)
    dtb = dt_bias.reshape(-1)[src].reshape(1, LANES)
    alog = a_log.reshape(-1)[src].reshape(1, LANES)
    tril, triu = _chunk_sum_matrices(tm, chunk)
    cb = conv_b.reshape(1, conv_dim)
    gg = g.reshape(1, d_model)
    tok, out, mod_spec = _tile_specs(batch, tm, 0, ctx_tiles)
    kernel = functools.partial(_ssd_in_kernel, ctx_tiles=ctx_tiles)
    grp = lambda w: pl.BlockSpec((1, groups, tm, w), lambda b, j: (b, 0, j, 0))
    return pl.pallas_call(
        kernel,
        grid=(batch, s_len // tm),
        in_specs=[tok(d_model), mod_spec(d_model), _resident(gg), _resident(wz), _resident(wx), _resident(wdt),
                  _resident(conv_w), _resident(cb), _resident(dtb), _resident(alog), _resident(tril), _resident(triu)],
        out_specs=[out(inner), grp(gw), pl.BlockSpec((1, groups, gw, tm), lambda b, j: (b, 0, 0, j)),
                   grp(state), grp(state), out(LANES), pl.BlockSpec((1, LANES, tm), lambda b, j: (b, 0, j))],
        out_shape=[jax.ShapeDtypeStruct((batch, s_len, inner), BF16),
                   jax.ShapeDtypeStruct((batch, groups, s_len, gw), BF16),
                   jax.ShapeDtypeStruct((batch, groups, gw, s_len), BF16),
                   jax.ShapeDtypeStruct((batch, groups, s_len, state), BF16),
                   jax.ShapeDtypeStruct((batch, groups, s_len, state), BF16),
                   jax.ShapeDtypeStruct((batch, s_len, LANES), F32),
                   jax.ShapeDtypeStruct((batch, LANES, s_len), F32)],
        scratch_shapes=[pltpu.VMEM((1, tm, BD_CHUNK), F32), pltpu.VMEM((1, tm, BD_CHUNK), F32)],
        compiler_params=_params(("parallel", "parallel")),
        name="ssd_in",
    )(xs, mod, gg, wz, wx, wdt, conv_w, cb, dtb, alog, tril, triu)


def _ssd_scan_kernel(xf_ref, xtf_ref, bf_ref, cf_ref, dcf_ref, drf_ref,
                     xb_ref, xtb_ref, bb_ref, cb_ref, dcb_ref, drb_ref,
                     yf_ref, yb_ref, h_ref, *, chunk):
    groups, n_sub = xf_ref.shape[1], xf_ref.shape[2] // chunk
    p = SSD_HEAD_DIM

    @pl.when(pl.program_id(1) == 0)
    def _():
        h_ref[...] = jnp.zeros_like(h_ref)

    dirs = ((xf_ref, xtf_ref, bf_ref, cf_ref, dcf_ref, drf_ref, yf_ref),
            (xb_ref, xtb_ref, bb_ref, cb_ref, dcb_ref, drb_ref, yb_ref))
    masks = (_chunk_mask(chunk, reverse=False), _chunk_mask(chunk, reverse=True))

    low_half = lax.broadcasted_iota(jnp.int32, (1, 2 * p), 1) < p
    pairs = [slice(k * 2 * p, (k + 1) * 2 * p) for k in range(SSD_HPG // 2)]

    def front(g, d, sub):
        rows = slice(sub * chunk, (sub + 1) * chunk)
        bm, cm = dirs[d][2][0, g, rows, :], dirs[d][3][0, g, rows, :]
        return dict(g=g, d=d, rows=rows, bm=bm, cm=cm, cbm=_dot_nt(cm, bm))

    def middle(st):
        g, d, rows = st["g"], st["d"], st["rows"]
        _, xt_ref, _, _, dc_ref, dr_ref, _ = dirs[d]
        base = d * (LANES // 2) + g * 2 * SSD_HPG
        dcol, xt = dc_ref[0, rows, :], xt_ref[0, g, :, rows]
        end = 0 if d == 1 else chunk - 1
        mm, ecol, xw, dec = [], [], [], []
        for r in range(SSD_HPG):
            dt_row = dr_ref[0, base + r:base + r + 1, rows]
            cs_row = dr_ref[0, base + SSD_HPG + r:base + SSD_HPG + r + 1, rows]
            cs_col = _lane_pick(dcol, base + SSD_HPG + r)
            mmat = st["cbm"] * jnp.exp2(jnp.where(masks[d], cs_col - cs_row, -jnp.inf)) * dt_row
            mm.append(mmat.astype(BF16))
            ecol.append(jnp.exp2(cs_col))
            cs_end = cs_row[:, end:end + 1]
            xw.append((xt[r * p:(r + 1) * p, :].astype(F32) * (jnp.exp2(cs_end - cs_row) * dt_row)).astype(BF16))
            dec.append(jnp.exp2(cs_end))
        st.update(mm=mm, ecol=ecol, xw=jnp.concatenate(xw, axis=0), dec=dec)

    def back(st):
        g, d, rows = st["g"], st["d"], st["rows"]
        x_ref, y_ref = dirs[d][0], dirs[d][6]
        for k, pair in enumerate(pairs):
            ch = _dot_nt(st["cm"], h_ref[d, g, pair, :].astype(BF16))
            xp = x_ref[0, g, rows, pair]
            y0 = _dot(st["mm"][2 * k], xp) + st["ecol"][2 * k] * ch
            y1 = _dot(st["mm"][2 * k + 1], xp) + st["ecol"][2 * k + 1] * ch
            y_ref[0, g, rows, pair] = jnp.where(low_half, y0, y1).astype(y_ref.dtype)
        upd = _dot(st["xw"], st["bm"])
        for r in range(SSD_HPG):
            hs = slice(r * p, (r + 1) * p)
            h_ref[d, g, hs, :] = st["dec"][r] * h_ref[d, g, hs, :] + upd[hs, :]

    prev = None
    for k in range(n_sub):
        for g in range(groups):
            for d in range(2):
                cur = front(g, d, k if d == 0 else n_sub - 1 - k)
                if prev is not None:
                    back(prev)
                middle(cur)
                prev = cur
    back(prev)


def _ssd_scan(xs_in, xs_t, bm, cm, dcol, drow, *, chunk, ctx_len):
    batch, groups, s_len, gw = xs_in.shape
    state = bm.shape[3]
    span = chunk * SSD_STEP_CHUNKS
    assert s_len % span == 0 and ctx_len % span == 0
    n_steps = s_len // span
    fwd, bwd = _scan_chunk_maps(n_steps, ctx_len // span)

    def specs(order):
        return [pl.BlockSpec((1, groups, span, gw), lambda b, i: (b, 0, order(i), 0)),
                pl.BlockSpec((1, groups, gw, span), lambda b, i: (b, 0, 0, order(i))),
                pl.BlockSpec((1, groups, span, state), lambda b, i: (b, 0, order(i), 0)),
                pl.BlockSpec((1, groups, span, state), lambda b, i: (b, 0, order(i), 0)),
                pl.BlockSpec((1, span, LANES), lambda b, i: (b, order(i), 0)),
                pl.BlockSpec((1, LANES, span), lambda b, i: (b, 0, order(i)))]

    sf, sb = specs(fwd), specs(bwd)
    out = jax.ShapeDtypeStruct((batch, groups, s_len, gw), BF16)
    args = (xs_in, xs_t, bm, cm, dcol, drow)
    return pl.pallas_call(
        functools.partial(_ssd_scan_kernel, chunk=chunk),
        grid=(batch, n_steps),
        in_specs=sf + sb,
        out_specs=[sf[0], sb[0]],
        out_shape=[out, out],
        scratch_shapes=[pltpu.VMEM((2, groups, gw, state), F32)],
        compiler_params=_params(("parallel", "arbitrary")),
        name="ssd_scan",
    )(*args, *args)


def _ssd_out_kernel(*refs):
    nt = OUT_TILES
    x_refs, mod_refs, yf_refs, yb_refs, xs_refs, z_refs = (refs[k * nt:(k + 1) * nt] for k in range(6))
    dsk_ref, ng_ref, wout_ref, g3_ref, o_ref, u_ref, un_ref = refs[6 * nt:]
    groups, tm, gw = yf_refs[0].shape[1], yf_refs[0].shape[2], yf_refs[0].shape[3]
    for t in range(nt):
        rows = slice(t * tm, (t + 1) * tm)
        ssq = None
        for g in range(groups):
            sl = slice(g * gw, (g + 1) * gw)
            y = (yf_refs[t][0, g].astype(F32) + yb_refs[t][0, g].astype(F32)
                 + dsk_ref[:, sl] * xs_refs[t][0, g].astype(F32))
            u = y * _silu(z_refs[t][0, :, sl].astype(F32))
            u_ref[:, sl] = u
            part = jnp.sum(u * u, axis=-1, keepdims=True)
            ssq = part if ssq is None else ssq + part
        inv = lax.rsqrt(ssq / (groups * gw) + EPS)
        for g in range(groups):
            sl = slice(g * gw, (g + 1) * gw)
            un_ref[rows, sl] = (u_ref[:, sl] * inv * ng_ref[:, sl]).astype(BF16)
    out = None
    for g in range(groups):
        sl = slice(g * gw, (g + 1) * gw)
        part = _dot(un_ref[:, sl], wout_ref[sl, :])
        out = part if out is None else out + part
    for t in range(nt):
        rows = slice(t * tm, (t + 1) * tm)
        o_ref[rows, :] = x_refs[t][0] + mod_refs[t][0, 5:6, :] * _rms(out[rows, :], g3_ref[...])


def _ssd_out(xs, mod, yf, yb, xs_in, z, d_skip, norm_g, w_out, g3, *, tm, tile_off, n_tiles, ctx_tiles):
    batch, _, d_model = xs.shape
    groups, gw = yf.shape[1], yf.shape[3]
    inner = groups * gw
    dsk = jnp.repeat(d_skip, SSD_HEAD_DIM).reshape(1, inner)
    ng, g3r = norm_g.reshape(1, inner), g3.reshape(1, d_model)
    wout = w_out.astype(BF16)
    tok, grouped, mod_specs, n_steps = _readout_specs(batch, tm, tile_off, n_tiles, ctx_tiles)
    rows = OUT_TILES * tm
    rep = lambda a: [a] * OUT_TILES
    grp = lambda: grouped(groups, gw)
    out = pl.pallas_call(
        _ssd_out_kernel,
        grid=(n_steps,),
        in_specs=tok(d_model) + mod_specs(d_model) + grp() + grp() + grp() + tok(inner)
                 + [_resident(dsk), _resident(ng), _resident(wout), _resident(g3r)],
        out_specs=pl.BlockSpec((rows, d_model), lambda i: (i, 0)),
        out_shape=jax.ShapeDtypeStruct((batch * n_tiles * tm, d_model), F32),
        scratch_shapes=[pltpu.VMEM((tm, inner), F32), pltpu.VMEM((rows, inner), BF16)],
        compiler_params=_params(("parallel",)),
        name="ssd_out",
    )(*rep(xs), *rep(mod), *rep(yf), *rep(yb), *rep(xs_in), *rep(z), dsk, ng, wout, g3r)
    return out.reshape(batch, n_tiles * tm, d_model)


def _forward(x, c, ctx, c_ctx, ada_w, ada_b, norm_g, ffn_w_gate, ffn_w_up, ffn_w_down,
             mlstm_w_in, mlstm_conv_w, mlstm_conv_b, mlstm_w_q, mlstm_w_k, mlstm_w_v,
             mlstm_w_gates, mlstm_b_gates, mlstm_norm_g, mlstm_skip, mlstm_w_out,
             ssd_w_in, ssd_conv_w, ssd_conv_b, ssd_dt_bias, ssd_a_log, ssd_d,
             ssd_norm_g, ssd_w_out, *, tm, ml_chunk, ssd_chunk):
    batch, seq, d_model = x.shape
    ctx_len = ctx.shape[1]
    depth = ada_w.shape[0]
    assert ctx_len == tm and seq % tm == 0 and tm % GRID_W == 0 and tm % ml_chunk == 0 and tm % ssd_chunk == 0
    assert batch < MOD_ROWS
    xs = x
    ctx_tiles = ctx_len // tm
    c_all = jnp.zeros((MOD_ROWS, d_model), F32).at[:batch].set(c).at[batch].set(c_ctx)
    ffn_weights = _stack_bf16(ffn_w_gate), _stack_bf16(ffn_w_up), _stack_bf16(ffn_w_down)
    mods = _modulation(c_all, ada_w, ada_b).reshape(depth, MOD_ROWS, N_MOD, d_model)
    for i in range(depth):
        last = i == depth - 1
        mod = mods[i]
        g = norm_g[i]
        xs = _ffn(xs, mod, g[0], g[1], ffn_weights, 2 * i,
                  j0=0, tm=tm, ctx_tiles=ctx_tiles, ctx_src=ctx if i == 0 else None)
        n_tiles = xs.shape[1] // tm
        out_off = ctx_tiles if last else 0
        out_kw = dict(tm=tm, tile_off=out_off, n_tiles=n_tiles - out_off, ctx_tiles=ctx_tiles)
        j = i // 2
        if i % 2 == 0:
            q, k, v, xc, z, gates = _mlstm_in(xs, mod, g[2], mlstm_w_in[j], mlstm_conv_w[j], mlstm_conv_b[j],
                                              mlstm_w_q[j], mlstm_w_k[j], mlstm_w_v[j], mlstm_w_gates[j],
                                              mlstm_b_gates[j], tm=tm, ctx_tiles=ctx_tiles)
            hf, hb = _mlstm_scan(q, k, v, gates, chunk=ml_chunk, ctx_len=ctx_len)
            xs = _mlstm_out(xs, mod, hf, hb, xc, z, mlstm_norm_g[j], mlstm_skip[j], mlstm_w_out[j], g[3], **out_kw)
        else:
            z, xs_in, xs_t, bm, cm, dcol, drow = _ssd_in(xs, mod, g[2], ssd_w_in[j], ssd_conv_w[j], ssd_conv_b[j],
                                                         ssd_dt_bias[j], ssd_a_log[j], tm=tm, chunk=ssd_chunk,
                                                         ctx_tiles=ctx_tiles)
            yf, yb = _ssd_scan(xs_in, xs_t, bm, cm, dcol, drow, chunk=ssd_chunk, ctx_len=ctx_len)
            xs = _ssd_out(xs, mod, yf, yb, xs_in, z, ssd_d[j], ssd_norm_g[j], ssd_w_out[j], g[3], **out_kw)
        if last:
            ctx_tiles = 0
        xs = _ffn(xs, mod, g[4], g[5], ffn_weights, 2 * i + 1, j0=6, tm=tm, ctx_tiles=ctx_tiles)
    return xs


def kernel(x, c, ctx, c_ctx, ada_w, ada_b, norm_g, ffn_w_gate, ffn_w_up, ffn_w_down, mlstm_w_in, mlstm_conv_w, mlstm_conv_b, mlstm_w_q, mlstm_w_k, mlstm_w_v, mlstm_w_gates, mlstm_b_gates, mlstm_norm_g, mlstm_skip, mlstm_w_out, ssd_w_in, ssd_conv_w, ssd_conv_b, ssd_dt_bias, ssd_a_log, ssd_d, ssd_norm_g, ssd_w_out):
    return _forward(x, c, ctx, c_ctx, ada_w, ada_b, norm_g, ffn_w_gate, ffn_w_up, ffn_w_down,
                    mlstm_w_in, mlstm_conv_w, mlstm_conv_b, mlstm_w_q, mlstm_w_k, mlstm_w_v,
                    mlstm_w_gates, mlstm_b_gates, mlstm_norm_g, mlstm_skip, mlstm_w_out,
                    ssd_w_in, ssd_conv_w, ssd_conv_b, ssd_dt_bias, ssd_a_log, ssd_d,
                    ssd_norm_g, ssd_w_out, tm=TOKEN_TILE, ml_chunk=MLSTM_CHUNK, ssd_chunk=SSD_CHUNK)
```

```python
import functools

import jax
import jax.numpy as jnp
from jax import lax
from jax.experimental import pallas as pl
from jax.experimental.pallas import tpu as pltpu

F32 = jnp.float32
BF16 = jnp.bfloat16

EPS = 1e-6
LOG2_E = 1.4426950408889634
GRID_W = 64
CONV_W = 5
N_MOD = 9
ML_HEADS = 4
ML_BLOCK = 4
SSD_GROUPS = 8
SSD_HPG = 4
SSD_HEAD_DIM = 64
SSD_STATE = 128

LANES = 128
SUBLANES = 8
MXU_DIM = 256
TOKEN_TILE = 256
MLSTM_CHUNK = 256
SSD_CHUNK = 128
SSD_STEP_CHUNKS = 2
FFN_TILES = 2
OUT_TILES = 2
IN_TILES = 2
FF_CHUNK = MXU_DIM
BD_CHUNK = MXU_DIM
MOD_ROWS = 16
VMEM_LIMIT_BYTES = 56 * 1024 * 1024


def _rms(x, g):
    return x * lax.rsqrt(jnp.mean(x * x, axis=-1, keepdims=True) + EPS) * g


def _silu(x):
    return x * jax.nn.sigmoid(x)


def _softplus(x):
    return jnp.maximum(x, 0.0) + jnp.log1p(jnp.exp(-jnp.abs(x)))


def _log_sigmoid(x):
    return -_softplus(-x)


def _dot(a, b):
    return jnp.dot(a, b, preferred_element_type=F32)


def _dot_nt(a, b):
    return lax.dot_general(a, b, (((1,), (1,)), ((), ())), preferred_element_type=F32)


def _dot_tn(a, b):
    return lax.dot_general(a, b, (((0,), (0,)), ((), ())), preferred_element_type=F32)


def _resident(arr):
    nd = arr.ndim
    return pl.BlockSpec(arr.shape, lambda *_: (0,) * nd, pipeline_mode=pl.Buffered(1))


def _params(sem):
    return pltpu.CompilerParams(dimension_semantics=sem, vmem_limit_bytes=VMEM_LIMIT_BYTES)


def _tile_specs(batch, tm, tile_off, ctx_tiles):
    def tok(width):
        return pl.BlockSpec((1, tm, width), lambda b, j: (b, j + tile_off, 0))

    def out(width):
        return pl.BlockSpec((1, tm, width), lambda b, j: (b, j, 0))

    def mod_spec(d_model):
        return pl.BlockSpec((1, N_MOD, d_model),
                            lambda b, j: (jnp.where(j + tile_off < ctx_tiles, batch, b), 0, 0))

    return tok, out, mod_spec


def _mod_kernel(c_ref, w_ref, b_ref, o_ref):
    sc = _silu(c_ref[...])
    o_ref[...] = jnp.dot(sc, w_ref[...], preferred_element_type=F32,
                         precision=lax.Precision.HIGHEST) + b_ref[...]


def _modulation(c_all, w, b):
    depth, d_model, n_out = w.shape
    rows = c_all.shape[0]
    return pl.pallas_call(
        _mod_kernel,
        grid=(depth, n_out // d_model),
        in_specs=[pl.BlockSpec(c_all.shape, lambda l, n: (0, 0)),
                  pl.BlockSpec((None, d_model, d_model), lambda l, n: (l, 0, n)),
                  pl.BlockSpec((None, 1, d_model), lambda l, n: (l, 0, n))],
        out_specs=pl.BlockSpec((None, rows, d_model), lambda l, n: (l, 0, n)),
        out_shape=jax.ShapeDtypeStruct((depth, rows, n_out), F32),
        compiler_params=_params(("arbitrary", "arbitrary")),
        name="modulation",
    )(c_all, w, b.reshape(depth, 1, n_out))


def _cast_kernel(w_ref, o_ref):
    o_ref[...] = w_ref[...].astype(o_ref.dtype)


def _stack_bf16(w):
    w = w.reshape((-1,) + w.shape[2:])
    n, rows, cols = w.shape
    parts = 4
    assert rows % (parts * 2 * SUBLANES) == 0
    spec = pl.BlockSpec((None, rows // parts, cols), lambda i, r: (i, r, 0))
    return pl.pallas_call(
        _cast_kernel,
        grid=(n, parts),
        in_specs=[spec],
        out_specs=spec,
        out_shape=jax.ShapeDtypeStruct(w.shape, BF16),
        compiler_params=_params(("parallel", "parallel")),
        name="cast_bf16",
    )(w)


def _ffn_kernel(*refs, j0, tiles_per_row, ctx_tiles, split_ctx):
    n_x = FFN_TILES * (2 if split_ctx else 1)
    x_refs, mod_refs = refs[:n_x], refs[n_x:n_x + FFN_TILES]
    gpre_ref, gpost_ref, wg_ref, wu_ref, wd_ref, o_ref, h_ref, acc_ref = refs[n_x + FFN_TILES:]
    tm = x_refs[0].shape[0]

    def tile_input(t):
        if not split_ctx:
            return x_refs[t][...]
        j = (pl.program_id(0) * FFN_TILES + t) % tiles_per_row
        return jnp.where(j < ctx_tiles, x_refs[FFN_TILES + t][...], x_refs[t][...])

    for t, mod_ref in enumerate(mod_refs):
        rows = slice(t * tm, (t + 1) * tm)
        shift, scale = mod_ref[0, j0:j0 + 1, :], mod_ref[0, j0 + 1:j0 + 2, :]
        h_ref[rows, :] = (_rms(tile_input(t), gpre_ref[...]) * (1.0 + scale) + shift).astype(BF16)
    h = h_ref[...]
    for c in range(wg_ref.shape[1] // FF_CHUNK):
        sl = slice(c * FF_CHUNK, (c + 1) * FF_CHUNK)
        a = (_silu(_dot(h, wg_ref[:, sl])) * _dot(h, wu_ref[:, sl])).astype(BF16)
        contrib = _dot(a, wd_ref[sl, :])
        if c == 0:
            acc_ref[...] = contrib
        else:
            acc_ref[...] += contrib
    for t, mod_ref in enumerate(mod_refs):
        rows = slice(t * tm, (t + 1) * tm)
        gate = mod_ref[0, j0 + 2:j0 + 3, :]
        o_ref[rows, :] = tile_input(t) + 0.5 * gate * _rms(acc_ref[rows, :], gpost_ref[...])


def _ffn(xs, mod, g_pre, g_post, weights, which, *, j0, tm, ctx_tiles, ctx_src=None):
    batch, s_len, d_model = xs.shape
    wg, wu, wd = weights
    assert wg.shape[-1] % FF_CHUNK == 0

    def stacked(w):
        return pl.BlockSpec((None,) + w.shape[1:], lambda i: (which, 0, 0), pipeline_mode=pl.Buffered(1))

    gpre, gpost = g_pre.reshape(1, d_model), g_post.reshape(1, d_model)
    split_ctx = ctx_src is not None
    lat_per_row = s_len // tm
    tiles_per_row = lat_per_row + (ctx_tiles if split_ctx else 0)
    n_steps = batch * tiles_per_row // FFN_TILES
    assert n_steps * FFN_TILES == batch * tiles_per_row
    rows = FFN_TILES * tm

    def split(i, t):
        tile = i * FFN_TILES + t
        return tile // tiles_per_row, tile % tiles_per_row

    def mod_spec(t):
        def index(i):
            b, j = split(i, t)
            return jnp.where(j < ctx_tiles, batch, b), 0, 0
        return pl.BlockSpec((1, N_MOD, d_model), index)

    def x_spec(t):
        if not split_ctx:
            return pl.BlockSpec((tm, d_model), lambda i: (i * FFN_TILES + t, 0))

        def index(i):
            b, j = split(i, t)
            return b * lat_per_row + jnp.maximum(j - ctx_tiles, 0), 0
        return pl.BlockSpec((tm, d_model), index)

    def ctx_spec(t):
        def index(i):
            b, j = split(i, t)
            return b * ctx_tiles + jnp.minimum(j, ctx_tiles - 1), 0
        return pl.BlockSpec((tm, d_model), index)

    x_specs = [x_spec(t) for t in range(FFN_TILES)]
    x_args = [xs.reshape(batch * s_len, d_model)] * FFN_TILES
    if split_ctx:
        x_specs += [ctx_spec(t) for t in range(FFN_TILES)]
        x_args += [ctx_src.reshape(-1, d_model)] * FFN_TILES
    out = pl.pallas_call(
        functools.partial(_ffn_kernel, j0=j0, tiles_per_row=tiles_per_row, ctx_tiles=ctx_tiles,
                          split_ctx=split_ctx),
        grid=(n_steps,),
        in_specs=x_specs + [mod_spec(t) for t in range(FFN_TILES)]
                 + [_resident(gpre), _resident(gpost), stacked(wg), stacked(wu), stacked(wd)],
        out_specs=pl.BlockSpec((rows, d_model), lambda i: (i, 0)),
        out_shape=jax.ShapeDtypeStruct((batch * tiles_per_row * tm, d_model), F32),
        scratch_shapes=[pltpu.VMEM((rows, d_model), BF16), pltpu.VMEM((rows, d_model), F32)],
        compiler_params=_params(("parallel",)),
        name="ffn",
    )(*x_args, *([mod] * FFN_TILES), gpre, gpost, wg, wu, wd)
    return out.reshape(batch, tiles_per_row * tm, d_model)


def _short_conv(u, cw_ref, cb_ref, sl, is_ctx):
    tm, width = u.shape
    n_grp, per_row = tm // SUBLANES, GRID_W // SUBLANES
    pad = CONV_W // 2
    u3 = u.reshape(n_grp, SUBLANES, width)
    sub = lax.broadcasted_iota(jnp.int32, (1, SUBLANES, 1), 1)
    zero = jnp.zeros((1, SUBLANES, width), F32)
    out = cb_ref[:, sl] + u3 * cw_ref[pad:pad + 1, sl]
    for j in range(CONV_W):
        d = j - pad
        if d == 0:
            continue
        rot = pltpu.roll(u3, shift=(-d) % SUBLANES, axis=1)
        step = 1 if d > 0 else -1
        parts, run = [], []
        for g in range(n_grp):
            src = g + step
            edge = (src % per_row == 0) if d > 0 else (g % per_row == 0)
            if not edge:
                run.append(src)
                continue
            if run:
                parts.append(rot[run[0]:run[-1] + 1])
                run = []
            inside = 0 <= src < n_grp
            parts.append(jnp.where(is_ctx, rot[src:src + 1], zero) if inside else zero)
        if run:
            parts.append(rot[run[0]:run[-1] + 1])
        neighbour = jnp.concatenate(parts, axis=0)
        from_neighbour = (sub >= SUBLANES - d) if d > 0 else (sub < -d)
        out = out + jnp.where(from_neighbour, neighbour, rot) * cw_ref[j:j + 1, sl]
    return out.reshape(tm, width)


def _block_diag(w):
    per = BD_CHUNK // ML_BLOCK
    n_tiles = w.shape[0] // per
    w = w.reshape(n_tiles, per, ML_BLOCK, ML_BLOCK)
    eye = jnp.eye(per, dtype=w.dtype)
    return jnp.einsum("cnij,nm->cnimj", w, eye).reshape(n_tiles, BD_CHUNK, BD_CHUNK).astype(BF16)


def _mlstm_in_kernel(*refs, tiles_per_row, ctx_tiles, kscale):
    nt = IN_TILES
    x_ref, mod_refs = refs[0], refs[1:1 + nt]
    (g_ref, win_ref, cw_ref, cb_ref, bdq_ref, bdk_ref, bdv_ref, wgq_ref, wgk_ref, wgv_ref, bg_ref,
     q_ref, k_ref, v_ref, xc_ref, z_ref, gates_ref, h_ref) = refs[1 + nt:]
    tm = x_ref.shape[0] // nt
    inner = q_ref.shape[1]
    tile_rows = [slice(t * tm, (t + 1) * tm) for t in range(nt)]
    is_ctx = [(pl.program_id(0) * nt + t) % tiles_per_row < ctx_tiles for t in range(nt)]
    for rows, mod_ref in zip(tile_rows, mod_refs):
        shift, scale = mod_ref[0, 3:4, :], mod_ref[0, 4:5, :]
        h_ref[rows, :] = (_rms(x_ref[rows, :], g_ref[...]) * (1.0 + scale) + shift).astype(BF16)
    h = h_ref[...]
    gates = bg_ref[...]
    n_chunks = inner // BD_CHUNK
    cols = lambda c, base=0: slice(base + c * BD_CHUNK, base + (c + 1) * BD_CHUNK)

    def after_conv(c, xcb, vb, gates):
        sl = cols(c)
        q = _dot(xcb, bdq_ref[c])
        k = _dot(xcb, bdk_ref[c])
        qb, kb = q.astype(BF16), k.astype(BF16)
        q_ref[:, sl] = qb
        k_ref[:, sl] = (k * kscale).astype(BF16)
        return gates + _dot(qb, wgq_ref[sl, :]) + _dot(kb, wgk_ref[sl, :]) + _dot(vb, wgv_ref[sl, :])

    xm_next = _dot(h, win_ref[:, cols(0)])
    pending = None
    for c in range(n_chunks):
        sl = cols(c)
        xm = xm_next
        z_ref[:, sl] = _silu(_dot(h, win_ref[:, cols(c, inner)])).astype(BF16)
        vb = _dot(xm.astype(BF16), bdv_ref[c]).astype(BF16)
        v_ref[:, sl] = vb
        if c + 1 < n_chunks:
            xm_next = _dot(h, win_ref[:, cols(c + 1)])
        if pending is not None:
            gates = after_conv(*pending, gates)
        xcb = jnp.concatenate([_silu(_short_conv(xm[rows, :], cw_ref, cb_ref, sl, ctx)).astype(BF16)
                               for rows, ctx in zip(tile_rows, is_ctx)], axis=0)
        xc_ref[:, sl] = xcb
        pending = (c, xcb, vb)
    gates_ref[...] = after_conv(*pending, gates)


def _chunk_sum_matrices(tm, chunk):
    t_idx = lax.broadcasted_iota(jnp.int32, (tm, tm), 0)
    u_idx = lax.broadcasted_iota(jnp.int32, (tm, tm), 1)
    same = (t_idx // chunk) == (u_idx // chunk)
    return (jnp.logical_and(same, u_idx <= t_idx).astype(BF16), jnp.logical_and(same, u_idx >= t_idx).astype(BF16))


def _mlstm_in(xs, mod, g, w_in, conv_w, conv_b, w_q, w_k, w_v, w_gates, b_gates, *, tm, ctx_tiles):
    batch, s_len, d_model = xs.shape
    inner = w_in.shape[1] // 2
    n_gates = w_gates.shape[1]
    win = w_in.astype(BF16)
    cb = conv_b.reshape(1, inner)
    bdq, bdk, bdv = _block_diag(w_q), _block_diag(w_k), _block_diag(w_v)
    wg = jnp.pad(w_gates, ((0, 0), (0, LANES - n_gates))).astype(BF16)
    wgq, wgk, wgv = wg[:inner], wg[inner:2 * inner], wg[2 * inner:]
    bg = jnp.pad(b_gates, (0, LANES - n_gates)).reshape(1, LANES)
    gg = g.reshape(1, d_model)
    tiles_per_row = s_len // tm
    n_steps = batch * tiles_per_row // IN_TILES
    assert n_steps * IN_TILES == batch * tiles_per_row
    rows = IN_TILES * tm

    def mod_spec(t):
        def index(i):
            tile = i * IN_TILES + t
            return jnp.where(tile % tiles_per_row < ctx_tiles, batch, tile // tiles_per_row), 0, 0
        return pl.BlockSpec((1, N_MOD, d_model), index)

    flat = lambda width: pl.BlockSpec((rows, width), lambda i: (i, 0))
    act = jax.ShapeDtypeStruct((batch * s_len, inner), BF16)
    kernel = functools.partial(_mlstm_in_kernel, tiles_per_row=tiles_per_row, ctx_tiles=ctx_tiles,
                               kscale=(inner // ML_HEADS) ** -0.5)
    outs = pl.pallas_call(
        kernel,
        grid=(n_steps,),
        in_specs=[flat(d_model)] + [mod_spec(t) for t in range(IN_TILES)]
                 + [_resident(gg), _resident(win), _resident(conv_w), _resident(cb), _resident(bdq), _resident(bdk),
                    _resident(bdv), _resident(wgq), _resident(wgk), _resident(wgv), _resident(bg)],
        out_specs=[flat(inner)] * 5 + [flat(LANES)],
        out_shape=[act] * 5 + [jax.ShapeDtypeStruct((batch * s_len, LANES), F32)],
        scratch_shapes=[pltpu.VMEM((rows, d_model), BF16)],
        compiler_params=_params(("parallel",)),
        name="mlstm_in",
    )(xs.reshape(batch * s_len, d_model), *([mod] * IN_TILES), gg, win, conv_w, cb, bdq, bdk, bdv, wgq, wgk, wgv, bg)
    return [o.reshape(batch, s_len, o.shape[1]) for o in outs]


def _chunk_mask(chunk, reverse):
    t_idx = lax.broadcasted_iota(jnp.int32, (chunk, chunk), 0)
    s_idx = lax.broadcasted_iota(jnp.int32, (chunk, chunk), 1)
    return s_idx >= t_idx if reverse else s_idx <= t_idx


def _lane_pick(tile, idx):
    lane = lax.broadcasted_iota(jnp.int32, (1, tile.shape[1]), 1)
    return jnp.sum(jnp.where(lane == idx, tile, 0.0), axis=1, keepdims=True)


def _mlstm_scan_kernel(qf_ref, kf_ref, vf_ref, gf_ref, qb_ref, kb_ref, vb_ref, gb_ref, tril_ref, triu_ref,
                       hf_ref, hb_ref, c_ref, n_ref, m_ref, gcol_ref, grow_ref):
    chunk = qf_ref.shape[1]
    dh = c_ref.shape[2]

    @pl.when(pl.program_id(1) == 0)
    def _():
        c_ref[...] = jnp.zeros_like(c_ref)
        n_ref[...] = jnp.zeros_like(n_ref)
        m_ref[...] = jnp.zeros_like(m_ref)

    dirs = ((qf_ref, kf_ref, vf_ref, gf_ref, hf_ref), (qb_ref, kb_ref, vb_ref, gb_ref, hb_ref))
    masks = (_chunk_mask(chunk, reverse=False), _chunk_mask(chunk, reverse=True))
    lane = lax.broadcasted_iota(jnp.int32, (1, LANES), 1)
    for d, (g_ref, sum_ref) in enumerate(((gf_ref, tril_ref), (gb_ref, triu_ref))):
        gates = g_ref[0]
        cum = _dot3(sum_ref[...], _split3(_log_sigmoid(gates)))
        tile = jnp.where(jnp.bitwise_and(lane, ML_HEADS) == 0, gates, cum)
        gcol_ref[d] = tile
        grow_ref[d] = tile.T

    ones = jnp.ones((chunk, LANES), BF16)

    def front(hd, d):
        hs = slice(hd * dh, (hd + 1) * dh)
        q = dirs[d][0][0, :, hs]
        return dict(hd=hd, d=d, hs=hs, qk=_dot_nt(q, dirs[d][1][0, :, hs]),
                    qn=_dot(q, n_ref[d, hd].astype(BF16))[:, 0:1])

    def middle(st):
        hd, d, hs = st["hd"], st["d"], st["hs"]
        k_ref = dirs[d][1]
        tri = masks[d]
        i_lane = d * 2 * ML_HEADS + hd
        f_lane = i_lane + ML_HEADS
        ig_col, b_col = _lane_pick(gcol_ref[d], i_lane), _lane_pick(gcol_ref[d], f_lane)
        ig_row, b_row = grow_ref[d, i_lane:i_lane + 1, :], grow_ref[d, f_lane:f_lane + 1, :]
        end = 0 if d == 1 else chunk - 1
        b_end = b_row[:, end:end + 1]
        m_prev = m_ref[d, hd, 0:1, 0:1]
        dm = jnp.where(tri, b_col - b_row + ig_row, -jnp.inf)
        m_inter = b_col + m_prev
        m_t = jnp.maximum(m_inter, jnp.max(dm, axis=1, keepdims=True))
        s = st["qk"] * jnp.exp(dm - m_t)
        sc = jnp.exp(m_inter - m_t)
        den = jnp.sum(s, axis=1, keepdims=True) + sc * st["qn"]
        gl = b_end - b_col + ig_col
        m_new = jnp.maximum(b_end + m_prev, jnp.max(gl, axis=0, keepdims=True))
        kw = k_ref[0, :, hs].astype(F32) * jnp.exp(gl - m_new)
        dec = jnp.exp(b_end + m_prev - m_new)
        m_ref[d, hd] = jnp.broadcast_to(m_new, m_ref.shape[2:])
        st.update(s=s.astype(BF16), sc=sc, kw=kw.astype(BF16), dec=dec,
                  inv=1.0 / jnp.maximum(jnp.abs(den), jnp.exp(-m_t)))

    def back(st):
        hd, d, hs = st["hd"], st["d"], st["hs"]
        q_ref, _, v_ref, _, o_ref = dirs[d]
        v, cmat = v_ref[0, :, hs], c_ref[d, hd]
        num = _dot(st["s"], v) + st["sc"] * _dot(q_ref[0, :, hs], cmat.astype(BF16))
        o_ref[0, :, hs] = (num * st["inv"]).astype(o_ref.dtype)
        c_ref[d, hd] = st["dec"] * cmat + _dot_tn(st["kw"], v)
        n_ref[d, hd] = st["dec"] * n_ref[d, hd] + _dot_tn(st["kw"], ones)

    prev = None
    for hd in range(ML_HEADS):
        for d in range(2):
            cur = front(hd, d)
            if prev is not None:
                back(prev)
            middle(cur)
            prev = cur
    back(prev)


def _scan_chunk_maps(n_chunks, ctx_chunks):
    def fwd(i):
        return i

    def bwd(i):
        return jnp.where(i < ctx_chunks, ctx_chunks - 1 - i, n_chunks - 1 + ctx_chunks - i)

    return fwd, bwd


def _mlstm_scan(q, k, v, gates, *, chunk, ctx_len):
    batch, s_len, inner = q.shape
    dh = inner // ML_HEADS
    n_chunks = s_len // chunk
    fwd, bwd = _scan_chunk_maps(n_chunks, ctx_len // chunk)
    tril, triu = _chunk_sum_matrices(chunk, chunk)

    def specs(order):
        head_spec = pl.BlockSpec((1, chunk, inner), lambda b, i: (b, order(i), 0))
        gate_spec = pl.BlockSpec((1, chunk, LANES), lambda b, i: (b, order(i), 0))
        return head_spec, [head_spec, head_spec, head_spec, gate_spec]

    hf_spec, in_f = specs(fwd)
    hb_spec, in_b = specs(bwd)
    out = jax.ShapeDtypeStruct((batch, s_len, inner), BF16)
    return pl.pallas_call(
        _mlstm_scan_kernel,
        grid=(batch, n_chunks),
        in_specs=in_f + in_b + [_resident(tril), _resident(triu)],
        out_specs=[hf_spec, hb_spec],
        out_shape=[out, out],
        scratch_shapes=[pltpu.VMEM((2, ML_HEADS, dh, dh), F32), pltpu.VMEM((2, ML_HEADS, dh, LANES), F32),
                        pltpu.VMEM((2, ML_HEADS, SUBLANES, LANES), F32),
                        pltpu.VMEM((2, chunk, LANES), F32), pltpu.VMEM((2, LANES, chunk), F32)],
        compiler_params=_params(("parallel", "arbitrary")),
        name="mlstm_scan",
    )(q, k, v, gates, q, k, v, gates, tril, triu)


def _mlstm_out_kernel(*refs):
    nt = OUT_TILES
    x_refs, mod_refs, hf_refs, hb_refs, xc_refs, z_refs = (refs[k * nt:(k + 1) * nt] for k in range(6))
    ng_ref, sk_ref, wout_ref, g3_ref, o_ref, u_ref = refs[6 * nt:]
    tm, inner = hf_refs[0].shape[1], hf_refs[0].shape[2]
    dh = inner // ML_HEADS
    y = None
    for hd in range(ML_HEADS):
        sl = slice(hd * dh, (hd + 1) * dh)
        for t in range(nt):
            hs = hf_refs[t][0, :, sl].astype(F32) + hb_refs[t][0, :, sl].astype(F32)
            cen = hs - jnp.mean(hs, axis=-1, keepdims=True)
            hn = cen * lax.rsqrt(jnp.mean(cen * cen, axis=-1, keepdims=True) + EPS)
            u = hn * ng_ref[:, sl] + sk_ref[:, sl] * xc_refs[t][0, :, sl].astype(F32)
            u_ref[t * tm:(t + 1) * tm, sl] = (u * z_refs[t][0, :, sl].astype(F32)).astype(BF16)
        part = _dot(u_ref[:, sl], wout_ref[sl, :])
        y = part if y is None else y + part
    for t in range(nt):
        rows = slice(t * tm, (t + 1) * tm)
        o_ref[rows, :] = x_refs[t][0] + mod_refs[t][0, 5:6, :] * _rms(y[rows, :], g3_ref[...])


def _readout_specs(batch, tm, tile_off, n_tiles, ctx_tiles):
    assert (batch * n_tiles) % OUT_TILES == 0

    def where(i, t):
        tile = i * OUT_TILES + t
        return tile // n_tiles, tile % n_tiles + tile_off

    def per_tile(make):
        return [make(t) for t in range(OUT_TILES)]

    def tok(width):
        return per_tile(lambda t: pl.BlockSpec((1, tm, width), lambda i: where(i, t) + (0,)))

    def grouped(groups, width):
        return per_tile(lambda t: pl.BlockSpec((1, groups, tm, width),
                                               lambda i: (where(i, t)[0], 0, where(i, t)[1], 0)))

    def mod(d_model):
        return per_tile(lambda t: pl.BlockSpec(
            (1, N_MOD, d_model), lambda i: (jnp.where(where(i, t)[1] < ctx_tiles, batch, where(i, t)[0]), 0, 0)))

    return tok, grouped, mod, batch * n_tiles // OUT_TILES


def _mlstm_out(xs, mod, hf, hb, xc, z, norm_g, skip, w_out, g3, *, tm, tile_off, n_tiles, ctx_tiles):
    batch, _, d_model = xs.shape
    inner = hf.shape[2]
    ng, sk, g3r = norm_g.reshape(1, inner), skip.reshape(1, inner), g3.reshape(1, d_model)
    wout = w_out.astype(BF16)
    tok, _, mod_specs, n_steps = _readout_specs(batch, tm, tile_off, n_tiles, ctx_tiles)
    rows = OUT_TILES * tm
    rep = lambda a: [a] * OUT_TILES
    out = pl.pallas_call(
        _mlstm_out_kernel,
        grid=(n_steps,),
        in_specs=tok(d_model) + mod_specs(d_model) + tok(inner) + tok(inner) + tok(inner) + tok(inner)
                 + [_resident(ng), _resident(sk), _resident(wout), _resident(g3r)],
        out_specs=pl.BlockSpec((rows, d_model), lambda i: (i, 0)),
        out_shape=jax.ShapeDtypeStruct((batch * n_tiles * tm, d_model), F32),
        scratch_shapes=[pltpu.VMEM((rows, inner), BF16)],
        compiler_params=_params(("parallel",)),
        name="mlstm_out",
    )(*rep(xs), *rep(mod), *rep(hf), *rep(hb), *rep(xc), *rep(z), ng, sk, wout, g3r)
    return out.reshape(batch, n_tiles * tm, d_model)


def _split3(x):
    hi = x.astype(BF16)
    r1 = x - hi.astype(F32)
    mid = r1.astype(BF16)
    lo = (r1 - mid.astype(F32)).astype(BF16)
    return hi, mid, lo


def _dot3(a, pieces):
    return _dot(a, pieces[0]) + _dot(a, pieces[1]) + _dot(a, pieces[2])


def _ssd_lane_source():
    src = []
    for d in range(2):
        for g in range(SSD_GROUPS):
            for _ in range(2):
                for r in range(SSD_HPG):
                    src.append(d * SSD_GROUPS * SSD_HPG + g * SSD_HPG + r)
    return jnp.array(src, jnp.int32)


def _ssd_in_kernel(x_ref, mod_ref, g_ref, wz_ref, wx_ref, wdt_ref, cw_ref, cb_ref, dtb_ref, alog_ref,
                   tril_ref, triu_ref,
                   z_ref, xs_ref, xst_ref, bm_ref, cm_ref, dcol_ref, drow_ref, ua_ref, ub_ref, *, ctx_tiles):
    tm = x_ref.shape[1]
    groups = xs_ref.shape[1]
    is_ctx = pl.program_id(1) < ctx_tiles
    shift, scale = mod_ref[0, 3:4, :], mod_ref[0, 4:5, :]
    h = (_rms(x_ref[0], g_ref[...]) * (1.0 + scale) + shift).astype(BF16)
    half = BD_CHUNK // 2
    n_z, n_x = wz_ref.shape[1] // BD_CHUNK, wx_ref.shape[1] // BD_CHUNK
    cols = lambda c: slice(c * BD_CHUNK, (c + 1) * BD_CHUNK)
    dyn0 = jnp.minimum(pl.program_id(1), 0)
    stage = (ua_ref, ub_ref)
    stage[0][dyn0] = _dot(h, wx_ref[:, cols(0)])
    for c in range(n_x):
        sl = cols(c)
        for cz in range(c * n_z // n_x, (c + 1) * n_z // n_x):
            z_ref[0, :, cols(cz)] = _dot(h, wz_ref[:, cols(cz)]).astype(BF16)
        if c + 1 < n_x:
            stage[(c + 1) % 2][dyn0] = _dot(h, wx_ref[:, cols(c + 1)])
        xbc = _silu(_short_conv(stage[c % 2][dyn0], cw_ref, cb_ref, sl, is_ctx))
        if c < groups:
            xs_ref[0, c] = xbc.astype(BF16)
            xst_ref[0, c] = xbc.astype(BF16).T
        else:
            gi = 2 * (c - groups)
            ref, gi = (bm_ref, gi) if gi < groups else (cm_ref, gi - groups)
            ref[0, gi] = xbc[:, :half].astype(BF16)
            ref[0, gi + 1] = xbc[:, half:].astype(BF16)
    dt = _softplus(_dot(h, wdt_ref[...]) + dtb_ref[...])
    la = _split3(dt * (-LOG2_E * jnp.exp(alog_ref[...])))
    lane = lax.broadcasted_iota(jnp.int32, (1, LANES), 1)
    cs = jnp.where(lane < LANES // 2, _dot3(tril_ref[...], la), _dot3(triu_ref[...], la))
    tile = jnp.where(jnp.bitwise_and(lane, SSD_HPG) == 0, dt, cs)
    dcol_ref[0] = tile
    drow_ref[0] = tile.T


def _ssd_in(xs, mod, g, w_in, conv_w, conv_b, dt_bias, a_log, *, tm, chunk, ctx_tiles):
    batch, s_len, d_model = xs.shape
    groups, state = SSD_GROUPS, SSD_STATE
    gw = SSD_HPG * SSD_HEAD_DIM
    gn = groups * state
    conv_dim = conv_w.shape[1]
    inner = conv_dim - 2 * gn
    assert gw == BD_CHUNK and 2 * state == BD_CHUNK and inner == groups * gw
    assert 4 * dt_bias.size == 2 * LANES
    wz = w_in[:, :inner].astype(BF16)
    wx = w_in[:, inner:inner + conv_dim].astype(BF16)
    src = _ssd_lane_source()
    wdt = w_in[:, inner + conv_dim:][:, src].astype(BF16)
    dtb = dt_bias.reshape(-1)[src].reshape(1, LANES)
    alog = a_log.reshape(-1)[src].reshape(1, LANES)
    tril, triu = _chunk_sum_matrices(tm, chunk)
    cb = conv_b.reshape(1, conv_dim)
    gg = g.reshape(1, d_model)
    tok, out, mod_spec = _tile_specs(batch, tm, 0, ctx_tiles)
    kernel = functools.partial(_ssd_in_kernel, ctx_tiles=ctx_tiles)
    grp = lambda w: pl.BlockSpec((1, groups, tm, w), lambda b, j: (b, 0, j, 0))
    return pl.pallas_call(
        kernel,
        grid=(batch, s_len // tm),
        in_specs=[tok(d_model), mod_spec(d_model), _resident(gg), _resident(wz), _resident(wx), _resident(wdt),
                  _resident(conv_w), _resident(cb), _resident(dtb), _resident(alog), _resident(tril), _resident(triu)],
        out_specs=[out(inner), grp(gw), pl.BlockSpec((1, groups, gw, tm), lambda b, j: (b, 0, 0, j)),
                   grp(state), grp(state), out(LANES), pl.BlockSpec((1, LANES, tm), lambda b, j: (b, 0, j))],
        out_shape=[jax.ShapeDtypeStruct((batch, s_len, inner), BF16),
                   jax.ShapeDtypeStruct((batch, groups, s_len, gw), BF16),
                   jax.ShapeDtypeStruct((batch, groups, gw, s_len), BF16),
                   jax.ShapeDtypeStruct((batch, groups, s_len, state), BF16),
                   jax.ShapeDtypeStruct((batch, groups, s_len, state), BF16),
                   jax.ShapeDtypeStruct((batch, s_len, LANES), F32),
                   jax.ShapeDtypeStruct((batch, LANES, s_len), F32)],
        scratch_shapes=[pltpu.VMEM((1, tm, BD_CHUNK), F32), pltpu.VMEM((1, tm, BD_CHUNK), F32)],
        compiler_params=_params(("parallel", "parallel")),
        name="ssd_in",
    )(xs, mod, gg, wz, wx, wdt, conv_w, cb, dtb, alog, tril, triu)


def _ssd_scan_kernel(xf_ref, xtf_ref, bf_ref, cf_ref, dcf_ref, drf_ref,
                     xb_ref, xtb_ref, bb_ref, cb_ref, dcb_ref, drb_ref,
                     yf_ref, yb_ref, h_ref, *, chunk):
    groups, n_sub = xf_ref.shape[1], xf_ref.shape[2] // chunk
    p = SSD_HEAD_DIM

    @pl.when(pl.program_id(1) == 0)
    def _():
        h_ref[...] = jnp.zeros_like(h_ref)

    dirs = ((xf_ref, xtf_ref, bf_ref, cf_ref, dcf_ref, drf_ref, yf_ref),
            (xb_ref, xtb_ref, bb_ref, cb_ref, dcb_ref, drb_ref, yb_ref))
    masks = (_chunk_mask(chunk, reverse=False), _chunk_mask(chunk, reverse=True))

    low_half = lax.broadcasted_iota(jnp.int32, (1, 2 * p), 1) < p
    pairs = [slice(k * 2 * p, (k + 1) * 2 * p) for k in range(SSD_HPG // 2)]

    def front(g, d, sub):
        rows = slice(sub * chunk, (sub + 1) * chunk)
        bm, cm = dirs[d][2][0, g, rows, :], dirs[d][3][0, g, rows, :]
        return dict(g=g, d=d, rows=rows, bm=bm, cm=cm, cbm=_dot_nt(cm, bm))

    def middle(st):
        g, d, rows = st["g"], st["d"], st["rows"]
        _, xt_ref, _, _, dc_ref, dr_ref, _ = dirs[d]
        base = d * (LANES // 2) + g * 2 * SSD_HPG
        dcol, xt = dc_ref[0, rows, :], xt_ref[0, g, :, rows]
        end = 0 if d == 1 else chunk - 1
        mm, ecol, xw, dec = [], [], [], []
        for r in range(SSD_HPG):
            dt_row = dr_ref[0, base + r:base + r + 1, rows]
            cs_row = dr_ref[0, base + SSD_HPG + r:base + SSD_HPG + r + 1, rows]
            cs_col = _lane_pick(dcol, base + SSD_HPG + r)
            mmat = st["cbm"] * jnp.exp2(jnp.where(masks[d], cs_col - cs_row, -jnp.inf)) * dt_row
            mm.append(mmat.astype(BF16))
            ecol.append(jnp.exp2(cs_col))
            cs_end = cs_row[:, end:end + 1]
            xw.append((xt[r * p:(r + 1) * p, :].astype(F32) * (jnp.exp2(cs_end - cs_row) * dt_row)).astype(BF16))
            dec.append(jnp.exp2(cs_end))
        st.update(mm=mm, ecol=ecol, xw=jnp.concatenate(xw, axis=0), dec=dec)

    def back(st):
        g, d, rows = st["g"], st["d"], st["rows"]
        x_ref, y_ref = dirs[d][0], dirs[d][6]
        for k, pair in enumerate(pairs):
            ch = _dot_nt(st["cm"], h_ref[d, g, pair, :].astype(BF16))
            xp = x_ref[0, g, rows, pair]
            y0 = _dot(st["mm"][2 * k], xp) + st["ecol"][2 * k] * ch
            y1 = _dot(st["mm"][2 * k + 1], xp) + st["ecol"][2 * k + 1] * ch
            y_ref[0, g, rows, pair] = jnp.where(low_half, y0, y1).astype(y_ref.dtype)
        upd = _dot(st["xw"], st["bm"])
        for r in range(SSD_HPG):
            hs = slice(r * p, (r + 1) * p)
            h_ref[d, g, hs, :] = st["dec"][r] * h_ref[d, g, hs, :] + upd[hs, :]

    prev = None
    for k in range(n_sub):
        for g in range(groups):
            for d in range(2):
                cur = front(g, d, k if d == 0 else n_sub - 1 - k)
                if prev is not None:
                    back(prev)
                middle(cur)
                prev = cur
    back(prev)


def _ssd_scan(xs_in, xs_t, bm, cm, dcol, drow, *, chunk, ctx_len):
    batch, groups, s_len, gw = xs_in.shape
    state = bm.shape[3]
    span = chunk * SSD_STEP_CHUNKS
    assert s_len % span == 0 and ctx_len % span == 0
    n_steps = s_len // span
    fwd, bwd = _scan_chunk_maps(n_steps, ctx_len // span)

    def specs(order):
        return [pl.BlockSpec((1, groups, span, gw), lambda b, i: (b, 0, order(i), 0)),
                pl.BlockSpec((1, groups, gw, span), lambda b, i: (b, 0, 0, order(i))),
                pl.BlockSpec((1, groups, span, state), lambda b, i: (b, 0, order(i), 0)),
                pl.BlockSpec((1, groups, span, state), lambda b, i: (b, 0, order(i), 0)),
                pl.BlockSpec((1, span, LANES), lambda b, i: (b, order(i), 0)),
                pl.BlockSpec((1, LANES, span), lambda b, i: (b, 0, order(i)))]

    sf, sb = specs(fwd), specs(bwd)
    out = jax.ShapeDtypeStruct((batch, groups, s_len, gw), BF16)
    args = (xs_in, xs_t, bm, cm, dcol, drow)
    return pl.pallas_call(
        functools.partial(_ssd_scan_kernel, chunk=chunk),
        grid=(batch, n_steps),
        in_specs=sf + sb,
        out_specs=[sf[0], sb[0]],
        out_shape=[out, out],
        scratch_shapes=[pltpu.VMEM((2, groups, gw, state), F32)],
        compiler_params=_params(("parallel", "arbitrary")),
        name="ssd_scan",
    )(*args, *args)


def _ssd_out_kernel(*refs):
    nt = OUT_TILES
    x_refs, mod_refs, yf_refs, yb_refs, xs_refs, z_refs = (refs[k * nt:(k + 1) * nt] for k in range(6))
    dsk_ref, ng_ref, wout_ref, g3_ref, o_ref, u_ref, un_ref = refs[6 * nt:]
    groups, tm, gw = yf_refs[0].shape[1], yf_refs[0].shape[2], yf_refs[0].shape[3]
    for t in range(nt):
        rows = slice(t * tm, (t + 1) * tm)
        ssq = None
        for g in range(groups):
            sl = slice(g * gw, (g + 1) * gw)
            y = (yf_refs[t][0, g].astype(F32) + yb_refs[t][0, g].astype(F32)
                 + dsk_ref[:, sl] * xs_refs[t][0, g].astype(F32))
            u = y * _silu(z_refs[t][0, :, sl].astype(F32))
            u_ref[:, sl] = u
            part = jnp.sum(u * u, axis=-1, keepdims=True)
            ssq = part if ssq is None else ssq + part
        inv = lax.rsqrt(ssq / (groups * gw) + EPS)
        for g in range(groups):
            sl = slice(g * gw, (g + 1) * gw)
            un_ref[rows, sl] = (u_ref[:, sl] * inv * ng_ref[:, sl]).astype(BF16)
    out = None
    for g in range(groups):
        sl = slice(g * gw, (g + 1) * gw)
        part = _dot(un_ref[:, sl], wout_ref[sl, :])
        out = part if out is None else out + part
    for t in range(nt):
        rows = slice(t * tm, (t + 1) * tm)
        o_ref[rows, :] = x_refs[t][0] + mod_refs[t][0, 5:6, :] * _rms(out[rows, :], g3_ref[...])


def _ssd_out(xs, mod, yf, yb, xs_in, z, d_skip, norm_g, w_out, g3, *, tm, tile_off, n_tiles, ctx_tiles):
    batch, _, d_model = xs.shape
    groups, gw = yf.shape[1], yf.shape[3]
    inner = groups * gw
    dsk = jnp.repeat(d_skip, SSD_HEAD_DIM).reshape(1, inner)
    ng, g3r = norm_g.reshape(1, inner), g3.reshape(1, d_model)
    wout = w_out.astype(BF16)
    tok, grouped, mod_specs, n_steps = _readout_specs(batch, tm, tile_off, n_tiles, ctx_tiles)
    rows = OUT_TILES * tm
    rep = lambda a: [a] * OUT_TILES
    grp = lambda: grouped(groups, gw)
    out = pl.pallas_call(
        _ssd_out_kernel,
        grid=(n_steps,),
        in_specs=tok(d_model) + mod_specs(d_model) + grp() + grp() + grp() + tok(inner)
                 + [_resident(dsk), _resident(ng), _resident(wout), _resident(g3r)],
        out_specs=pl.BlockSpec((rows, d_model), lambda i: (i, 0)),
        out_shape=jax.ShapeDtypeStruct((batch * n_tiles * tm, d_model), F32),
        scratch_shapes=[pltpu.VMEM((tm, inner), F32), pltpu.VMEM((rows, inner), BF16)],
        compiler_params=_params(("parallel",)),
        name="ssd_out",
    )(*rep(xs), *rep(mod), *rep(yf), *rep(yb), *rep(xs_in), *rep(z), dsk, ng, wout, g3r)
    return out.reshape(batch, n_tiles * tm, d_model)


def _forward(x, c, ctx, c_ctx, ada_w, ada_b, norm_g, ffn_w_gate, ffn_w_up, ffn_w_down,
             mlstm_w_in, mlstm_conv_w, mlstm_conv_b, mlstm_w_q, mlstm_w_k, mlstm_w_v,
             mlstm_w_gates, mlstm_b_gates, mlstm_norm_g, mlstm_skip, mlstm_w_out,
             ssd_w_in, ssd_conv_w, ssd_conv_b, ssd_dt_bias, ssd_a_log, ssd_d,
             ssd_norm_g, ssd_w_out, *, tm, ml_chunk, ssd_chunk):
    batch, seq, d_model = x.shape
    ctx_len = ctx.shape[1]
    depth = ada_w.shape[0]
    assert ctx_len == tm and seq % tm == 0 and tm % GRID_W == 0 and tm % ml_chunk == 0 and tm % ssd_chunk == 0
    assert batch < MOD_ROWS
    xs = x
    ctx_tiles = ctx_len // tm
    c_all = jnp.zeros((MOD_ROWS, d_model), F32).at[:batch].set(c).at[batch].set(c_ctx)
    ffn_weights = _stack_bf16(ffn_w_gate), _stack_bf16(ffn_w_up), _stack_bf16(ffn_w_down)
    mods = _modulation(c_all, ada_w, ada_b).reshape(depth, MOD_ROWS, N_MOD, d_model)
    for i in range(depth):
        last = i == depth - 1
        mod = mods[i]
        g = norm_g[i]
        xs = _ffn(xs, mod, g[0], g[1], ffn_weights, 2 * i,
                  j0=0, tm=tm, ctx_tiles=ctx_tiles, ctx_src=ctx if i == 0 else None)
        n_tiles = xs.shape[1] // tm
        out_off = ctx_tiles if last else 0
        out_kw = dict(tm=tm, tile_off=out_off, n_tiles=n_tiles - out_off, ctx_tiles=ctx_tiles)
        j = i // 2
        if i % 2 == 0:
            q, k, v, xc, z, gates = _mlstm_in(xs, mod, g[2], mlstm_w_in[j], mlstm_conv_w[j], mlstm_conv_b[j],
                                              mlstm_w_q[j], mlstm_w_k[j], mlstm_w_v[j], mlstm_w_gates[j],
                                              mlstm_b_gates[j], tm=tm, ctx_tiles=ctx_tiles)
            hf, hb = _mlstm_scan(q, k, v, gates, chunk=ml_chunk, ctx_len=ctx_len)
            xs = _mlstm_out(xs, mod, hf, hb, xc, z, mlstm_norm_g[j], mlstm_skip[j], mlstm_w_out[j], g[3], **out_kw)
        else:
            z, xs_in, xs_t, bm, cm, dcol, drow = _ssd_in(xs, mod, g[2], ssd_w_in[j], ssd_conv_w[j], ssd_conv_b[j],
                                                         ssd_dt_bias[j], ssd_a_log[j], tm=tm, chunk=ssd_chunk,
                                                         ctx_tiles=ctx_tiles)
            yf, yb = _ssd_scan(xs_in, xs_t, bm, cm, dcol, drow, chunk=ssd_chunk, ctx_len=ctx_len)
            xs = _ssd_out(xs, mod, yf, yb, xs_in, z, ssd_d[j], ssd_norm_g[j], ssd_w_out[j], g[3], **out_kw)
        if last:
            ctx_tiles = 0
        xs = _ffn(xs, mod, g[4], g[5], ffn_weights, 2 * i + 1, j0=6, tm=tm, ctx_tiles=ctx_tiles)
    return xs


def kernel(x, c, ctx, c_ctx, ada_w, ada_b, norm_g, ffn_w_gate, ffn_w_up, ffn_w_down, mlstm_w_in, mlstm_conv_w, mlstm_conv_b, mlstm_w_q, mlstm_w_k, mlstm_w_v, mlstm_w_gates, mlstm_b_gates, mlstm_norm_g, mlstm_skip, mlstm_w_out, ssd_w_in, ssd_conv_w, ssd_conv_b, ssd_dt_bias, ssd_a_log, ssd_d, ssd_norm_g, ssd_w_out):
    return _forward(x, c, ctx, c_ctx, ada_w, ada_b, norm_g, ffn_w_gate, ffn_w_up, ffn_w_down,
                    mlstm_w_in, mlstm_conv_w, mlstm_conv_b, mlstm_w_q, mlstm_w_k, mlstm_w_v,
                    mlstm_w_gates, mlstm_b_gates, mlstm_norm_g, mlstm_skip, mlstm_w_out,
                    ssd_w_in, ssd_conv_w, ssd_conv_b, ssd_dt_bias, ssd_a_log, ssd_d,
                    ssd_norm_g, ssd_w_out, tm=TOKEN_TILE, ml_chunk=MLSTM_CHUNK, ssd_chunk=SSD_CHUNK)
```

```python
import functools

import jax
import jax.numpy as jnp
from jax import lax
from jax.experimental import pallas as pl
from jax.experimental.pallas import tpu as pltpu

F32 = jnp.float32
BF16 = jnp.bfloat16

EPS = 1e-6
LOG2_E = 1.4426950408889634
GRID_W = 64
CONV_W = 5
N_MOD = 9
ML_HEADS = 4
ML_BLOCK = 4
SSD_GROUPS = 8
SSD_HPG = 4
SSD_HEAD_DIM = 64
SSD_STATE = 128

LANES = 128
SUBLANES = 8
MXU_DIM = 256
TOKEN_TILE = 256
MLSTM_CHUNK = 256
SSD_CHUNK = 128
SSD_STEP_CHUNKS = 2
FFN_TILES = 4
OUT_TILES = 2
IN_TILES = 2
FF_CHUNK = MXU_DIM
BD_CHUNK = MXU_DIM
MOD_ROWS = 16
VMEM_LIMIT_BYTES = 56 * 1024 * 1024


def _rms(x, g):
    return x * lax.rsqrt(jnp.mean(x * x, axis=-1, keepdims=True) + EPS) * g


def _silu(x):
    return x * jax.nn.sigmoid(x)


def _softplus(x):
    return jnp.maximum(x, 0.0) + jnp.log1p(jnp.exp(-jnp.abs(x)))


def _log_sigmoid(x):
    return -_softplus(-x)


def _dot(a, b):
    return jnp.dot(a, b, preferred_element_type=F32)


def _dot_nt(a, b):
    return lax.dot_general(a, b, (((1,), (1,)), ((), ())), preferred_element_type=F32)


def _dot_tn(a, b):
    return lax.dot_general(a, b, (((0,), (0,)), ((), ())), preferred_element_type=F32)


def _resident(arr):
    nd = arr.ndim
    return pl.BlockSpec(arr.shape, lambda *_: (0,) * nd, pipeline_mode=pl.Buffered(1))


def _params(sem):
    return pltpu.CompilerParams(dimension_semantics=sem, vmem_limit_bytes=VMEM_LIMIT_BYTES)


def _tile_specs(batch, tm, tile_off, ctx_tiles):
    def tok(width):
        return pl.BlockSpec((1, tm, width), lambda b, j: (b, j + tile_off, 0))

    def out(width):
        return pl.BlockSpec((1, tm, width), lambda b, j: (b, j, 0))

    def mod_spec(d_model):
        return pl.BlockSpec((1, N_MOD, d_model),
                            lambda b, j: (jnp.where(j + tile_off < ctx_tiles, batch, b), 0, 0))

    return tok, out, mod_spec


def _mod_kernel(c_ref, w_ref, b_ref, o_ref):
    sc = _silu(c_ref[...])
    o_ref[...] = jnp.dot(sc, w_ref[...], preferred_element_type=F32,
                         precision=lax.Precision.HIGHEST) + b_ref[...]


def _modulation(c_all, w, b):
    depth, d_model, n_out = w.shape
    rows = c_all.shape[0]
    return pl.pallas_call(
        _mod_kernel,
        grid=(depth, n_out // d_model),
        in_specs=[pl.BlockSpec(c_all.shape, lambda l, n: (0, 0)),
                  pl.BlockSpec((None, d_model, d_model), lambda l, n: (l, 0, n)),
                  pl.BlockSpec((None, 1, d_model), lambda l, n: (l, 0, n))],
        out_specs=pl.BlockSpec((None, rows, d_model), lambda l, n: (l, 0, n)),
        out_shape=jax.ShapeDtypeStruct((depth, rows, n_out), F32),
        compiler_params=_params(("arbitrary", "arbitrary")),
        name="modulation",
    )(c_all, w, b.reshape(depth, 1, n_out))


def _cast_kernel(w_ref, o_ref):
    o_ref[...] = w_ref[...].astype(o_ref.dtype)


def _stack_bf16(w):
    w = w.reshape((-1,) + w.shape[2:])
    n, rows, cols = w.shape
    parts = 4
    assert rows % (parts * 2 * SUBLANES) == 0
    spec = pl.BlockSpec((None, rows // parts, cols), lambda i, r: (i, r, 0))
    return pl.pallas_call(
        _cast_kernel,
        grid=(n, parts),
        in_specs=[spec],
        out_specs=spec,
        out_shape=jax.ShapeDtypeStruct(w.shape, BF16),
        compiler_params=_params(("parallel", "parallel")),
        name="cast_bf16",
    )(w)


def _ffn_kernel(*refs, j0, tiles_per_row, ctx_tiles, split_ctx):
    n_x = FFN_TILES * (2 if split_ctx else 1)
    x_refs, mod_refs = refs[:n_x], refs[n_x:n_x + FFN_TILES]
    gpre_ref, gpost_ref, wg_ref, wu_ref, wd_ref, o_ref, h_ref, acc_ref = refs[n_x + FFN_TILES:]
    tm = x_refs[0].shape[0]

    def tile_input(t):
        if not split_ctx:
            return x_refs[t][...]
        j = (pl.program_id(0) * FFN_TILES + t) % tiles_per_row
        return jnp.where(j < ctx_tiles, x_refs[FFN_TILES + t][...], x_refs[t][...])

    for t, mod_ref in enumerate(mod_refs):
        rows = slice(t * tm, (t + 1) * tm)
        shift, scale = mod_ref[0, j0:j0 + 1, :], mod_ref[0, j0 + 1:j0 + 2, :]
        h_ref[rows, :] = (_rms(tile_input(t), gpre_ref[...]) * (1.0 + scale) + shift).astype(BF16)
    h = h_ref[...]
    for c in range(wg_ref.shape[1] // FF_CHUNK):
        sl = slice(c * FF_CHUNK, (c + 1) * FF_CHUNK)
        a = (_silu(_dot(h, wg_ref[:, sl])) * _dot(h, wu_ref[:, sl])).astype(BF16)
        contrib = _dot(a, wd_ref[sl, :])
        if c == 0:
            acc_ref[...] = contrib
        else:
            acc_ref[...] += contrib
    for t, mod_ref in enumerate(mod_refs):
        rows = slice(t * tm, (t + 1) * tm)
        gate = mod_ref[0, j0 + 2:j0 + 3, :]
        o_ref[rows, :] = tile_input(t) + 0.5 * gate * _rms(acc_ref[rows, :], gpost_ref[...])


def _ffn(xs, mod, g_pre, g_post, weights, which, *, j0, tm, ctx_tiles, ctx_src=None):
    batch, s_len, d_model = xs.shape
    wg, wu, wd = weights
    assert wg.shape[-1] % FF_CHUNK == 0

    def stacked(w):
        return pl.BlockSpec((None,) + w.shape[1:], lambda i: (which, 0, 0), pipeline_mode=pl.Buffered(1))

    gpre, gpost = g_pre.reshape(1, d_model), g_post.reshape(1, d_model)
    split_ctx = ctx_src is not None
    lat_per_row = s_len // tm
    tiles_per_row = lat_per_row + (ctx_tiles if split_ctx else 0)
    n_steps = batch * tiles_per_row // FFN_TILES
    assert n_steps * FFN_TILES == batch * tiles_per_row
    rows = FFN_TILES * tm

    def split(i, t):
        tile = i * FFN_TILES + t
        return tile // tiles_per_row, tile % tiles_per_row

    def mod_spec(t):
        def index(i):
            b, j = split(i, t)
            return jnp.where(j < ctx_tiles, batch, b), 0, 0
        return pl.BlockSpec((1, N_MOD, d_model), index)

    def x_spec(t):
        if not split_ctx:
            return pl.BlockSpec((tm, d_model), lambda i: (i * FFN_TILES + t, 0))

        def index(i):
            b, j = split(i, t)
            return b * lat_per_row + jnp.maximum(j - ctx_tiles, 0), 0
        return pl.BlockSpec((tm, d_model), index)

    def ctx_spec(t):
        def index(i):
            b, j = split(i, t)
            return b * ctx_tiles + jnp.minimum(j, ctx_tiles - 1), 0
        return pl.BlockSpec((tm, d_model), index)

    x_specs = [x_spec(t) for t in range(FFN_TILES)]
    x_args = [xs.reshape(batch * s_len, d_model)] * FFN_TILES
    if split_ctx:
        x_specs += [ctx_spec(t) for t in range(FFN_TILES)]
        x_args += [ctx_src.reshape(-1, d_model)] * FFN_TILES
    out = pl.pallas_call(
        functools.partial(_ffn_kernel, j0=j0, tiles_per_row=tiles_per_row, ctx_tiles=ctx_tiles,
                          split_ctx=split_ctx),
        grid=(n_steps,),
        in_specs=x_specs + [mod_spec(t) for t in range(FFN_TILES)]
                 + [_resident(gpre), _resident(gpost), stacked(wg), stacked(wu), stacked(wd)],
        out_specs=pl.BlockSpec((rows, d_model), lambda i: (i, 0)),
        out_shape=jax.ShapeDtypeStruct((batch * tiles_per_row * tm, d_model), F32),
        scratch_shapes=[pltpu.VMEM((rows, d_model), BF16), pltpu.VMEM((rows, d_model), F32)],
        compiler_params=_params(("parallel",)),
        name="ffn",
    )(*x_args, *([mod] * FFN_TILES), gpre, gpost, wg, wu, wd)
    return out.reshape(batch, tiles_per_row * tm, d_model)


def _short_conv(u, cw_ref, cb_ref, sl, is_ctx):
    tm, width = u.shape
    n_grp, per_row = tm // SUBLANES, GRID_W // SUBLANES
    pad = CONV_W // 2
    u3 = u.reshape(n_grp, SUBLANES, width)
    sub = lax.broadcasted_iota(jnp.int32, (1, SUBLANES, 1), 1)
    zero = jnp.zeros((1, SUBLANES, width), F32)
    out = cb_ref[:, sl] + u3 * cw_ref[pad:pad + 1, sl]
    for j in range(CONV_W):
        d = j - pad
        if d == 0:
            continue
        rot = pltpu.roll(u3, shift=(-d) % SUBLANES, axis=1)
        step = 1 if d > 0 else -1
        parts, run = [], []
        for g in range(n_grp):
            src = g + step
            edge = (src % per_row == 0) if d > 0 else (g % per_row == 0)
            if not edge:
                run.append(src)
                continue
            if run:
                parts.append(rot[run[0]:run[-1] + 1])
                run = []
            inside = 0 <= src < n_grp
            parts.append(jnp.where(is_ctx, rot[src:src + 1], zero) if inside else zero)
        if run:
            parts.append(rot[run[0]:run[-1] + 1])
        neighbour = jnp.concatenate(parts, axis=0)
        from_neighbour = (sub >= SUBLANES - d) if d > 0 else (sub < -d)
        out = out + jnp.where(from_neighbour, neighbour, rot) * cw_ref[j:j + 1, sl]
    return out.reshape(tm, width)


def _block_diag(w):
    per = BD_CHUNK // ML_BLOCK
    n_tiles = w.shape[0] // per
    w = w.reshape(n_tiles, per, ML_BLOCK, ML_BLOCK)
    eye = jnp.eye(per, dtype=w.dtype)
    return jnp.einsum("cnij,nm->cnimj", w, eye).reshape(n_tiles, BD_CHUNK, BD_CHUNK).astype(BF16)


def _mlstm_in_kernel(*refs, tiles_per_row, ctx_tiles, kscale):
    nt = IN_TILES
    x_ref, mod_refs = refs[0], refs[1:1 + nt]
    (g_ref, win_ref, cw_ref, cb_ref, bdq_ref, bdk_ref, bdv_ref, wgq_ref, wgk_ref, wgv_ref, bg_ref,
     q_ref, k_ref, v_ref, xc_ref, z_ref, gates_ref, h_ref) = refs[1 + nt:]
    tm = x_ref.shape[0] // nt
    inner = q_ref.shape[1]
    tile_rows = [slice(t * tm, (t + 1) * tm) for t in range(nt)]
    is_ctx = [(pl.program_id(0) * nt + t) % tiles_per_row < ctx_tiles for t in range(nt)]
    for rows, mod_ref in zip(tile_rows, mod_refs):
        shift, scale = mod_ref[0, 3:4, :], mod_ref[0, 4:5, :]
        h_ref[rows, :] = (_rms(x_ref[rows, :], g_ref[...]) * (1.0 + scale) + shift).astype(BF16)
    h = h_ref[...]
    gates = bg_ref[...]
    n_chunks = inner // BD_CHUNK
    cols = lambda c, base=0: slice(base + c * BD_CHUNK, base + (c + 1) * BD_CHUNK)

    def after_conv(c, xcb, vb, gates):
        sl = cols(c)
        q = _dot(xcb, bdq_ref[c])
        k = _dot(xcb, bdk_ref[c])
        qb, kb = q.astype(BF16), k.astype(BF16)
        q_ref[:, sl] = qb
        k_ref[:, sl] = (k * kscale).astype(BF16)
        return gates + _dot(qb, wgq_ref[sl, :]) + _dot(kb, wgk_ref[sl, :]) + _dot(vb, wgv_ref[sl, :])

    xm_next = _dot(h, win_ref[:, cols(0)])
    pending = None
    for c in range(n_chunks):
        sl = cols(c)
        xm = xm_next
        z_ref[:, sl] = _silu(_dot(h, win_ref[:, cols(c, inner)])).astype(BF16)
        vb = _dot(xm.astype(BF16), bdv_ref[c]).astype(BF16)
        v_ref[:, sl] = vb
        if c + 1 < n_chunks:
            xm_next = _dot(h, win_ref[:, cols(c + 1)])
        if pending is not None:
            gates = after_conv(*pending, gates)
        xcb = jnp.concatenate([_silu(_short_conv(xm[rows, :], cw_ref, cb_ref, sl, ctx)).astype(BF16)
                               for rows, ctx in zip(tile_rows, is_ctx)], axis=0)
        xc_ref[:, sl] = xcb
        pending = (c, xcb, vb)
    gates_ref[...] = after_conv(*pending, gates)


def _chunk_sum_matrices(tm, chunk):
    t_idx = lax.broadcasted_iota(jnp.int32, (tm, tm), 0)
    u_idx = lax.broadcasted_iota(jnp.int32, (tm, tm), 1)
    same = (t_idx // chunk) == (u_idx // chunk)
    return (jnp.logical_and(same, u_idx <= t_idx).astype(BF16), jnp.logical_and(same, u_idx >= t_idx).astype(BF16))


def _mlstm_in(xs, mod, g, w_in, conv_w, conv_b, w_q, w_k, w_v, w_gates, b_gates, *, tm, ctx_tiles):
    batch, s_len, d_model = xs.shape
    inner = w_in.shape[1] // 2
    n_gates = w_gates.shape[1]
    win = w_in.astype(BF16)
    cb = conv_b.reshape(1, inner)
    bdq, bdk, bdv = _block_diag(w_q), _block_diag(w_k), _block_diag(w_v)
    wg = jnp.pad(w_gates, ((0, 0), (0, LANES - n_gates))).astype(BF16)
    wgq, wgk, wgv = wg[:inner], wg[inner:2 * inner], wg[2 * inner:]
    bg = jnp.pad(b_gates, (0, LANES - n_gates)).reshape(1, LANES)
    gg = g.reshape(1, d_model)
    tiles_per_row = s_len // tm
    n_steps = batch * tiles_per_row // IN_TILES
    assert n_steps * IN_TILES == batch * tiles_per_row
    rows = IN_TILES * tm

    def mod_spec(t):
        def index(i):
            tile = i * IN_TILES + t
            return jnp.where(tile % tiles_per_row < ctx_tiles, batch, tile // tiles_per_row), 0, 0
        return pl.BlockSpec((1, N_MOD, d_model), index)

    flat = lambda width: pl.BlockSpec((rows, width), lambda i: (i, 0))
    act = jax.ShapeDtypeStruct((batch * s_len, inner), BF16)
    kernel = functools.partial(_mlstm_in_kernel, tiles_per_row=tiles_per_row, ctx_tiles=ctx_tiles,
                               kscale=(inner // ML_HEADS) ** -0.5)
    outs = pl.pallas_call(
        kernel,
        grid=(n_steps,),
        in_specs=[flat(d_model)] + [mod_spec(t) for t in range(IN_TILES)]
                 + [_resident(gg), _resident(win), _resident(conv_w), _resident(cb), _resident(bdq), _resident(bdk),
                    _resident(bdv), _resident(wgq), _resident(wgk), _resident(wgv), _resident(bg)],
        out_specs=[flat(inner)] * 5 + [flat(LANES)],
        out_shape=[act] * 5 + [jax.ShapeDtypeStruct((batch * s_len, LANES), F32)],
        scratch_shapes=[pltpu.VMEM((rows, d_model), BF16)],
        compiler_params=_params(("parallel",)),
        name="mlstm_in",
    )(xs.reshape(batch * s_len, d_model), *([mod] * IN_TILES), gg, win, conv_w, cb, bdq, bdk, bdv, wgq, wgk, wgv, bg)
    return [o.reshape(batch, s_len, o.shape[1]) for o in outs]


def _chunk_mask(chunk, reverse):
    t_idx = lax.broadcasted_iota(jnp.int32, (chunk, chunk), 0)
    s_idx = lax.broadcasted_iota(jnp.int32, (chunk, chunk), 1)
    return s_idx >= t_idx if reverse else s_idx <= t_idx


def _lane_pick(tile, idx):
    lane = lax.broadcasted_iota(jnp.int32, (1, tile.shape[1]), 1)
    return jnp.sum(jnp.where(lane == idx, tile, 0.0), axis=1, keepdims=True)


def _mlstm_scan_kernel(qf_ref, kf_ref, vf_ref, gf_ref, qb_ref, kb_ref, vb_ref, gb_ref, tril_ref, triu_ref,
                       hf_ref, hb_ref, c_ref, n_ref, m_ref, gcol_ref, grow_ref):
    chunk = qf_ref.shape[1]
    dh = c_ref.shape[2]

    @pl.when(pl.program_id(1) == 0)
    def _():
        c_ref[...] = jnp.zeros_like(c_ref)
        n_ref[...] = jnp.zeros_like(n_ref)
        m_ref[...] = jnp.zeros_like(m_ref)

    dirs = ((qf_ref, kf_ref, vf_ref, gf_ref, hf_ref), (qb_ref, kb_ref, vb_ref, gb_ref, hb_ref))
    masks = (_chunk_mask(chunk, reverse=False), _chunk_mask(chunk, reverse=True))
    lane = lax.broadcasted_iota(jnp.int32, (1, LANES), 1)
    for d, (g_ref, sum_ref) in enumerate(((gf_ref, tril_ref), (gb_ref, triu_ref))):
        gates = g_ref[0]
        cum = _dot3(sum_ref[...], _split3(_log_sigmoid(gates)))
        tile = jnp.where(jnp.bitwise_and(lane, ML_HEADS) == 0, gates, cum)
        gcol_ref[d] = tile
        grow_ref[d] = tile.T

    ones = jnp.ones((chunk, LANES), BF16)

    def front(hd, d):
        hs = slice(hd * dh, (hd + 1) * dh)
        q = dirs[d][0][0, :, hs]
        return dict(hd=hd, d=d, hs=hs, qk=_dot_nt(q, dirs[d][1][0, :, hs]),
                    qn=_dot(q, n_ref[d, hd].astype(BF16))[:, 0:1])

    def middle(st):
        hd, d, hs = st["hd"], st["d"], st["hs"]
        k_ref = dirs[d][1]
        tri = masks[d]
        i_lane = d * 2 * ML_HEADS + hd
        f_lane = i_lane + ML_HEADS
        ig_col, b_col = _lane_pick(gcol_ref[d], i_lane), _lane_pick(gcol_ref[d], f_lane)
        ig_row, b_row = grow_ref[d, i_lane:i_lane + 1, :], grow_ref[d, f_lane:f_lane + 1, :]
        end = 0 if d == 1 else chunk - 1
        b_end = b_row[:, end:end + 1]
        m_prev = m_ref[d, hd, 0:1, 0:1]
        dm = jnp.where(tri, b_col - b_row + ig_row, -jnp.inf)
        m_inter = b_col + m_prev
        m_t = jnp.maximum(m_inter, jnp.max(dm, axis=1, keepdims=True))
        s = st["qk"] * jnp.exp(dm - m_t)
        sc = jnp.exp(m_inter - m_t)
        den = jnp.sum(s, axis=1, keepdims=True) + sc * st["qn"]
        gl = b_end - b_col + ig_col
        m_new = jnp.maximum(b_end + m_prev, jnp.max(gl, axis=0, keepdims=True))
        kw = k_ref[0, :, hs].astype(F32) * jnp.exp(gl - m_new)
        dec = jnp.exp(b_end + m_prev - m_new)
        m_ref[d, hd] = jnp.broadcast_to(m_new, m_ref.shape[2:])
        st.update(s=s.astype(BF16), sc=sc, kw=kw.astype(BF16), dec=dec,
                  inv=1.0 / jnp.maximum(jnp.abs(den), jnp.exp(-m_t)))

    def back(st):
        hd, d, hs = st["hd"], st["d"], st["hs"]
        q_ref, _, v_ref, _, o_ref = dirs[d]
        v, cmat = v_ref[0, :, hs], c_ref[d, hd]
        num = _dot(st["s"], v) + st["sc"] * _dot(q_ref[0, :, hs], cmat.astype(BF16))
        o_ref[0, :, hs] = (num * st["inv"]).astype(o_ref.dtype)
        c_ref[d, hd] = st["dec"] * cmat + _dot_tn(st["kw"], v)
        n_ref[d, hd] = st["dec"] * n_ref[d, hd] + _dot_tn(st["kw"], ones)

    prev = None
    for hd in range(ML_HEADS):
        for d in range(2):
            cur = front(hd, d)
            if prev is not None:
                back(prev)
            middle(cur)
            prev = cur
    back(prev)


def _scan_chunk_maps(n_chunks, ctx_chunks):
    def fwd(i):
        return i

    def bwd(i):
        return jnp.where(i < ctx_chunks, ctx_chunks - 1 - i, n_chunks - 1 + ctx_chunks - i)

    return fwd, bwd


def _mlstm_scan(q, k, v, gates, *, chunk, ctx_len):
    batch, s_len, inner = q.shape
    dh = inner // ML_HEADS
    n_chunks = s_len // chunk
    fwd, bwd = _scan_chunk_maps(n_chunks, ctx_len // chunk)
    tril, triu = _chunk_sum_matrices(chunk, chunk)

    def specs(order):
        head_spec = pl.BlockSpec((1, chunk, inner), lambda b, i: (b, order(i), 0))
        gate_spec = pl.BlockSpec((1, chunk, LANES), lambda b, i: (b, order(i), 0))
        return head_spec, [head_spec, head_spec, head_spec, gate_spec]

    hf_spec, in_f = specs(fwd)
    hb_spec, in_b = specs(bwd)
    out = jax.ShapeDtypeStruct((batch, s_len, inner), BF16)
    return pl.pallas_call(
        _mlstm_scan_kernel,
        grid=(batch, n_chunks),
        in_specs=in_f + in_b + [_resident(tril), _resident(triu)],
        out_specs=[hf_spec, hb_spec],
        out_shape=[out, out],
        scratch_shapes=[pltpu.VMEM((2, ML_HEADS, dh, dh), F32), pltpu.VMEM((2, ML_HEADS, dh, LANES), F32),
                        pltpu.VMEM((2, ML_HEADS, SUBLANES, LANES), F32),
                        pltpu.VMEM((2, chunk, LANES), F32), pltpu.VMEM((2, LANES, chunk), F32)],
        compiler_params=_params(("parallel", "arbitrary")),
        name="mlstm_scan",
    )(q, k, v, gates, q, k, v, gates, tril, triu)


def _mlstm_out_kernel(*refs):
    nt = OUT_TILES
    x_refs, mod_refs, hf_refs, hb_refs, xc_refs, z_refs = (refs[k * nt:(k + 1) * nt] for k in range(6))
    ng_ref, sk_ref, wout_ref, g3_ref, o_ref, u_ref = refs[6 * nt:]
    tm, inner = hf_refs[0].shape[1], hf_refs[0].shape[2]
    dh = inner // ML_HEADS
    y = None
    for hd in range(ML_HEADS):
        sl = slice(hd * dh, (hd + 1) * dh)
        for t in range(nt):
            hs = hf_refs[t][0, :, sl].astype(F32) + hb_refs[t][0, :, sl].astype(F32)
            cen = hs - jnp.mean(hs, axis=-1, keepdims=True)
            hn = cen * lax.rsqrt(jnp.mean(cen * cen, axis=-1, keepdims=True) + EPS)
            u = hn * ng_ref[:, sl] + sk_ref[:, sl] * xc_refs[t][0, :, sl].astype(F32)
            u_ref[t * tm:(t + 1) * tm, sl] = (u * z_refs[t][0, :, sl].astype(F32)).astype(BF16)
        part = _dot(u_ref[:, sl], wout_ref[sl, :])
        y = part if y is None else y + part
    for t in range(nt):
        rows = slice(t * tm, (t + 1) * tm)
        o_ref[rows, :] = x_refs[t][0] + mod_refs[t][0, 5:6, :] * _rms(y[rows, :], g3_ref[...])


def _readout_specs(batch, tm, tile_off, n_tiles, ctx_tiles):
    assert (batch * n_tiles) % OUT_TILES == 0

    def where(i, t):
        tile = i * OUT_TILES + t
        return tile // n_tiles, tile % n_tiles + tile_off

    def per_tile(make):
        return [make(t) for t in range(OUT_TILES)]

    def tok(width):
        return per_tile(lambda t: pl.BlockSpec((1, tm, width), lambda i: where(i, t) + (0,)))

    def grouped(groups, width):
        return per_tile(lambda t: pl.BlockSpec((1, groups, tm, width),
                                               lambda i: (where(i, t)[0], 0, where(i, t)[1], 0)))

    def mod(d_model):
        return per_tile(lambda t: pl.BlockSpec(
            (1, N_MOD, d_model), lambda i: (jnp.where(where(i, t)[1] < ctx_tiles, batch, where(i, t)[0]), 0, 0)))

    return tok, grouped, mod, batch * n_tiles // OUT_TILES


def _mlstm_out(xs, mod, hf, hb, xc, z, norm_g, skip, w_out, g3, *, tm, tile_off, n_tiles, ctx_tiles):
    batch, _, d_model = xs.shape
    inner = hf.shape[2]
    ng, sk, g3r = norm_g.reshape(1, inner), skip.reshape(1, inner), g3.reshape(1, d_model)
    wout = w_out.astype(BF16)
    tok, _, mod_specs, n_steps = _readout_specs(batch, tm, tile_off, n_tiles, ctx_tiles)
    rows = OUT_TILES * tm
    rep = lambda a: [a] * OUT_TILES
    out = pl.pallas_call(
        _mlstm_out_kernel,
        grid=(n_steps,),
        in_specs=tok(d_model) + mod_specs(d_model) + tok(inner) + tok(inner) + tok(inner) + tok(inner)
                 + [_resident(ng), _resident(sk), _resident(wout), _resident(g3r)],
        out_specs=pl.BlockSpec((rows, d_model), lambda i: (i, 0)),
        out_shape=jax.ShapeDtypeStruct((batch * n_tiles * tm, d_model), F32),
        scratch_shapes=[pltpu.VMEM((rows, inner), BF16)],
        compiler_params=_params(("parallel",)),
        name="mlstm_out",
    )(*rep(xs), *rep(mod), *rep(hf), *rep(hb), *rep(xc), *rep(z), ng, sk, wout, g3r)
    return out.reshape(batch, n_tiles * tm, d_model)


def _split3(x):
    hi = x.astype(BF16)
    r1 = x - hi.astype(F32)
    mid = r1.astype(BF16)
    lo = (r1 - mid.astype(F32)).astype(BF16)
    return hi, mid, lo


def _dot3(a, pieces):
    return _dot(a, pieces[0]) + _dot(a, pieces[1]) + _dot(a, pieces[2])


def _ssd_lane_source():
    src = []
    for d in range(2):
        for g in range(SSD_GROUPS):
            for _ in range(2):
                for r in range(SSD_HPG):
                    src.append(d * SSD_GROUPS * SSD_HPG + g * SSD_HPG + r)
    return jnp.array(src, jnp.int32)


def _ssd_in_kernel(x_ref, mod_ref, g_ref, wz_ref, wx_ref, wdt_ref, cw_ref, cb_ref, dtb_ref, alog_ref,
                   tril_ref, triu_ref,
                   z_ref, xs_ref, xst_ref, bm_ref, cm_ref, dcol_ref, drow_ref, ua_ref, ub_ref, *, ctx_tiles):
    tm = x_ref.shape[1]
    groups = xs_ref.shape[1]
    is_ctx = pl.program_id(1) < ctx_tiles
    shift, scale = mod_ref[0, 3:4, :], mod_ref[0, 4:5, :]
    h = (_rms(x_ref[0], g_ref[...]) * (1.0 + scale) + shift).astype(BF16)
    half = BD_CHUNK // 2
    n_z, n_x = wz_ref.shape[1] // BD_CHUNK, wx_ref.shape[1] // BD_CHUNK
    cols = lambda c: slice(c * BD_CHUNK, (c + 1) * BD_CHUNK)
    dyn0 = jnp.minimum(pl.program_id(1), 0)
    stage = (ua_ref, ub_ref)
    stage[0][dyn0] = _dot(h, wx_ref[:, cols(0)])
    for c in range(n_x):
        sl = cols(c)
        for cz in range(c * n_z // n_x, (c + 1) * n_z // n_x):
            z_ref[0, :, cols(cz)] = _dot(h, wz_ref[:, cols(cz)]).astype(BF16)
        if c + 1 < n_x:
            stage[(c + 1) % 2][dyn0] = _dot(h, wx_ref[:, cols(c + 1)])
        xbc = _silu(_short_conv(stage[c % 2][dyn0], cw_ref, cb_ref, sl, is_ctx))
        if c < groups:
            xs_ref[0, c] = xbc.astype(BF16)
            xst_ref[0, c] = xbc.astype(BF16).T
        else:
            gi = 2 * (c - groups)
            ref, gi = (bm_ref, gi) if gi < groups else (cm_ref, gi - groups)
            ref[0, gi] = xbc[:, :half].astype(BF16)
            ref[0, gi + 1] = xbc[:, half:].astype(BF16)
    dt = _softplus(_dot(h, wdt_ref[...]) + dtb_ref[...])
    la = _split3(dt * (-LOG2_E * jnp.exp(alog_ref[...])))
    lane = lax.broadcasted_iota(jnp.int32, (1, LANES), 1)
    cs = jnp.where(lane < LANES // 2, _dot3(tril_ref[...], la), _dot3(triu_ref[...], la))
    tile = jnp.where(jnp.bitwise_and(lane, SSD_HPG) == 0, dt, cs)
    dcol_ref[0] = tile
    drow_ref[0] = tile.T


def _ssd_in(xs, mod, g, w_in, conv_w, conv_b, dt_bias, a_log, *, tm, chunk, ctx_tiles):
    batch, s_len, d_model = xs.shape
    groups, state = SSD_GROUPS, SSD_STATE
    gw = SSD_HPG * SSD_HEAD_DIM
    gn = groups * state
    conv_dim = conv_w.shape[1]
    inner = conv_dim - 2 * gn
    assert gw == BD_CHUNK and 2 * state == BD_CHUNK and inner == groups * gw
    assert 4 * dt_bias.size == 2 * LANES
    wz = w_in[:, :inner].astype(BF16)
    wx = w_in[:, inner:inner + conv_dim].astype(BF16)
    src = _ssd_lane_source()
    wdt = w_in[:, inner + conv_dim:][:, src].astype(BF16)
    dtb = dt_bias.reshape(-1)[src].reshape(1, LANES)
    alog = a_log.reshape(-1)[src].reshape(1, LANES)
    tril, triu = _chunk_sum_matrices(tm, chunk)
    cb = conv_b.reshape(1, conv_dim)
    gg = g.reshape(1, d_model)
    tok, out, mod_spec = _tile_specs(batch, tm, 0, ctx_tiles)
    kernel = functools.partial(_ssd_in_kernel, ctx_tiles=ctx_tiles)
    grp = lambda w: pl.BlockSpec((1, groups, tm, w), lambda b, j: (b, 0, j, 0))
    return pl.pallas_call(
        kernel,
        grid=(batch, s_len // tm),
        in_specs=[tok(d_model), mod_spec(d_model), _resident(gg), _resident(wz), _resident(wx), _resident(wdt),
                  _resident(conv_w), _resident(cb), _resident(dtb), _resident(alog), _resident(tril), _resident(triu)],
        out_specs=[out(inner), grp(gw), pl.BlockSpec((1, groups, gw, tm), lambda b, j: (b, 0, 0, j)),
                   grp(state), grp(state), out(LANES), pl.BlockSpec((1, LANES, tm), lambda b, j: (b, 0, j))],
        out_shape=[jax.ShapeDtypeStruct((batch, s_len, inner), BF16),
                   jax.ShapeDtypeStruct((batch, groups, s_len, gw), BF16),
                   jax.ShapeDtypeStruct((batch, groups, gw, s_len), BF16),
                   jax.ShapeDtypeStruct((batch, groups, s_len, state), BF16),
                   jax.ShapeDtypeStruct((batch, groups, s_len, state), BF16),
                   jax.ShapeDtypeStruct((batch, s_len, LANES), F32),
                   jax.ShapeDtypeStruct((batch, LANES, s_len), F32)],
        scratch_shapes=[pltpu.VMEM((1, tm, BD_CHUNK), F32), pltpu.VMEM((1, tm, BD_CHUNK), F32)],
        compiler_params=_params(("parallel", "parallel")),
        name="ssd_in",
    )(xs, mod, gg, wz, wx, wdt, conv_w, cb, dtb, alog, tril, triu)


def _ssd_scan_kernel(xf_ref, xtf_ref, bf_ref, cf_ref, dcf_ref, drf_ref,
                     xb_ref, xtb_ref, bb_ref, cb_ref, dcb_ref, drb_ref,
                     yf_ref, yb_ref, h_ref, *, chunk):
    groups, n_sub = xf_ref.shape[1], xf_ref.shape[2] // chunk
    p = SSD_HEAD_DIM

    @pl.when(pl.program_id(1) == 0)
    def _():
        h_ref[...] = jnp.zeros_like(h_ref)

    dirs = ((xf_ref, xtf_ref, bf_ref, cf_ref, dcf_ref, drf_ref, yf_ref),
            (xb_ref, xtb_ref, bb_ref, cb_ref, dcb_ref, drb_ref, yb_ref))
    masks = (_chunk_mask(chunk, reverse=False), _chunk_mask(chunk, reverse=True))

    low_half = lax.broadcasted_iota(jnp.int32, (1, 2 * p), 1) < p
    pairs = [slice(k * 2 * p, (k + 1) * 2 * p) for k in range(SSD_HPG // 2)]

    def front(g, d, sub):
        rows = slice(sub * chunk, (sub + 1) * chunk)
        bm, cm = dirs[d][2][0, g, rows, :], dirs[d][3][0, g, rows, :]
        return dict(g=g, d=d, rows=rows, bm=bm, cm=cm, cbm=_dot_nt(cm, bm))

    def middle(st):
        g, d, rows = st["g"], st["d"], st["rows"]
        _, xt_ref, _, _, dc_ref, dr_ref, _ = dirs[d]
        base = d * (LANES // 2) + g * 2 * SSD_HPG
        dcol, xt = dc_ref[0, rows, :], xt_ref[0, g, :, rows]
        end = 0 if d == 1 else chunk - 1
        mm, ecol, xw, dec = [], [], [], []
        for r in range(SSD_HPG):
            dt_row = dr_ref[0, base + r:base + r + 1, rows]
            cs_row = dr_ref[0, base + SSD_HPG + r:base + SSD_HPG + r + 1, rows]
            cs_col = _lane_pick(dcol, base + SSD_HPG + r)
            mmat = st["cbm"] * jnp.exp2(jnp.where(masks[d], cs_col - cs_row, -jnp.inf)) * dt_row
            mm.append(mmat.astype(BF16))
            ecol.append(jnp.exp2(cs_col))
            cs_end = cs_row[:, end:end + 1]
            xw.append((xt[r * p:(r + 1) * p, :].astype(F32) * (jnp.exp2(cs_end - cs_row) * dt_row)).astype(BF16))
            dec.append(jnp.exp2(cs_end))
        st.update(mm=mm, ecol=ecol, xw=jnp.concatenate(xw, axis=0), dec=dec)

    def back(st):
        g, d, rows = st["g"], st["d"], st["rows"]
        x_ref, y_ref = dirs[d][0], dirs[d][6]
        for k, pair in enumerate(pairs):
            ch = _dot_nt(st["cm"], h_ref[d, g, pair, :].astype(BF16))
            xp = x_ref[0, g, rows, pair]
            y0 = _dot(st["mm"][2 * k], xp) + st["ecol"][2 * k] * ch
            y1 = _dot(st["mm"][2 * k + 1], xp) + st["ecol"][2 * k + 1] * ch
            y_ref[0, g, rows, pair] = jnp.where(low_half, y0, y1).astype(y_ref.dtype)
        upd = _dot(st["xw"], st["bm"])
        for r in range(SSD_HPG):
            hs = slice(r * p, (r + 1) * p)
            h_ref[d, g, hs, :] = st["dec"][r] * h_ref[d, g, hs, :] + upd[hs, :]

    prev = None
    for k in range(n_sub):
        for g in range(groups):
            for d in range(2):
                cur = front(g, d, k if d == 0 else n_sub - 1 - k)
                if prev is not None:
                    back(prev)
                middle(cur)
                prev = cur
    back(prev)


def _ssd_scan(xs_in, xs_t, bm, cm, dcol, drow, *, chunk, ctx_len):
    batch, groups, s_len, gw = xs_in.shape
    state = bm.shape[3]
    span = chunk * SSD_STEP_CHUNKS
    assert s_len % span == 0 and ctx_len % span == 0
    n_steps = s_len // span
    fwd, bwd = _scan_chunk_maps(n_steps, ctx_len // span)

    def specs(order):
        return [pl.BlockSpec((1, groups, span, gw), lambda b, i: (b, 0, order(i), 0)),
                pl.BlockSpec((1, groups, gw, span), lambda b, i: (b, 0, 0, order(i))),
                pl.BlockSpec((1, groups, span, state), lambda b, i: (b, 0, order(i), 0)),
                pl.BlockSpec((1, groups, span, state), lambda b, i: (b, 0, order(i), 0)),
                pl.BlockSpec((1, span, LANES), lambda b, i: (b, order(i), 0)),
                pl.BlockSpec((1, LANES, span), lambda b, i: (b, 0, order(i)))]

    sf, sb = specs(fwd), specs(bwd)
    out = jax.ShapeDtypeStruct((batch, groups, s_len, gw), BF16)
    args = (xs_in, xs_t, bm, cm, dcol, drow)
    return pl.pallas_call(
        functools.partial(_ssd_scan_kernel, chunk=chunk),
        grid=(batch, n_steps),
        in_specs=sf + sb,
        out_specs=[sf[0], sb[0]],
        out_shape=[out, out],
        scratch_shapes=[pltpu.VMEM((2, groups, gw, state), F32)],
        compiler_params=_params(("parallel", "arbitrary")),
        name="ssd_scan",
    )(*args, *args)


def _ssd_out_kernel(*refs):
    nt = OUT_TILES
    x_refs, mod_refs, yf_refs, yb_refs, xs_refs, z_refs = (refs[k * nt:(k + 1) * nt] for k in range(6))
    dsk_ref, ng_ref, wout_ref, g3_ref, o_ref, u_ref, un_ref = refs[6 * nt:]
    groups, tm, gw = yf_refs[0].shape[1], yf_refs[0].shape[2], yf_refs[0].shape[3]
    for t in range(nt):
        rows = slice(t * tm, (t + 1) * tm)
        ssq = None
        for g in range(groups):
            sl = slice(g * gw, (g + 1) * gw)
            y = (yf_refs[t][0, g].astype(F32) + yb_refs[t][0, g].astype(F32)
                 + dsk_ref[:, sl] * xs_refs[t][0, g].astype(F32))
            u = y * _silu(z_refs[t][0, :, sl].astype(F32))
            u_ref[:, sl] = u
            part = jnp.sum(u * u, axis=-1, keepdims=True)
            ssq = part if ssq is None else ssq + part
        inv = lax.rsqrt(ssq / (groups * gw) + EPS)
        for g in range(groups):
            sl = slice(g * gw, (g + 1) * gw)
            un_ref[rows, sl] = (u_ref[:, sl] * inv * ng_ref[:, sl]).astype(BF16)
    out = None
    for g in range(groups):
        sl = slice(g * gw, (g + 1) * gw)
        part = _dot(un_ref[:, sl], wout_ref[sl, :])
        out = part if out is None else out + part
    for t in range(nt):
        rows = slice(t * tm, (t + 1) * tm)
        o_ref[rows, :] = x_refs[t][0] + mod_refs[t][0, 5:6, :] * _rms(out[rows, :], g3_ref[...])


def _ssd_out(xs, mod, yf, yb, xs_in, z, d_skip, norm_g, w_out, g3, *, tm, tile_off, n_tiles, ctx_tiles):
    batch, _, d_model = xs.shape
    groups, gw = yf.shape[1], yf.shape[3]
    inner = groups * gw
    dsk = jnp.repeat(d_skip, SSD_HEAD_DIM).reshape(1, inner)
    ng, g3r = norm_g.reshape(1, inner), g3.reshape(1, d_model)
    wout = w_out.astype(BF16)
    tok, grouped, mod_specs, n_steps = _readout_specs(batch, tm, tile_off, n_tiles, ctx_tiles)
    rows = OUT_TILES * tm
    rep = lambda a: [a] * OUT_TILES
    grp = lambda: grouped(groups, gw)
    out = pl.pallas_call(
        _ssd_out_kernel,
        grid=(n_steps,),
        in_specs=tok(d_model) + mod_specs(d_model) + grp() + grp() + grp() + tok(inner)
                 + [_resident(dsk), _resident(ng), _resident(wout), _resident(g3r)],
        out_specs=pl.BlockSpec((rows, d_model), lambda i: (i, 0)),
        out_shape=jax.ShapeDtypeStruct((batch * n_tiles * tm, d_model), F32),
        scratch_shapes=[pltpu.VMEM((tm, inner), F32), pltpu.VMEM((rows, inner), BF16)],
        compiler_params=_params(("parallel",)),
        name="ssd_out",
    )(*rep(xs), *rep(mod), *rep(yf), *rep(yb), *rep(xs_in), *rep(z), dsk, ng, wout, g3r)
    return out.reshape(batch, n_tiles * tm, d_model)


def _forward(x, c, ctx, c_ctx, ada_w, ada_b, norm_g, ffn_w_gate, ffn_w_up, ffn_w_down,
             mlstm_w_in, mlstm_conv_w, mlstm_conv_b, mlstm_w_q, mlstm_w_k, mlstm_w_v,
             mlstm_w_gates, mlstm_b_gates, mlstm_norm_g, mlstm_skip, mlstm_w_out,
             ssd_w_in, ssd_conv_w, ssd_conv_b, ssd_dt_bias, ssd_a_log, ssd_d,
             ssd_norm_g, ssd_w_out, *, tm, ml_chunk, ssd_chunk):
    batch, seq, d_model = x.shape
    ctx_len = ctx.shape[1]
    depth = ada_w.shape[0]
    assert ctx_len == tm and seq % tm == 0 and tm % GRID_W == 0 and tm % ml_chunk == 0 and tm % ssd_chunk == 0
    assert batch < MOD_ROWS
    xs = x
    ctx_tiles = ctx_len // tm
    c_all = jnp.zeros((MOD_ROWS, d_model), F32).at[:batch].set(c).at[batch].set(c_ctx)
    ffn_weights = _stack_bf16(ffn_w_gate), _stack_bf16(ffn_w_up), _stack_bf16(ffn_w_down)
    mods = _modulation(c_all, ada_w, ada_b).reshape(depth, MOD_ROWS, N_MOD, d_model)
    for i in range(depth):
        last = i == depth - 1
        mod = mods[i]
        g = norm_g[i]
        xs = _ffn(xs, mod, g[0], g[1], ffn_weights, 2 * i,
                  j0=0, tm=tm, ctx_tiles=ctx_tiles, ctx_src=ctx if i == 0 else None)
        n_tiles = xs.shape[1] // tm
        out_off = ctx_tiles if last else 0
        out_kw = dict(tm=tm, tile_off=out_off, n_tiles=n_tiles - out_off, ctx_tiles=ctx_tiles)
        j = i // 2
        if i % 2 == 0:
            q, k, v, xc, z, gates = _mlstm_in(xs, mod, g[2], mlstm_w_in[j], mlstm_conv_w[j], mlstm_conv_b[j],
                                              mlstm_w_q[j], mlstm_w_k[j], mlstm_w_v[j], mlstm_w_gates[j],
                                              mlstm_b_gates[j], tm=tm, ctx_tiles=ctx_tiles)
            hf, hb = _mlstm_scan(q, k, v, gates, chunk=ml_chunk, ctx_len=ctx_len)
            xs = _mlstm_out(xs, mod, hf, hb, xc, z, mlstm_norm_g[j], mlstm_skip[j], mlstm_w_out[j], g[3], **out_kw)
        else:
            z, xs_in, xs_t, bm, cm, dcol, drow = _ssd_in(xs, mod, g[2], ssd_w_in[j], ssd_conv_w[j], ssd_conv_b[j],
                                                         ssd_dt_bias[j], ssd_a_log[j], tm=tm, chunk=ssd_chunk,
                                                         ctx_tiles=ctx_tiles)
            yf, yb = _ssd_scan(xs_in, xs_t, bm, cm, dcol, drow, chunk=ssd_chunk, ctx_len=ctx_len)
            xs = _ssd_out(xs, mod, yf, yb, xs_in, z, ssd_d[j], ssd_norm_g[j], ssd_w_out[j], g[3], **out_kw)
        if last:
            ctx_tiles = 0
        xs = _ffn(xs, mod, g[4], g[5], ffn_weights, 2 * i + 1, j0=6, tm=tm, ctx_tiles=ctx_tiles)
    return xs


def kernel(x, c, ctx, c_ctx, ada_w, ada_b, norm_g, ffn_w_gate, ffn_w_up, ffn_w_down, mlstm_w_in, mlstm_conv_w, mlstm_conv_b, mlstm_w_q, mlstm_w_k, mlstm_w_v, mlstm_w_gates, mlstm_b_gates, mlstm_norm_g, mlstm_skip, mlstm_w_out, ssd_w_in, ssd_conv_w, ssd_conv_b, ssd_dt_bias, ssd_a_log, ssd_d, ssd_norm_g, ssd_w_out):
    return _forward(x, c, ctx, c_ctx, ada_w, ada_b, norm_g, ffn_w_gate, ffn_w_up, ffn_w_down,
                    mlstm_w_in, mlstm_conv_w, mlstm_conv_b, mlstm_w_q, mlstm_w_k, mlstm_w_v,
                    mlstm_w_gates, mlstm_b_gates, mlstm_norm_g, mlstm_skip, mlstm_w_out,
                    ssd_w_in, ssd_conv_w, ssd_conv_b, ssd_dt_bias, ssd_a_log, ssd_d,
                    ssd_norm_g, ssd_w_out, tm=TOKEN_TILE, ml_chunk=MLSTM_CHUNK, ssd_chunk=SSD_CHUNK)
```

```python
import functools

import jax
import jax.numpy as jnp
from jax import lax
from jax.experimental import pallas as pl
from jax.experimental.pallas import tpu as pltpu

F32 = jnp.float32
BF16 = jnp.bfloat16

EPS = 1e-6
LOG2_E = 1.4426950408889634
GRID_W = 64
CONV_W = 5
N_MOD = 9
ML_HEADS = 4
ML_BLOCK = 4
SSD_GROUPS = 8
SSD_HPG = 4
SSD_HEAD_DIM = 64
SSD_STATE = 128

LANES = 128
SUBLANES = 8
MXU_DIM = 256
TOKEN_TILE = 256
MLSTM_CHUNK = 256
SSD_CHUNK = 128
SSD_STEP_CHUNKS = 2
FFN_TILES = 4
OUT_TILES = 2
IN_TILES = 2
FF_CHUNK = MXU_DIM
BD_CHUNK = MXU_DIM
MOD_ROWS = 16
VMEM_LIMIT_BYTES = 56 * 1024 * 1024


def _rms(x, g):
    return x * lax.rsqrt(jnp.mean(x * x, axis=-1, keepdims=True) + EPS) * g


def _silu(x):
    return x * jax.nn.sigmoid(x)


def _softplus(x):
    return jnp.maximum(x, 0.0) + jnp.log1p(jnp.exp(-jnp.abs(x)))


def _log_sigmoid(x):
    return -_softplus(-x)


def _dot(a, b):
    return jnp.dot(a, b, preferred_element_type=F32)


def _dot_nt(a, b):
    return lax.dot_general(a, b, (((1,), (1,)), ((), ())), preferred_element_type=F32)


def _dot_tn(a, b):
    return lax.dot_general(a, b, (((0,), (0,)), ((), ())), preferred_element_type=F32)


def _resident(arr):
    nd = arr.ndim
    return pl.BlockSpec(arr.shape, lambda *_: (0,) * nd, pipeline_mode=pl.Buffered(1))


def _params(sem):
    return pltpu.CompilerParams(dimension_semantics=sem, vmem_limit_bytes=VMEM_LIMIT_BYTES)


def _tile_specs(batch, tm, tile_off, ctx_tiles):
    def tok(width):
        return pl.BlockSpec((1, tm, width), lambda b, j: (b, j + tile_off, 0))

    def out(width):
        return pl.BlockSpec((1, tm, width), lambda b, j: (b, j, 0))

    def mod_spec(d_model):
        return pl.BlockSpec((1, N_MOD, d_model),
                            lambda b, j: (jnp.where(j + tile_off < ctx_tiles, batch, b), 0, 0))

    return tok, out, mod_spec


def _mod_kernel(c_ref, w_ref, b_ref, o_ref):
    sc = _silu(c_ref[...])
    o_ref[...] = jnp.dot(sc, w_ref[...], preferred_element_type=F32,
                         precision=lax.Precision.HIGHEST) + b_ref[...]


def _modulation(c_all, w, b):
    depth, d_model, n_out = w.shape
    rows = c_all.shape[0]
    return pl.pallas_call(
        _mod_kernel,
        grid=(depth, n_out // d_model),
        in_specs=[pl.BlockSpec(c_all.shape, lambda l, n: (0, 0)),
                  pl.BlockSpec((None, d_model, d_model), lambda l, n: (l, 0, n)),
                  pl.BlockSpec((None, 1, d_model), lambda l, n: (l, 0, n))],
        out_specs=pl.BlockSpec((None, rows, d_model), lambda l, n: (l, 0, n)),
        out_shape=jax.ShapeDtypeStruct((depth, rows, n_out), F32),
        compiler_params=_params(("arbitrary", "arbitrary")),
        name="modulation",
    )(c_all, w, b.reshape(depth, 1, n_out))


def _cast_kernel(w_ref, o_ref):
    o_ref[...] = w_ref[...].astype(o_ref.dtype)


def _stack_bf16(w):
    w = w.reshape((-1,) + w.shape[2:])
    n, rows, cols = w.shape
    parts = 4
    assert rows % (parts * 2 * SUBLANES) == 0
    spec = pl.BlockSpec((None, rows // parts, cols), lambda i, r: (i, r, 0))
    return pl.pallas_call(
        _cast_kernel,
        grid=(n, parts),
        in_specs=[spec],
        out_specs=spec,
        out_shape=jax.ShapeDtypeStruct(w.shape, BF16),
        compiler_params=_params(("parallel", "parallel")),
        name="cast_bf16",
    )(w)


def _ffn_kernel(*refs, j0, tiles_per_row, ctx_tiles, split_ctx):
    n_x = FFN_TILES * (2 if split_ctx else 1)
    x_refs, mod_refs = refs[:n_x], refs[n_x:n_x + FFN_TILES]
    gpre_ref, gpost_ref, wg_ref, wu_ref, wd_ref, o_ref, h_ref, acc_ref = refs[n_x + FFN_TILES:]
    tm = x_refs[0].shape[0]

    def tile_input(t):
        if not split_ctx:
            return x_refs[t][...]
        j = (pl.program_id(0) * FFN_TILES + t) % tiles_per_row
        return jnp.where(j < ctx_tiles, x_refs[FFN_TILES + t][...], x_refs[t][...])

    for t, mod_ref in enumerate(mod_refs):
        rows = slice(t * tm, (t + 1) * tm)
        shift, scale = mod_ref[0, j0:j0 + 1, :], mod_ref[0, j0 + 1:j0 + 2, :]
        h_ref[rows, :] = (_rms(tile_input(t), gpre_ref[...]) * (1.0 + scale) + shift).astype(BF16)
    h = h_ref[...]
    for c in range(wg_ref.shape[1] // FF_CHUNK):
        sl = slice(c * FF_CHUNK, (c + 1) * FF_CHUNK)
        a = (_silu(_dot(h, wg_ref[:, sl])) * _dot(h, wu_ref[:, sl])).astype(BF16)
        contrib = _dot(a, wd_ref[sl, :])
        if c == 0:
            acc_ref[...] = contrib
        else:
            acc_ref[...] += contrib
    for t, mod_ref in enumerate(mod_refs):
        rows = slice(t * tm, (t + 1) * tm)
        gate = mod_ref[0, j0 + 2:j0 + 3, :]
        o_ref[rows, :] = tile_input(t) + 0.5 * gate * _rms(acc_ref[rows, :], gpost_ref[...])


def _ffn(xs, mod, g_pre, g_post, weights, which, *, j0, tm, ctx_tiles, ctx_src=None):
    batch, s_len, d_model = xs.shape
    wg, wu, wd = weights
    assert wg.shape[-1] % FF_CHUNK == 0

    def stacked(w):
        return pl.BlockSpec((None,) + w.shape[1:], lambda i: (which, 0, 0), pipeline_mode=pl.Buffered(1))

    gpre, gpost = g_pre.reshape(1, d_model), g_post.reshape(1, d_model)
    split_ctx = ctx_src is not None
    lat_per_row = s_len // tm
    tiles_per_row = lat_per_row + (ctx_tiles if split_ctx else 0)
    n_steps = batch * tiles_per_row // FFN_TILES
    assert n_steps * FFN_TILES == batch * tiles_per_row
    rows = FFN_TILES * tm

    def split(i, t):
        tile = i * FFN_TILES + t
        return tile // tiles_per_row, tile % tiles_per_row

    def mod_spec(t):
        def index(i):
            b, j = split(i, t)
            return jnp.where(j < ctx_tiles, batch, b), 0, 0
        return pl.BlockSpec((1, N_MOD, d_model), index)

    def x_spec(t):
        if not split_ctx:
            return pl.BlockSpec((tm, d_model), lambda i: (i * FFN_TILES + t, 0))

        def index(i):
            b, j = split(i, t)
            return b * lat_per_row + jnp.maximum(j - ctx_tiles, 0), 0
        return pl.BlockSpec((tm, d_model), index)

    def ctx_spec(t):
        def index(i):
            b, j = split(i, t)
            return b * ctx_tiles + jnp.minimum(j, ctx_tiles - 1), 0
        return pl.BlockSpec((tm, d_model), index)

    x_specs = [x_spec(t) for t in range(FFN_TILES)]
    x_args = [xs.reshape(batch * s_len, d_model)] * FFN_TILES
    if split_ctx:
        x_specs += [ctx_spec(t) for t in range(FFN_TILES)]
        x_args += [ctx_src.reshape(-1, d_model)] * FFN_TILES
    out = pl.pallas_call(
        functools.partial(_ffn_kernel, j0=j0, tiles_per_row=tiles_per_row, ctx_tiles=ctx_tiles,
                          split_ctx=split_ctx),
        grid=(n_steps,),
        in_specs=x_specs + [mod_spec(t) for t in range(FFN_TILES)]
                 + [_resident(gpre), _resident(gpost), stacked(wg), stacked(wu), stacked(wd)],
        out_specs=pl.BlockSpec((rows, d_model), lambda i: (i, 0)),
        out_shape=jax.ShapeDtypeStruct((batch * tiles_per_row * tm, d_model), F32),
        scratch_shapes=[pltpu.VMEM((rows, d_model), BF16), pltpu.VMEM((rows, d_model), F32)],
        compiler_params=_params(("parallel",)),
        name="ffn",
    )(*x_args, *([mod] * FFN_TILES), gpre, gpost, wg, wu, wd)
    return out.reshape(batch, tiles_per_row * tm, d_model)


def _short_conv(u, cw_ref, cb_ref, sl, is_ctx):
    tm, width = u.shape
    n_grp, per_row = tm // SUBLANES, GRID_W // SUBLANES
    pad = CONV_W // 2
    u3 = u.reshape(n_grp, SUBLANES, width)
    sub = lax.broadcasted_iota(jnp.int32, (1, SUBLANES, 1), 1)
    zero = jnp.zeros((1, SUBLANES, width), F32)
    out = cb_ref[:, sl] + u3 * cw_ref[pad:pad + 1, sl]
    for j in range(CONV_W):
        d = j - pad
        if d == 0:
            continue
        rot = pltpu.roll(u3, shift=(-d) % SUBLANES, axis=1)
        step = 1 if d > 0 else -1
        parts, run = [], []
        for g in range(n_grp):
            src = g + step
            edge = (src % per_row == 0) if d > 0 else (g % per_row == 0)
            if not edge:
                run.append(src)
                continue
            if run:
                parts.append(rot[run[0]:run[-1] + 1])
                run = []
            inside = 0 <= src < n_grp
            parts.append(jnp.where(is_ctx, rot[src:src + 1], zero) if inside else zero)
        if run:
            parts.append(rot[run[0]:run[-1] + 1])
        neighbour = jnp.concatenate(parts, axis=0)
        from_neighbour = (sub >= SUBLANES - d) if d > 0 else (sub < -d)
        out = out + jnp.where(from_neighbour, neighbour, rot) * cw_ref[j:j + 1, sl]
    return out.reshape(tm, width)


def _block_diag(w):
    per = BD_CHUNK // ML_BLOCK
    n_tiles = w.shape[0] // per
    w = w.reshape(n_tiles, per, ML_BLOCK, ML_BLOCK)
    eye = jnp.eye(per, dtype=w.dtype)
    return jnp.einsum("cnij,nm->cnimj", w, eye).reshape(n_tiles, BD_CHUNK, BD_CHUNK).astype(BF16)


def _mlstm_in_kernel(*refs, tiles_per_row, ctx_tiles, kscale):
    nt = IN_TILES
    x_ref, mod_refs = refs[0], refs[1:1 + nt]
    (g_ref, win_ref, cw_ref, cb_ref, bdq_ref, bdk_ref, bdv_ref, wgq_ref, wgk_ref, wgv_ref, bg_ref,
     q_ref, k_ref, v_ref, xc_ref, z_ref, gates_ref, h_ref) = refs[1 + nt:]
    tm = x_ref.shape[0] // nt
    inner = q_ref.shape[1]
    tile_rows = [slice(t * tm, (t + 1) * tm) for t in range(nt)]
    is_ctx = [(pl.program_id(0) * nt + t) % tiles_per_row < ctx_tiles for t in range(nt)]
    for rows, mod_ref in zip(tile_rows, mod_refs):
        shift, scale = mod_ref[0, 3:4, :], mod_ref[0, 4:5, :]
        h_ref[rows, :] = (_rms(x_ref[rows, :], g_ref[...]) * (1.0 + scale) + shift).astype(BF16)
    h = h_ref[...]
    gates = bg_ref[...]
    n_chunks = inner // BD_CHUNK
    cols = lambda c, base=0: slice(base + c * BD_CHUNK, base + (c + 1) * BD_CHUNK)

    def after_conv(c, xcb, vb, gates):
        sl = cols(c)
        q = _dot(xcb, bdq_ref[c])
        k = _dot(xcb, bdk_ref[c])
        qb, kb = q.astype(BF16), k.astype(BF16)
        q_ref[:, sl] = qb
        k_ref[:, sl] = (k * kscale).astype(BF16)
        return gates + _dot(qb, wgq_ref[sl, :]) + _dot(kb, wgk_ref[sl, :]) + _dot(vb, wgv_ref[sl, :])

    xm_next = _dot(h, win_ref[:, cols(0)])
    pending = None
    for c in range(n_chunks):
        sl = cols(c)
        xm = xm_next
        z_ref[:, sl] = _silu(_dot(h, win_ref[:, cols(c, inner)])).astype(BF16)
        vb = _dot(xm.astype(BF16), bdv_ref[c]).astype(BF16)
        v_ref[:, sl] = vb
        if c + 1 < n_chunks:
            xm_next = _dot(h, win_ref[:, cols(c + 1)])
        if pending is not None:
            gates = after_conv(*pending, gates)
        xcb = jnp.concatenate([_silu(_short_conv(xm[rows, :], cw_ref, cb_ref, sl, ctx)).astype(BF16)
                               for rows, ctx in zip(tile_rows, is_ctx)], axis=0)
        xc_ref[:, sl] = xcb
        pending = (c, xcb, vb)
    gates_ref[...] = after_conv(*pending, gates)


def _chunk_sum_matrices(tm, chunk):
    t_idx = lax.broadcasted_iota(jnp.int32, (tm, tm), 0)
    u_idx = lax.broadcasted_iota(jnp.int32, (tm, tm), 1)
    same = (t_idx // chunk) == (u_idx // chunk)
    return (jnp.logical_and(same, u_idx <= t_idx).astype(BF16), jnp.logical_and(same, u_idx >= t_idx).astype(BF16))


def _mlstm_in(xs, mod, g, w_in, conv_w, conv_b, w_q, w_k, w_v, w_gates, b_gates, *, tm, ctx_tiles):
    batch, s_len, d_model = xs.shape
    inner = w_in.shape[1] // 2
    n_gates = w_gates.shape[1]
    win = w_in.astype(BF16)
    cb = conv_b.reshape(1, inner)
    bdq, bdk, bdv = _block_diag(w_q), _block_diag(w_k), _block_diag(w_v)
    wg = jnp.pad(w_gates, ((0, 0), (0, LANES - n_gates))).astype(BF16)
    wgq, wgk, wgv = wg[:inner], wg[inner:2 * inner], wg[2 * inner:]
    bg = jnp.pad(b_gates, (0, LANES - n_gates)).reshape(1, LANES)
    gg = g.reshape(1, d_model)
    tiles_per_row = s_len // tm
    n_steps = batch * tiles_per_row // IN_TILES
    assert n_steps * IN_TILES == batch * tiles_per_row
    rows = IN_TILES * tm

    def mod_spec(t):
        def index(i):
            tile = i * IN_TILES + t
            return jnp.where(tile % tiles_per_row < ctx_tiles, batch, tile // tiles_per_row), 0, 0
        return pl.BlockSpec((1, N_MOD, d_model), index)

    flat = lambda width: pl.BlockSpec((rows, width), lambda i: (i, 0))
    act = jax.ShapeDtypeStruct((batch * s_len, inner), BF16)
    kernel = functools.partial(_mlstm_in_kernel, tiles_per_row=tiles_per_row, ctx_tiles=ctx_tiles,
                               kscale=(inner // ML_HEADS) ** -0.5)
    outs = pl.pallas_call(
        kernel,
        grid=(n_steps,),
        in_specs=[flat(d_model)] + [mod_spec(t) for t in range(IN_TILES)]
                 + [_resident(gg), _resident(win), _resident(conv_w), _resident(cb), _resident(bdq), _resident(bdk),
                    _resident(bdv), _resident(wgq), _resident(wgk), _resident(wgv), _resident(bg)],
        out_specs=[flat(inner)] * 5 + [flat(LANES)],
        out_shape=[act] * 5 + [jax.ShapeDtypeStruct((batch * s_len, LANES), F32)],
        scratch_shapes=[pltpu.VMEM((rows, d_model), BF16)],
        compiler_params=_params(("parallel",)),
        name="mlstm_in",
    )(xs.reshape(batch * s_len, d_model), *([mod] * IN_TILES), gg, win, conv_w, cb, bdq, bdk, bdv, wgq, wgk, wgv, bg)
    return [o.reshape(batch, s_len, o.shape[1]) for o in outs]


def _chunk_mask(chunk, reverse):
    t_idx = lax.broadcasted_iota(jnp.int32, (chunk, chunk), 0)
    s_idx = lax.broadcasted_iota(jnp.int32, (chunk, chunk), 1)
    return s_idx >= t_idx if reverse else s_idx <= t_idx


def _lane_pick(tile, idx):
    lane = lax.broadcasted_iota(jnp.int32, (1, tile.shape[1]), 1)
    return jnp.sum(jnp.where(lane == idx, tile, 0.0), axis=1, keepdims=True)


def _mlstm_scan_kernel(qf_ref, kf_ref, vf_ref, gf_ref, qb_ref, kb_ref, vb_ref, gb_ref, tril_ref, triu_ref,
                       hf_ref, hb_ref, c_ref, n_ref, m_ref, gcol_ref, grow_ref):
    chunk = qf_ref.shape[1]
    dh = c_ref.shape[2]

    @pl.when(pl.program_id(1) == 0)
    def _():
        c_ref[...] = jnp.zeros_like(c_ref)
        n_ref[...] = jnp.zeros_like(n_ref)
        m_ref[...] = jnp.zeros_like(m_ref)

    dirs = ((qf_ref, kf_ref, vf_ref, gf_ref, hf_ref), (qb_ref, kb_ref, vb_ref, gb_ref, hb_ref))
    masks = (_chunk_mask(chunk, reverse=False), _chunk_mask(chunk, reverse=True))
    lane = lax.broadcasted_iota(jnp.int32, (1, LANES), 1)
    for d, (g_ref, sum_ref) in enumerate(((gf_ref, tril_ref), (gb_ref, triu_ref))):
        gates = g_ref[0]
        cum = _dot3(sum_ref[...], _split3(_log_sigmoid(gates)))
        tile = jnp.where(jnp.bitwise_and(lane, ML_HEADS) == 0, gates, cum)
        gcol_ref[d] = tile
        grow_ref[d] = tile.T

    ones = jnp.ones((chunk, LANES), BF16)

    def front(hd, d):
        hs = slice(hd * dh, (hd + 1) * dh)
        q = dirs[d][0][0, :, hs]
        return dict(hd=hd, d=d, hs=hs, qk=_dot_nt(q, dirs[d][1][0, :, hs]),
                    qn=_dot(q, n_ref[d, hd].astype(BF16))[:, 0:1])

    def middle(st):
        hd, d, hs = st["hd"], st["d"], st["hs"]
        k_ref = dirs[d][1]
        tri = masks[d]
        i_lane = d * 2 * ML_HEADS + hd
        f_lane = i_lane + ML_HEADS
        ig_col, b_col = _lane_pick(gcol_ref[d], i_lane), _lane_pick(gcol_ref[d], f_lane)
        ig_row, b_row = grow_ref[d, i_lane:i_lane + 1, :], grow_ref[d, f_lane:f_lane + 1, :]
        end = 0 if d == 1 else chunk - 1
        b_end = b_row[:, end:end + 1]
        m_prev = m_ref[d, hd, 0:1, 0:1]
        s_parts, sc_parts, inv_parts = [], [], []
        for lo in range(0, chunk, chunk // 2):
            rows = slice(lo, lo + chunk // 2)
            dm = jnp.where(tri[rows, :], b_col[rows, :] - b_row + ig_row, -jnp.inf)
            m_inter = b_col[rows, :] + m_prev
            m_t = jnp.maximum(m_inter, jnp.max(dm, axis=1, keepdims=True))
            s_h = st["qk"][rows, :] * jnp.exp(dm - m_t)
            sc_h = jnp.exp(m_inter - m_t)
            den = jnp.sum(s_h, axis=1, keepdims=True) + sc_h * st["qn"][rows, :]
            s_parts.append(s_h.astype(BF16))
            sc_parts.append(sc_h)
            inv_parts.append(1.0 / jnp.maximum(jnp.abs(den), jnp.exp(-m_t)))
        s, sc, inv = (jnp.concatenate(p, axis=0) for p in (s_parts, sc_parts, inv_parts))
        gl = b_end - b_col + ig_col
        m_new = jnp.maximum(b_end + m_prev, jnp.max(gl, axis=0, keepdims=True))
        kw = k_ref[0, :, hs].astype(F32) * jnp.exp(gl - m_new)
        dec = jnp.exp(b_end + m_prev - m_new)
        m_ref[d, hd] = jnp.broadcast_to(m_new, m_ref.shape[2:])
        st.update(s=s, sc=sc, kw=kw.astype(BF16), dec=dec, inv=inv)

    def back(st):
        hd, d, hs = st["hd"], st["d"], st["hs"]
        q_ref, _, v_ref, _, o_ref = dirs[d]
        v, cmat = v_ref[0, :, hs], c_ref[d, hd]
        num = _dot(st["s"], v) + st["sc"] * _dot(q_ref[0, :, hs], cmat.astype(BF16))
        o_ref[0, :, hs] = (num * st["inv"]).astype(o_ref.dtype)
        c_ref[d, hd] = st["dec"] * cmat + _dot_tn(st["kw"], v)
        n_ref[d, hd] = st["dec"] * n_ref[d, hd] + _dot_tn(st["kw"], ones)

    prev = None
    for hd in range(ML_HEADS):
        for d in range(2):
            cur = front(hd, d)
            if prev is not None:
                back(prev)
            middle(cur)
            prev = cur
    back(prev)


def _scan_chunk_maps(n_chunks, ctx_chunks):
    def fwd(i):
        return i

    def bwd(i):
        return jnp.where(i < ctx_chunks, ctx_chunks - 1 - i, n_chunks - 1 + ctx_chunks - i)

    return fwd, bwd


def _mlstm_scan(q, k, v, gates, *, chunk, ctx_len):
    batch, s_len, inner = q.shape
    dh = inner // ML_HEADS
    n_chunks = s_len // chunk
    fwd, bwd = _scan_chunk_maps(n_chunks, ctx_len // chunk)
    tril, triu = _chunk_sum_matrices(chunk, chunk)

    def specs(order):
        head_spec = pl.BlockSpec((1, chunk, inner), lambda b, i: (b, order(i), 0))
        gate_spec = pl.BlockSpec((1, chunk, LANES), lambda b, i: (b, order(i), 0))
        return head_spec, [head_spec, head_spec, head_spec, gate_spec]

    hf_spec, in_f = specs(fwd)
    hb_spec, in_b = specs(bwd)
    out = jax.ShapeDtypeStruct((batch, s_len, inner), BF16)
    return pl.pallas_call(
        _mlstm_scan_kernel,
        grid=(batch, n_chunks),
        in_specs=in_f + in_b + [_resident(tril), _resident(triu)],
        out_specs=[hf_spec, hb_spec],
        out_shape=[out, out],
        scratch_shapes=[pltpu.VMEM((2, ML_HEADS, dh, dh), F32), pltpu.VMEM((2, ML_HEADS, dh, LANES), F32),
                        pltpu.VMEM((2, ML_HEADS, SUBLANES, LANES), F32),
                        pltpu.VMEM((2, chunk, LANES), F32), pltpu.VMEM((2, LANES, chunk), F32)],
        compiler_params=_params(("parallel", "arbitrary")),
        name="mlstm_scan",
    )(q, k, v, gates, q, k, v, gates, tril, triu)


def _mlstm_out_kernel(*refs):
    nt = OUT_TILES
    x_refs, mod_refs, hf_refs, hb_refs, xc_refs, z_refs = (refs[k * nt:(k + 1) * nt] for k in range(6))
    ng_ref, sk_ref, wout_ref, g3_ref, o_ref, u_ref = refs[6 * nt:]
    tm, inner = hf_refs[0].shape[1], hf_refs[0].shape[2]
    dh = inner // ML_HEADS
    y = None
    for hd in range(ML_HEADS):
        sl = slice(hd * dh, (hd + 1) * dh)
        for t in range(nt):
            hs = hf_refs[t][0, :, sl].astype(F32) + hb_refs[t][0, :, sl].astype(F32)
            cen = hs - jnp.mean(hs, axis=-1, keepdims=True)
            hn = cen * lax.rsqrt(jnp.mean(cen * cen, axis=-1, keepdims=True) + EPS)
            u = hn * ng_ref[:, sl] + sk_ref[:, sl] * xc_refs[t][0, :, sl].astype(F32)
            u_ref[t * tm:(t + 1) * tm, sl] = (u * z_refs[t][0, :, sl].astype(F32)).astype(BF16)
        part = _dot(u_ref[:, sl], wout_ref[sl, :])
        y = part if y is None else y + part
    for t in range(nt):
        rows = slice(t * tm, (t + 1) * tm)
        o_ref[rows, :] = x_refs[t][0] + mod_refs[t][0, 5:6, :] * _rms(y[rows, :], g3_ref[...])


def _readout_specs(batch, tm, tile_off, n_tiles, ctx_tiles):
    assert (batch * n_tiles) % OUT_TILES == 0

    def where(i, t):
        tile = i * OUT_TILES + t
        return tile // n_tiles, tile % n_tiles + tile_off

    def per_tile(make):
        return [make(t) for t in range(OUT_TILES)]

    def tok(width):
        return per_tile(lambda t: pl.BlockSpec((1, tm, width), lambda i: where(i, t) + (0,)))

    def grouped(groups, width):
        return per_tile(lambda t: pl.BlockSpec((1, groups, tm, width),
                                               lambda i: (where(i, t)[0], 0, where(i, t)[1], 0)))

    def mod(d_model):
        return per_tile(lambda t: pl.BlockSpec(
            (1, N_MOD, d_model), lambda i: (jnp.where(where(i, t)[1] < ctx_tiles, batch, where(i, t)[0]), 0, 0)))

    return tok, grouped, mod, batch * n_tiles // OUT_TILES


def _mlstm_out(xs, mod, hf, hb, xc, z, norm_g, skip, w_out, g3, *, tm, tile_off, n_tiles, ctx_tiles):
    batch, _, d_model = xs.shape
    inner = hf.shape[2]
    ng, sk, g3r = norm_g.reshape(1, inner), skip.reshape(1, inner), g3.reshape(1, d_model)
    wout = w_out.astype(BF16)
    tok, _, mod_specs, n_steps = _readout_specs(batch, tm, tile_off, n_tiles, ctx_tiles)
    rows = OUT_TILES * tm
    rep = lambda a: [a] * OUT_TILES
    out = pl.pallas_call(
        _mlstm_out_kernel,
        grid=(n_steps,),
        in_specs=tok(d_model) + mod_specs(d_model) + tok(inner) + tok(inner) + tok(inner) + tok(inner)
                 + [_resident(ng), _resident(sk), _resident(wout), _resident(g3r)],
        out_specs=pl.BlockSpec((rows, d_model), lambda i: (i, 0)),
        out_shape=jax.ShapeDtypeStruct((batch * n_tiles * tm, d_model), F32),
        scratch_shapes=[pltpu.VMEM((rows, inner), BF16)],
        compiler_params=_params(("parallel",)),
        name="mlstm_out",
    )(*rep(xs), *rep(mod), *rep(hf), *rep(hb), *rep(xc), *rep(z), ng, sk, wout, g3r)
    return out.reshape(batch, n_tiles * tm, d_model)


def _split3(x):
    hi = x.astype(BF16)
    r1 = x - hi.astype(F32)
    mid = r1.astype(BF16)
    lo = (r1 - mid.astype(F32)).astype(BF16)
    return hi, mid, lo


def _dot3(a, pieces):
    return _dot(a, pieces[0]) + _dot(a, pieces[1]) + _dot(a, pieces[2])


def _ssd_lane_source():
    src = []
    for d in range(2):
        for g in range(SSD_GROUPS):
            for _ in range(2):
                for r in range(SSD_HPG):
                    src.append(d * SSD_GROUPS * SSD_HPG + g * SSD_HPG + r)
    return jnp.array(src, jnp.int32)


def _ssd_in_kernel(x_ref, mod_ref, g_ref, wz_ref, wx_ref, wdt_ref, cw_ref, cb_ref, dtb_ref, alog_ref,
                   tril_ref, triu_ref,
                   z_ref, xs_ref, xst_ref, bm_ref, cm_ref, dcol_ref, drow_ref, ua_ref, ub_ref, *, ctx_tiles):
    tm = x_ref.shape[1]
    groups = xs_ref.shape[1]
    is_ctx = pl.program_id(1) < ctx_tiles
    shift, scale = mod_ref[0, 3:4, :], mod_ref[0, 4:5, :]
    h = (_rms(x_ref[0], g_ref[...]) * (1.0 + scale) + shift).astype(BF16)
    half = BD_CHUNK // 2
    n_z, n_x = wz_ref.shape[1] // BD_CHUNK, wx_ref.shape[1] // BD_CHUNK
    cols = lambda c: slice(c * BD_CHUNK, (c + 1) * BD_CHUNK)
    dyn0 = jnp.minimum(pl.program_id(1), 0)
    stage = (ua_ref, ub_ref)
    stage[0][dyn0] = _dot(h, wx_ref[:, cols(0)])
    for c in range(n_x):
        sl = cols(c)
        for cz in range(c * n_z // n_x, (c + 1) * n_z // n_x):
            z_ref[0, :, cols(cz)] = _dot(h, wz_ref[:, cols(cz)]).astype(BF16)
        if c + 1 < n_x:
            stage[(c + 1) % 2][dyn0] = _dot(h, wx_ref[:, cols(c + 1)])
        xbc = _silu(_short_conv(stage[c % 2][dyn0], cw_ref, cb_ref, sl, is_ctx))
        if c < groups:
            xs_ref[0, c] = xbc.astype(BF16)
            xst_ref[0, c] = xbc.astype(BF16).T
        else:
            gi = 2 * (c - groups)
            ref, gi = (bm_ref, gi) if gi < groups else (cm_ref, gi - groups)
            ref[0, gi] = xbc[:, :half].astype(BF16)
            ref[0, gi + 1] = xbc[:, half:].astype(BF16)
    dt = _softplus(_dot(h, wdt_ref[...]) + dtb_ref[...])
    la = _split3(dt * (-LOG2_E * jnp.exp(alog_ref[...])))
    lane = lax.broadcasted_iota(jnp.int32, (1, LANES), 1)
    cs = jnp.where(lane < LANES // 2, _dot3(tril_ref[...], la), _dot3(triu_ref[...], la))
    tile = jnp.where(jnp.bitwise_and(lane, SSD_HPG) == 0, dt, cs)
    dcol_ref[0] = tile
    drow_ref[0] = tile.T


def _ssd_in(xs, mod, g, w_in, conv_w, conv_b, dt_bias, a_log, *, tm, chunk, ctx_tiles):
    batch, s_len, d_model = xs.shape
    groups, state = SSD_GROUPS, SSD_STATE
    gw = SSD_HPG * SSD_HEAD_DIM
    gn = groups * state
    conv_dim = conv_w.shape[1]
    inner = conv_dim - 2 * gn
    assert gw == BD_CHUNK and 2 * state == BD_CHUNK and inner == groups * gw
    assert 4 * dt_bias.size == 2 * LANES
    wz = w_in[:, :inner].astype(BF16)
    wx = w_in[:, inner:inner + conv_dim].astype(BF16)
    src = _ssd_lane_source()
    wdt = w_in[:, inner + conv_dim:][:, src].astype(BF16)
    dtb = dt_bias.reshape(-1)[src].reshape(1, LANES)
    alog = a_log.reshape(-1)[src].reshape(1, LANES)
    tril, triu = _chunk_sum_matrices(tm, chunk)
    cb = conv_b.reshape(1, conv_dim)
    gg = g.reshape(1, d_model)
    tok, out, mod_spec = _tile_specs(batch, tm, 0, ctx_tiles)
    kernel = functools.partial(_ssd_in_kernel, ctx_tiles=ctx_tiles)
    grp = lambda w: pl.BlockSpec((1, groups, tm, w), lambda b, j: (b, 0, j, 0))
    return pl.pallas_call(
        kernel,
        grid=(batch, s_len // tm),
        in_specs=[tok(d_model), mod_spec(d_model), _resident(gg), _resident(wz), _resident(wx), _resident(wdt),
                  _resident(conv_w), _resident(cb), _resident(dtb), _resident(alog), _resident(tril), _resident(triu)],
        out_specs=[out(inner), grp(gw), pl.BlockSpec((1, groups, gw, tm), lambda b, j: (b, 0, 0, j)),
                   grp(state), grp(state), out(LANES), pl.BlockSpec((1, LANES, tm), lambda b, j: (b, 0, j))],
        out_shape=[jax.ShapeDtypeStruct((batch, s_len, inner), BF16),
                   jax.ShapeDtypeStruct((batch, groups, s_len, gw), BF16),
                   jax.ShapeDtypeStruct((batch, groups, gw, s_len), BF16),
                   jax.ShapeDtypeStruct((batch, groups, s_len, state), BF16),
                   jax.ShapeDtypeStruct((batch, groups, s_len, state), BF16),
                   jax.ShapeDtypeStruct((batch, s_len, LANES), F32),
                   jax.ShapeDtypeStruct((batch, LANES, s_len), F32)],
        scratch_shapes=[pltpu.VMEM((1, tm, BD_CHUNK), F32), pltpu.VMEM((1, tm, BD_CHUNK), F32)],
        compiler_params=_params(("parallel", "parallel")),
        name="ssd_in",
    )(xs, mod, gg, wz, wx, wdt, conv_w, cb, dtb, alog, tril, triu)


def _ssd_scan_kernel(xf_ref, xtf_ref, bf_ref, cf_ref, dcf_ref, drf_ref,
                     xb_ref, xtb_ref, bb_ref, cb_ref, dcb_ref, drb_ref,
                     yf_ref, yb_ref, h_ref, *, chunk):
    groups, n_sub = xf_ref.shape[1], xf_ref.shape[2] // chunk
    p = SSD_HEAD_DIM

    @pl.when(pl.program_id(1) == 0)
    def _():
        h_ref[...] = jnp.zeros_like(h_ref)

    dirs = ((xf_ref, xtf_ref, bf_ref, cf_ref, dcf_ref, drf_ref, yf_ref),
            (xb_ref, xtb_ref, bb_ref, cb_ref, dcb_ref, drb_ref, yb_ref))
    masks = (_chunk_mask(chunk, reverse=False), _chunk_mask(chunk, reverse=True))

    low_half = lax.broadcasted_iota(jnp.int32, (1, 2 * p), 1) < p
    pairs = [slice(k * 2 * p, (k + 1) * 2 * p) for k in range(SSD_HPG // 2)]

    def front(g, d, sub):
        rows = slice(sub * chunk, (sub + 1) * chunk)
        bm, cm = dirs[d][2][0, g, rows, :], dirs[d][3][0, g, rows, :]
        return dict(g=g, d=d, rows=rows, bm=bm, cm=cm, cbm=_dot_nt(cm, bm))

    def middle(st):
        g, d, rows = st["g"], st["d"], st["rows"]
        _, xt_ref, _, _, dc_ref, dr_ref, _ = dirs[d]
        base = d * (LANES // 2) + g * 2 * SSD_HPG
        dcol, xt = dc_ref[0, rows, :], xt_ref[0, g, :, rows]
        end = 0 if d == 1 else chunk - 1
        mm, ecol, xw, dec = [], [], [], []
        for r in range(SSD_HPG):
            dt_row = dr_ref[0, base + r:base + r + 1, rows]
            cs_row = dr_ref[0, base + SSD_HPG + r:base + SSD_HPG + r + 1, rows]
            cs_col = _lane_pick(dcol, base + SSD_HPG + r)
            mmat = st["cbm"] * jnp.exp2(jnp.where(masks[d], cs_col - cs_row, -jnp.inf)) * dt_row
            mm.append(mmat.astype(BF16))
            ecol.append(jnp.exp2(cs_col))
            cs_end = cs_row[:, end:end + 1]
            xw.append((xt[r * p:(r + 1) * p, :].astype(F32) * (jnp.exp2(cs_end - cs_row) * dt_row)).astype(BF16))
            dec.append(jnp.exp2(cs_end))
        st.update(mm=mm, ecol=ecol, xw=jnp.concatenate(xw, axis=0), dec=dec)

    def back(st):
        g, d, rows = st["g"], st["d"], st["rows"]
        x_ref, y_ref = dirs[d][0], dirs[d][6]
        for k, pair in enumerate(pairs):
            ch = _dot_nt(st["cm"], h_ref[d, g, pair, :].astype(BF16))
            xp = x_ref[0, g, rows, pair]
            y0 = _dot(st["mm"][2 * k], xp) + st["ecol"][2 * k] * ch
            y1 = _dot(st["mm"][2 * k + 1], xp) + st["ecol"][2 * k + 1] * ch
            y_ref[0, g, rows, pair] = jnp.where(low_half, y0, y1).astype(y_ref.dtype)
        upd = _dot(st["xw"], st["bm"])
        for r in range(SSD_HPG):
            hs = slice(r * p, (r + 1) * p)
            h_ref[d, g, hs, :] = st["dec"][r] * h_ref[d, g, hs, :] + upd[hs, :]

    prev = None
    for k in range(n_sub):
        for g in range(groups):
            for d in range(2):
                cur = front(g, d, k if d == 0 else n_sub - 1 - k)
                if prev is not None:
                    back(prev)
                middle(cur)
                prev = cur
    back(prev)


def _ssd_scan(xs_in, xs_t, bm, cm, dcol, drow, *, chunk, ctx_len):
    batch, groups, s_len, gw = xs_in.shape
    state = bm.shape[3]
    span = chunk * SSD_STEP_CHUNKS
    assert s_len % span == 0 and ctx_len % span == 0
    n_steps = s_len // span
    fwd, bwd = _scan_chunk_maps(n_steps, ctx_len // span)

    def specs(order):
        return [pl.BlockSpec((1, groups, span, gw), lambda b, i: (b, 0, order(i), 0)),
                pl.BlockSpec((1, groups, gw, span), lambda b, i: (b, 0, 0, order(i))),
                pl.BlockSpec((1, groups, span, state), lambda b, i: (b, 0, order(i), 0)),
                pl.BlockSpec((1, groups, span, state), lambda b, i: (b, 0, order(i), 0)),
                pl.BlockSpec((1, span, LANES), lambda b, i: (b, order(i), 0)),
                pl.BlockSpec((1, LANES, span), lambda b, i: (b, 0, order(i)))]

    sf, sb = specs(fwd), specs(bwd)
    out = jax.ShapeDtypeStruct((batch, groups, s_len, gw), BF16)
    args = (xs_in, xs_t, bm, cm, dcol, drow)
    return pl.pallas_call(
        functools.partial(_ssd_scan_kernel, chunk=chunk),
        grid=(batch, n_steps),
        in_specs=sf + sb,
        out_specs=[sf[0], sb[0]],
        out_shape=[out, out],
        scratch_shapes=[pltpu.VMEM((2, groups, gw, state), F32)],
        compiler_params=_params(("parallel", "arbitrary")),
        name="ssd_scan",
    )(*args, *args)


def _ssd_out_kernel(*refs):
    nt = OUT_TILES
    x_refs, mod_refs, yf_refs, yb_refs, xs_refs, z_refs = (refs[k * nt:(k + 1) * nt] for k in range(6))
    dsk_ref, ng_ref, wout_ref, g3_ref, o_ref, u_ref, un_ref = refs[6 * nt:]
    groups, tm, gw = yf_refs[0].shape[1], yf_refs[0].shape[2], yf_refs[0].shape[3]
    for t in range(nt):
        rows = slice(t * tm, (t + 1) * tm)
        ssq = None
        for g in range(groups):
            sl = slice(g * gw, (g + 1) * gw)
            y = (yf_refs[t][0, g].astype(F32) + yb_refs[t][0, g].astype(F32)
                 + dsk_ref[:, sl] * xs_refs[t][0, g].astype(F32))
            u = y * _silu(z_refs[t][0, :, sl].astype(F32))
            u_ref[:, sl] = u
            part = jnp.sum(u * u, axis=-1, keepdims=True)
            ssq = part if ssq is None else ssq + part
        inv = lax.rsqrt(ssq / (groups * gw) + EPS)
        for g in range(groups):
            sl = slice(g * gw, (g + 1) * gw)
            un_ref[rows, sl] = (u_ref[:, sl] * inv * ng_ref[:, sl]).astype(BF16)
    out = None
    for g in range(groups):
        sl = slice(g * gw, (g + 1) * gw)
        part = _dot(un_ref[:, sl], wout_ref[sl, :])
        out = part if out is None else out + part
    for t in range(nt):
        rows = slice(t * tm, (t + 1) * tm)
        o_ref[rows, :] = x_refs[t][0] + mod_refs[t][0, 5:6, :] * _rms(out[rows, :], g3_ref[...])


def _ssd_out(xs, mod, yf, yb, xs_in, z, d_skip, norm_g, w_out, g3, *, tm, tile_off, n_tiles, ctx_tiles):
    batch, _, d_model = xs.shape
    groups, gw = yf.shape[1], yf.shape[3]
    inner = groups * gw
    dsk = jnp.repeat(d_skip, SSD_HEAD_DIM).reshape(1, inner)
    ng, g3r = norm_g.reshape(1, inner), g3.reshape(1, d_model)
    wout = w_out.astype(BF16)
    tok, grouped, mod_specs, n_steps = _readout_specs(batch, tm, tile_off, n_tiles, ctx_tiles)
    rows = OUT_TILES * tm
    rep = lambda a: [a] * OUT_TILES
    grp = lambda: grouped(groups, gw)
    out = pl.pallas_call(
        _ssd_out_kernel,
        grid=(n_steps,),
        in_specs=tok(d_model) + mod_specs(d_model) + grp() + grp() + grp() + tok(inner)
                 + [_resident(dsk), _resident(ng), _resident(wout), _resident(g3r)],
        out_specs=pl.BlockSpec((rows, d_model), lambda i: (i, 0)),
        out_shape=jax.ShapeDtypeStruct((batch * n_tiles * tm, d_model), F32),
        scratch_shapes=[pltpu.VMEM((tm, inner), F32), pltpu.VMEM((rows, inner), BF16)],
        compiler_params=_params(("parallel",)),
        name="ssd_out",
    )(*rep(xs), *rep(mod), *rep(yf), *rep(yb), *rep(xs_in), *rep(z), dsk, ng, wout, g3r)
    return out.reshape(batch, n_tiles * tm, d_model)


def _forward(x, c, ctx, c_ctx, ada_w, ada_b, norm_g, ffn_w_gate, ffn_w_up, ffn_w_down,
             mlstm_w_in, mlstm_conv_w, mlstm_conv_b, mlstm_w_q, mlstm_w_k, mlstm_w_v,
             mlstm_w_gates, mlstm_b_gates, mlstm_norm_g, mlstm_skip, mlstm_w_out,
             ssd_w_in, ssd_conv_w, ssd_conv_b, ssd_dt_bias, ssd_a_log, ssd_d,
             ssd_norm_g, ssd_w_out, *, tm, ml_chunk, ssd_chunk):
    batch, seq, d_model = x.shape
    ctx_len = ctx.shape[1]
    depth = ada_w.shape[0]
    assert ctx_len == tm and seq % tm == 0 and tm % GRID_W == 0 and tm % ml_chunk == 0 and tm % ssd_chunk == 0
    assert batch < MOD_ROWS
    xs = x
    ctx_tiles = ctx_len // tm
    c_all = jnp.zeros((MOD_ROWS, d_model), F32).at[:batch].set(c).at[batch].set(c_ctx)
    ffn_weights = _stack_bf16(ffn_w_gate), _stack_bf16(ffn_w_up), _stack_bf16(ffn_w_down)
    mods = _modulation(c_all, ada_w, ada_b).reshape(depth, MOD_ROWS, N_MOD, d_model)
    for i in range(depth):
        last = i == depth - 1
        mod = mods[i]
        g = norm_g[i]
        xs = _ffn(xs, mod, g[0], g[1], ffn_weights, 2 * i,
                  j0=0, tm=tm, ctx_tiles=ctx_tiles, ctx_src=ctx if i == 0 else None)
        n_tiles = xs.shape[1] // tm
        out_off = ctx_tiles if last else 0
        out_kw = dict(tm=tm, tile_off=out_off, n_tiles=n_tiles - out_off, ctx_tiles=ctx_tiles)
        j = i // 2
        if i % 2 == 0:
            q, k, v, xc, z, gates = _mlstm_in(xs, mod, g[2], mlstm_w_in[j], mlstm_conv_w[j], mlstm_conv_b[j],
                                              mlstm_w_q[j], mlstm_w_k[j], mlstm_w_v[j], mlstm_w_gates[j],
                                              mlstm_b_gates[j], tm=tm, ctx_tiles=ctx_tiles)
            hf, hb = _mlstm_scan(q, k, v, gates, chunk=ml_chunk, ctx_len=ctx_len)
            xs = _mlstm_out(xs, mod, hf, hb, xc, z, mlstm_norm_g[j], mlstm_skip[j], mlstm_w_out[j], g[3], **out_kw)
        else:
            z, xs_in, xs_t, bm, cm, dcol, drow = _ssd_in(xs, mod, g[2], ssd_w_in[j], ssd_conv_w[j], ssd_conv_b[j],
                                                         ssd_dt_bias[j], ssd_a_log[j], tm=tm, chunk=ssd_chunk,
                                                         ctx_tiles=ctx_tiles)
            yf, yb = _ssd_scan(xs_in, xs_t, bm, cm, dcol, drow, chunk=ssd_chunk, ctx_len=ctx_len)
            xs = _ssd_out(xs, mod, yf, yb, xs_in, z, ssd_d[j], ssd_norm_g[j], ssd_w_out[j], g[3], **out_kw)
        if last:
            ctx_tiles = 0
        xs = _ffn(xs, mod, g[4], g[5], ffn_weights, 2 * i + 1, j0=6, tm=tm, ctx_tiles=ctx_tiles)
    return xs


def kernel(x, c, ctx, c_ctx, ada_w, ada_b, norm_g, ffn_w_gate, ffn_w_up, ffn_w_down, mlstm_w_in, mlstm_conv_w, mlstm_conv_b, mlstm_w_q, mlstm_w_k, mlstm_w_v, mlstm_w_gates, mlstm_b_gates, mlstm_norm_g, mlstm_skip, mlstm_w_out, ssd_w_in, ssd_conv_w, ssd_conv_b, ssd_dt_bias, ssd_a_log, ssd_d, ssd_norm_g, ssd_w_out):
    return _forward(x, c, ctx, c_ctx, ada_w, ada_b, norm_g, ffn_w_gate, ffn_w_up, ffn_w_down,
                    mlstm_w_in, mlstm_conv_w, mlstm_conv_b, mlstm_w_q, mlstm_w_k, mlstm_w_v,
                    mlstm_w_gates, mlstm_b_gates, mlstm_norm_g, mlstm_skip, mlstm_w_out,
                    ssd_w_in, ssd_conv_w, ssd_conv_b, ssd_dt_bias, ssd_a_log, ssd_d,
                    ssd_norm_g, ssd_w_out, tm=TOKEN_TILE, ml_chunk=MLSTM_CHUNK, ssd_chunk=SSD_CHUNK)
```
